```python
import jax, jax.numpy as jnp
from jax import lax
import numpy as np

D_MODEL = 4096
BATCH = 4
SEQ = 2048
DEPTH = 1

GRID_W = 64
CTX_LEN = 256
HEAD_DIM = 128
N_HEADS_TOTAL = D_MODEL // HEAD_DIM
NA_HEADS = N_HEADS_TOTAL // 2
NA_WIDTH = NA_HEADS * HEAD_DIM
CONV_WIDTH = D_MODEL - NA_WIDTH
MIX_WIDTH = NA_WIDTH + CONV_WIDTH
CONV_K = 3
NA_KH = 8
NA_KW = 16
FFN_HIDDEN = -(-8 * D_MODEL // (3 * 256)) * 256
ROPE_THETA = 10000.0
EPS = 1e-6
N_MOD = 6
Q0 = 0
K0 = NA_WIDTH
V0 = 2 * NA_WIDTH
CIN0 = 3 * NA_WIDTH
GB0 = CIN0 + CONV_WIDTH
GC0 = GB0 + CONV_WIDTH
IN_COLS = GC0 + CONV_WIDTH

kernel_name = "hybrid_na_shortconv_dit_layer"


def rmsnorm(x, g):
    xf = x.astype(jnp.float32)
    y = xf * lax.rsqrt(jnp.mean(xf * xf, axis=-1, keepdims=True) + EPS)
    return (y * g.astype(jnp.float32)).astype(x.dtype)


def modulate(xn, shift, scale):
    return xn * (1 + scale) + shift


def split_heads(t):
    b, s, _ = t.shape
    return t.reshape(b, s, -1, HEAD_DIM).transpose(0, 2, 1, 3)


def merge_heads(t):
    b, h, s, d = t.shape
    return t.transpose(0, 2, 1, 3).reshape(b, s, h * d)


def rope_1d(u, pos):
    half = u.shape[-1] // 2
    freqs = ROPE_THETA ** (-jnp.arange(half, dtype=jnp.float32) / half)
    ang = pos[:, None] * freqs[None, :]
    cos, sin = jnp.cos(ang), jnp.sin(ang)
    uf = u.astype(jnp.float32)
    u1, u2 = uf[..., :half], uf[..., half:]
    return jnp.concatenate([u1 * cos - u2 * sin, u1 * sin + u2 * cos], axis=-1)


def axial_rope(t, row_pos, col_pos):
    d = t.shape[-1] // 2
    out = jnp.concatenate([rope_1d(t[..., :d], row_pos), rope_1d(t[..., d:], col_pos)], axis=-1)
    return out.astype(t.dtype)


def neighbourhood_attention(q, k, v, k_ctx, v_ctx, rpb):
    b, h, s, dh = q.shape
    rows = s // GRID_W
    kh = min(NA_KH, rows)
    kw = NA_KW
    scale = dh ** -0.5
    qg = q.reshape(b, h, rows, GRID_W, dh)
    kg = k.reshape(b, h, rows, GRID_W, dh)
    vg = v.reshape(b, h, rows, GRID_W, dh)
    row_start = jnp.clip(jnp.arange(rows) - kh // 2, 0, rows - kh)
    col = jnp.arange(GRID_W)
    col_start = jnp.clip(col - kw // 2, 0, GRID_W - kw)
    col_mask = (col[None, :] >= col_start[:, None]) & (col[None, :] < col_start[:, None] + kw)
    dc_idx = jnp.clip(col[None, :] - col[:, None] + kw - 1, 0, 2 * kw - 2)
    rpb_c = rpb[:, :, dc_idx]

    def row_block(args):
        r, rs = args
        q_r = lax.dynamic_index_in_dim(qg, r, axis=2, keepdims=False)
        k_r = lax.dynamic_slice_in_dim(kg, rs, kh, axis=2)
        v_r = lax.dynamic_slice_in_dim(vg, rs, kh, axis=2)
        s_lat = jnp.einsum('bhqd,bhrkd->bhqrk', q_r, k_r).astype(jnp.float32) * scale
        dr_idx = rs + jnp.arange(kh) - r + (NA_KH - 1)
        bias = jnp.take(rpb_c, dr_idx, axis=1).transpose(0, 2, 1, 3)
        s_lat = jnp.where(col_mask[None, None, :, None, :], s_lat + bias[None].astype(jnp.float32), -jnp.inf)
        s_ctx = jnp.einsum('bhqd,bhld->bhql', q_r, k_ctx).astype(jnp.float32) * scale
        sc = jnp.concatenate([s_lat.reshape(b, h, GRID_W, kh * GRID_W), s_ctx], axis=-1)
        p = jax.nn.softmax(sc, axis=-1).astype(v.dtype)
        p_lat = p[..., :kh * GRID_W].reshape(b, h, GRID_W, kh, GRID_W)
        return (jnp.einsum('bhqrk,bhrkd->bhqd', p_lat, v_r)
                + jnp.einsum('bhql,bhld->bhqd', p[..., kh * GRID_W:], v_ctx))

    out = lax.map(row_block, (jnp.arange(rows), row_start))
    return out.transpose(1, 2, 0, 3, 4).reshape(b, h, s, dh)


def context_attention(q, k, v):
    sc = jnp.einsum('bhqd,bhkd->bhqk', q, k).astype(jnp.float32) * (q.shape[-1] ** -0.5)
    p = jax.nn.softmax(sc, axis=-1).astype(v.dtype)
    return jnp.einsum('bhqk,bhkd->bhqd', p, v)


def short_conv(u, w):
    up = jnp.pad(u, ((0, 0), (1, 1), (0, 0)))
    return w[0] * up[:, :-2] + w[1] * up[:, 1:-1] + w[2] * up[:, 2:]


def gated_short_conv(u, gate_b, gate_c, w):
    return gate_b * short_conv(gate_c * u, w)


def mix_output(attn_o, conv_o, g_grp, w_out):
    y = jnp.concatenate([rmsnorm(attn_o, g_grp[:NA_WIDTH]), rmsnorm(conv_o, g_grp[NA_WIDTH:])], axis=-1)
    return y @ w_out


def swiglu(hn, w_gate_up, w_down):
    gu = hn @ w_gate_up
    return (jax.nn.silu(gu[..., :FFN_HIDDEN]) * gu[..., FFN_HIDDEN:]) @ w_down


def setup_inputs(seed: int = 0) -> dict:
    key = jax.random.key(seed)
    ks = jax.random.split(key, 16)
    f32 = jnp.float32

    def nrm(k, shape, scale):
        return jax.random.normal(k, shape, f32) * scale

    def gain(k, shape):
        return 1.0 + 0.02 * jax.random.normal(k, shape, f32)

    return {
        "x": nrm(ks[0], (BATCH, SEQ, D_MODEL), 1.0),
        "c": nrm(ks[1], (BATCH, D_MODEL), 1.0),
        "ctx": nrm(ks[2], (BATCH, CTX_LEN, D_MODEL), 1.0),
        "c_ctx": nrm(ks[3], (D_MODEL,), 1.0),
        "ada_w": nrm(ks[4], (DEPTH, D_MODEL, N_MOD * D_MODEL), 0.5 * D_MODEL ** -0.5),
        "ada_b": nrm(ks[5], (DEPTH, N_MOD * D_MODEL), 0.02),
        "norm1_g": gain(ks[6], (DEPTH, D_MODEL)),
        "w_in": nrm(ks[7], (DEPTH, D_MODEL, IN_COLS), D_MODEL ** -0.5),
        "conv_w": nrm(ks[8], (DEPTH, CONV_K, CONV_WIDTH), CONV_K ** -0.5),
        "rpb": nrm(ks[9], (DEPTH, NA_HEADS, 2 * NA_KH - 1, 2 * NA_KW - 1), 0.1),
        "group_norm_g": gain(ks[10], (DEPTH, MIX_WIDTH)),
        "w_out": nrm(ks[11], (DEPTH, MIX_WIDTH, D_MODEL), MIX_WIDTH ** -0.5),
        "norm2_g": gain(ks[12], (DEPTH, D_MODEL)),
        "w_gate_up": nrm(ks[13], (DEPTH, D_MODEL, 2 * FFN_HIDDEN), D_MODEL ** -0.5),
        "w_down": nrm(ks[14], (DEPTH, FFN_HIDDEN, D_MODEL), FFN_HIDDEN ** -0.5),
        "final_norm_g": gain(ks[15], (D_MODEL,)),
    }


def reference(x, c, ctx, c_ctx, ada_w, ada_b, norm1_g, w_in, conv_w, rpb, group_norm_g, w_out,
              norm2_g, w_gate_up, w_down, final_norm_g):
    b, s, _ = x.shape
    t = jnp.arange(s)
    row_pos = (t // GRID_W).astype(jnp.float32)
    col_pos = (t % GRID_W).astype(jnp.float32)
    for l in range(DEPTH):
        update_ctx = l < DEPTH - 1
        mod = (jax.nn.silu(c) @ ada_w[l] + ada_b[l]).reshape(b, N_MOD, D_MODEL)
        mod_c = (jax.nn.silu(c_ctx) @ ada_w[l] + ada_b[l]).reshape(N_MOD, D_MODEL)
        sh1, sc1, g1, sh2, sc2, g2 = [mod[:, i, None, :] for i in range(N_MOD)]
        csh1, csc1, cg1, csh2, csc2, cg2 = [mod_c[i] for i in range(N_MOD)]

        h = modulate(rmsnorm(x, norm1_g[l]), sh1, sc1)
        hc = modulate(rmsnorm(ctx, norm1_g[l]), csh1, csc1)
        p = h @ w_in[l]
        q = axial_rope(split_heads(p[..., Q0:K0]), row_pos, col_pos)
        k = axial_rope(split_heads(p[..., K0:V0]), row_pos, col_pos)
        v = split_heads(p[..., V0:CIN0])
        if update_ctx:
            pc = hc @ w_in[l]
            kv_c = pc[..., K0:CIN0]
        else:
            kv_c = hc @ w_in[l][:, K0:CIN0]
        k_c = split_heads(kv_c[..., :NA_WIDTH])
        v_c = split_heads(kv_c[..., NA_WIDTH:])
        attn_o = merge_heads(neighbourhood_attention(q, k, v, k_c, v_c, rpb[l]))
        conv_o = gated_short_conv(p[..., CIN0:GB0], p[..., GB0:GC0], p[..., GC0:], conv_w[l])
        x = x + g1 * mix_output(attn_o, conv_o, group_norm_g[l], w_out[l])

        if update_ctx:
            q_c = split_heads(pc[..., Q0:K0])
            attn_c = merge_heads(context_attention(q_c, k_c, v_c))
            conv_c = gated_short_conv(pc[..., CIN0:GB0], pc[..., GB0:GC0], pc[..., GC0:], conv_w[l])
            ctx = ctx + cg1 * mix_output(attn_c, conv_c, group_norm_g[l], w_out[l])
            ctx = ctx + cg2 * swiglu(modulate(rmsnorm(ctx, norm2_g[l]), csh2, csc2), w_gate_up[l], w_down[l])

        x = x + g2 * swiglu(modulate(rmsnorm(x, norm2_g[l]), sh2, sc2), w_gate_up[l], w_down[l])
    return rmsnorm(x, final_norm_g)
```

```python
import functools

import jax
import jax.numpy as jnp
import numpy as np
from jax import lax
from jax.experimental import pallas as pl
from jax.experimental.pallas import tpu as pltpu

F32 = jnp.float32
BF16 = jnp.bfloat16

GRID_W = 64
HEAD_DIM = 128
NA_KH = 8
NA_KW = 16
ROPE_THETA = 10000.0
EPS = 1e-6
N_MOD = 6
RPB_ROWS = 2 * NA_KH - 1
RPB_COLS = 2 * NA_KW - 1

V7X_VMEM_BYTES = 64 * 1024 * 1024
V7X_LANES = 128
VMEM_REQUEST_CAP = 60000 * 1024

ATT_QROWS = 4
ATT_KROWS = ATT_QROWS + NA_KH
ATT_QB = ATT_QROWS * GRID_W
ATT_KB = ATT_KROWS * GRID_W


def _params(semantics, vmem_bytes):
    return pltpu.CompilerParams(
        dimension_semantics=semantics,
        vmem_limit_bytes=int(min(vmem_bytes, VMEM_REQUEST_CAP)),
    )


def _nbytes(shape, dtype):
    return int(np.prod(shape)) * jnp.dtype(dtype).itemsize


def _vmem_estimate(blocks, scratch=(), temps=()):
    total = sum(2 * _nbytes(s, d) for s, d in blocks)
    total += sum(_nbytes(s, d) for s, d in scratch)
    total += sum(_nbytes(s, d) for s, d in temps)
    return total + 2 * 1024 * 1024


def _silu(x):
    return x * (1.0 / (1.0 + jnp.exp(-x)))


def _adaln_body(c_ref, w_ref, b_ref, o_ref):
    s = _silu(c_ref[...]).astype(BF16)
    w = w_ref[...].astype(BF16)
    o_ref[...] = jnp.dot(s, w, preferred_element_type=F32) + b_ref[...]


def _adaln(cond, w, b):
    rows, d = cond.shape
    n = w.shape[1]
    tn = 512
    est = _vmem_estimate(
        [((rows, d), F32), ((d, tn), F32), ((1, tn), F32), ((rows, tn), F32)],
        temps=[((d, tn), BF16)],
    )
    return pl.pallas_call(
        _adaln_body,
        grid=(n // tn,),
        in_specs=[
            pl.BlockSpec((rows, d), lambda j: (0, 0)),
            pl.BlockSpec((d, tn), lambda j: (0, j)),
            pl.BlockSpec((1, tn), lambda j: (0, j)),
        ],
        out_specs=pl.BlockSpec((rows, tn), lambda j: (0, j)),
        out_shape=jax.ShapeDtypeStruct((rows, n), F32),
        compiler_params=_params(("arbitrary",), est),
        name="adaln",
    )(cond, w, b.reshape(1, n))


def _rms(x, g):
    return x * lax.rsqrt(jnp.mean(x * x, axis=-1, keepdims=True) + EPS) * g


def _norm_mod_body(x_ref, g_ref, sh_ref, sc_ref, o_ref):
    y = _rms(x_ref[0], g_ref[...])
    o_ref[0] = (y * (1.0 + sc_ref[0]) + sh_ref[0]).astype(o_ref.dtype)


def _norm_mod(x3, g, shift, scale, ts=256):
    b, s, d = x3.shape
    per_batch = shift.shape[0] == b
    mod_map = (lambda bi, si: (bi, 0, 0)) if per_batch else (lambda bi, si: (0, 0, 0))
    est = _vmem_estimate(
        [((ts, d), F32), ((ts, d), BF16)], temps=[((ts, d), F32)] * 2)
    return pl.pallas_call(
        _norm_mod_body,
        grid=(b, s // ts),
        in_specs=[
            pl.BlockSpec((1, ts, d), lambda bi, si: (bi, si, 0)),
            pl.BlockSpec((1, d), lambda bi, si: (0, 0)),
            pl.BlockSpec((1, 1, d), mod_map),
            pl.BlockSpec((1, 1, d), mod_map),
        ],
        out_specs=pl.BlockSpec((1, ts, d), lambda bi, si: (bi, si, 0)),
        out_shape=jax.ShapeDtypeStruct((b, s, d), BF16),
        compiler_params=_params(("arbitrary", "arbitrary"), est),
        name="norm_mod",
    )(x3, g.reshape(1, d), shift, scale)


def _final_body(x_ref, f_ref, g2_ref, g_ref, o_ref):
    x2 = x_ref[0] + g2_ref[0] * f_ref[0]
    o_ref[0] = _rms(x2, g_ref[...])


def _final_norm(x3, f3, gate, g, ts=256):
    b, s, d = x3.shape
    est = _vmem_estimate([((ts, d), F32)] * 3, temps=[((ts, d), F32)] * 2)
    return pl.pallas_call(
        _final_body,
        grid=(b, s // ts),
        in_specs=[
            pl.BlockSpec((1, ts, d), lambda bi, si: (bi, si, 0)),
            pl.BlockSpec((1, ts, d), lambda bi, si: (bi, si, 0)),
            pl.BlockSpec((1, 1, d), lambda bi, si: (bi, 0, 0)),
            pl.BlockSpec((1, d), lambda bi, si: (0, 0)),
        ],
        out_specs=pl.BlockSpec((1, ts, d), lambda bi, si: (bi, si, 0)),
        out_shape=jax.ShapeDtypeStruct((b, s, d), F32),
        compiler_params=_params(("arbitrary", "arbitrary"), est),
        name="final_norm",
    )(x3, f3, gate, g.reshape(1, d))


def _rope_tile(acc, cos, sin, factor):
    lane = lax.broadcasted_iota(jnp.int32, (acc.shape[0], HEAD_DIM), 1)
    low = (lane & 32) == 0
    outs = []
    for c in range(acc.shape[1] // HEAD_DIM):
        xc = acc[:, c * HEAD_DIM:(c + 1) * HEAD_DIM]
        partner = jnp.where(low, pltpu.roll(xc, HEAD_DIM - 32, 1), pltpu.roll(xc, 32, 1))
        outs.append((xc * cos + partner * sin) * factor)
    return jnp.concatenate(outs, axis=1)


def _inproj_body(a_ref, w_ref, cos_ref, sin_ref, o_ref, wbf_ref, *, n_q_tiles, n_rope_tiles,
                 q_scale):
    j = pl.program_id(0)

    @pl.when(pl.program_id(1) == 0)
    def _():
        wbf_ref[...] = w_ref[...].astype(BF16)

    acc = jnp.dot(a_ref[...], wbf_ref[...], preferred_element_type=F32)

    @pl.when(j < n_rope_tiles)
    def _():
        factor = jnp.where(j < n_q_tiles, jnp.float32(q_scale), jnp.float32(1.0))
        o_ref[...] = _rope_tile(acc, cos_ref[...], sin_ref[...], factor).astype(o_ref.dtype)

    @pl.when(j >= n_rope_tiles)
    def _():
        o_ref[...] = acc.astype(o_ref.dtype)


def _inproj(a, w, cos, sin, na_width, tm=1024, tn=512):
    m, k = a.shape
    n = w.shape[1]
    s = cos.shape[0]
    body = functools.partial(
        _inproj_body, n_q_tiles=na_width // tn, n_rope_tiles=2 * na_width // tn,
        q_scale=HEAD_DIM ** -0.5)
    est = _vmem_estimate(
        [((tm, k), BF16), ((k, tn), F32), ((tm, HEAD_DIM), F32), ((tm, HEAD_DIM), F32),
         ((tm, tn), BF16)],
        scratch=[((k, tn), BF16)], temps=[((tm, tn), F32)] * 2)
    return pl.pallas_call(
        body,
        grid=(n // tn, m // tm),
        in_specs=[
            pl.BlockSpec((tm, k), lambda j, i: (i, 0)),
            pl.BlockSpec((k, tn), lambda j, i: (0, j)),
            pl.BlockSpec((tm, HEAD_DIM), lambda j, i: (i % (s // tm), 0)),
            pl.BlockSpec((tm, HEAD_DIM), lambda j, i: (i % (s // tm), 0)),
        ],
        out_specs=pl.BlockSpec((tm, tn), lambda j, i: (i, j)),
        out_shape=jax.ShapeDtypeStruct((m, n), BF16),
        scratch_shapes=[pltpu.VMEM((k, tn), BF16)],
        compiler_params=_params(("arbitrary", "arbitrary"), est),
        name="in_proj",
    )(a, w, cos, sin)


def _plain_mm_body(a_ref, w_ref, o_ref, wbf_ref):
    @pl.when(pl.program_id(1) == 0)
    def _():
        wbf_ref[...] = w_ref[...].astype(BF16)

    o_ref[...] = jnp.dot(a_ref[...], wbf_ref[...], preferred_element_type=F32).astype(o_ref.dtype)


def _ctx_kv(a, w, col0, ncols, tm=1024, tn=512):
    m, k = a.shape
    off = col0 // tn
    est = _vmem_estimate(
        [((tm, k), BF16), ((k, tn), F32), ((tm, tn), BF16)],
        scratch=[((k, tn), BF16)], temps=[((tm, tn), F32)])
    return pl.pallas_call(
        _plain_mm_body,
        grid=(ncols // tn, m // tm),
        in_specs=[
            pl.BlockSpec((tm, k), lambda j, i: (i, 0)),
            pl.BlockSpec((k, tn), lambda j, i: (0, j + off)),
        ],
        out_specs=pl.BlockSpec((tm, tn), lambda j, i: (i, j)),
        out_shape=jax.ShapeDtypeStruct((m, ncols), BF16),
        scratch_shapes=[pltpu.VMEM((k, tn), BF16)],
        compiler_params=_params(("arbitrary", "arbitrary"), est),
        name="ctx_kv_proj",
    )(a, w)


def _outproj_body(a_ref, w_ref, x_ref, g_ref, o_ref, wbf_ref):
    @pl.when(pl.program_id(1) == 0)
    def _():
        wbf_ref[...] = w_ref[...].astype(BF16)

    acc = jnp.dot(a_ref[...], wbf_ref[...], preferred_element_type=F32)
    o_ref[...] = x_ref[...] + g_ref[0] * acc


def _outproj(a, w, x, gate, seq, tm=1024, tn=512):
    m, k = a.shape
    n = w.shape[1]
    per = seq // tm
    est = _vmem_estimate(
        [((tm, k), BF16), ((k, tn), F32), ((tm, tn), F32), ((1, tn), F32), ((tm, tn), F32)],
        scratch=[((k, tn), BF16)], temps=[((tm, tn), F32)])
    return pl.pallas_call(
        _outproj_body,
        grid=(n // tn, m // tm),
        in_specs=[
            pl.BlockSpec((tm, k), lambda j, i: (i, 0)),
            pl.BlockSpec((k, tn), lambda j, i: (0, j)),
            pl.BlockSpec((tm, tn), lambda j, i: (i, j)),
            pl.BlockSpec((1, 1, tn), lambda j, i: (i // per, 0, j)),
        ],
        out_specs=pl.BlockSpec((tm, tn), lambda j, i: (i, j)),
        out_shape=jax.ShapeDtypeStruct((m, n), F32),
        scratch_shapes=[pltpu.VMEM((k, tn), BF16)],
        compiler_params=_params(("arbitrary", "arbitrary"), est),
        name="out_proj",
    )(a, w, x, gate)


def _ffn_up_body(a_ref, wg_ref, wu_ref, o_ref, wbf_ref, *, tn):
    @pl.when(pl.program_id(1) == 0)
    def _():
        wbf_ref[:, :tn] = wg_ref[...].astype(BF16)
        wbf_ref[:, tn:] = wu_ref[...].astype(BF16)

    acc = jnp.dot(a_ref[...], wbf_ref[...], preferred_element_type=F32)
    o_ref[...] = (_silu(acc[:, :tn]) * acc[:, tn:]).astype(o_ref.dtype)


def _ffn_up(a, w, hidden, tm=1024, tn=256):
    m, k = a.shape
    nt = hidden // tn
    est = _vmem_estimate(
        [((tm, k), BF16), ((k, tn), F32), ((k, tn), F32), ((tm, tn), BF16)],
        scratch=[((k, 2 * tn), BF16)], temps=[((tm, 2 * tn), F32)] * 2)
    return pl.pallas_call(
        functools.partial(_ffn_up_body, tn=tn),
        grid=(nt, m // tm),
        in_specs=[
            pl.BlockSpec((tm, k), lambda j, i: (i, 0)),
            pl.BlockSpec((k, tn), lambda j, i: (0, j)),
            pl.BlockSpec((k, tn), lambda j, i: (0, j + nt)),
        ],
        out_specs=pl.BlockSpec((tm, tn), lambda j, i: (i, j)),
        out_shape=jax.ShapeDtypeStruct((m, hidden), BF16),
        scratch_shapes=[pltpu.VMEM((k, 2 * tn), BF16)],
        compiler_params=_params(("arbitrary", "arbitrary"), est),
        name="ffn_up",
    )(a, w, w)


def _ffn_down_body(a_ref, at_ref, w_ref, wt_ref, o_ref):
    kk = pl.program_id(2)
    part = jnp.dot(a_ref[...], w_ref[...].astype(BF16), preferred_element_type=F32)

    @pl.when(kk == 0)
    def _():
        tail = jnp.dot(at_ref[...], wt_ref[...].astype(BF16), preferred_element_type=F32)
        o_ref[...] = part + tail

    @pl.when(kk > 0)
    def _():
        o_ref[...] += part


def _ffn_down(a, w, tm=1024, tn=2048, tk=512, tkt=256):
    m, k = a.shape
    n = w.shape[1]
    nk = k // tk
    assert nk * tk + tkt == k and (nk * tk) % tkt == 0
    tail_blk = (nk * tk) // tkt
    est = _vmem_estimate(
        [((tm, tk), BF16), ((tm, tkt), BF16), ((tk, tn), F32), ((tkt, tn), F32),
         ((tm, tn), F32)],
        temps=[((tk, tn), BF16), ((tm, tn), F32)])
    return pl.pallas_call(
        _ffn_down_body,
        grid=(m // tm, n // tn, nk),
        in_specs=[
            pl.BlockSpec((tm, tk), lambda i, j, kk: (i, kk)),
            pl.BlockSpec((tm, tkt), lambda i, j, kk: (i, tail_blk)),
            pl.BlockSpec((tk, tn), lambda i, j, kk: (kk, j)),
            pl.BlockSpec((tkt, tn), lambda i, j, kk: (tail_blk, j)),
        ],
        out_specs=pl.BlockSpec((tm, tn), lambda i, j, kk: (i, j)),
        out_shape=jax.ShapeDtypeStruct((m, n), F32),
        compiler_params=_params(("arbitrary", "arbitrary", "arbitrary"), est),
        name="ffn_down",
    )(a, a, w, w)


def _block_classes(rows):
    last_ws = rows - ATT_KROWS
    first = (0, [0] * ATT_QROWS)
    interior = (NA_KH // 2, list(range(ATT_QROWS)))
    r0 = rows - ATT_QROWS
    last = (r0 - last_ws, [rows - NA_KH - last_ws] * ATT_QROWS)
    return [first, interior, last]


def _bias_body(rpb_ref, o_ref, *, classes):
    base = pl.program_id(0) * (RPB_ROWS * RPB_COLS)
    shape = (GRID_W, 2 * GRID_W)
    lane = lax.broadcasted_iota(jnp.int32, shape, 1)
    qc = lax.broadcasted_iota(jnp.int32, shape, 0)
    kc = lane & (GRID_W - 1)
    right = lane >= GRID_W
    dc_idx = kc - qc + (NA_KW - 1)
    cs = jnp.clip(qc - NA_KW // 2, 0, GRID_W - NA_KW)
    col_ok = (kc >= cs) & (kc < cs + NA_KW)
    neg = jnp.full(shape, -jnp.inf, F32)

    pair = []
    for d in range(-1, RPB_ROWS):
        dl = min(max(d, 0), RPB_ROWS - 1)
        dr = min(max(d + 1, 0), RPB_ROWS - 1)
        acc = jnp.zeros(shape, F32)
        for dc in range(RPB_COLS):
            val = jnp.where(right, rpb_ref[base + dr * RPB_COLS + dc],
                            rpb_ref[base + dl * RPB_COLS + dc])
            acc = jnp.where(dc_idx == dc, val, acc)
        pair.append(acc)

    for ci, (off, rs_rel) in enumerate(classes):
        for i in range(ATT_QROWS):
            for jp in range(ATT_KROWS // 2):
                jl, jr = 2 * jp, 2 * jp + 1
                ok_l = rs_rel[i] <= jl < rs_rel[i] + NA_KH
                ok_r = rs_rel[i] <= jr < rs_rel[i] + NA_KH
                d = jl - i - off + (NA_KH - 1)
                if ok_l and ok_r:
                    blk = jnp.where(col_ok, pair[d + 1], neg)
                elif ok_l:
                    blk = jnp.where(col_ok & jnp.logical_not(right), pair[d + 1], neg)
                elif ok_r:
                    blk = jnp.where(col_ok & right, pair[d + 1], neg)
                else:
                    blk = neg
                o_ref[0, ci, i * GRID_W:(i + 1) * GRID_W,
                      jp * 2 * GRID_W:(jp + 1) * 2 * GRID_W] = blk


def _bias_tables(rpb, rows):
    heads = rpb.shape[0]
    classes = _block_classes(rows)
    out_block = (1, len(classes), ATT_QB, ATT_KB)
    est = _vmem_estimate([(out_block, F32)], temps=[((GRID_W, 2 * GRID_W), F32)] * 32)
    return pl.pallas_call(
        functools.partial(_bias_body, classes=classes),
        grid=(heads,),
        in_specs=[pl.BlockSpec(memory_space=pltpu.SMEM)],
        out_specs=pl.BlockSpec(out_block, lambda h: (h, 0, 0, 0)),
        out_shape=jax.ShapeDtypeStruct((heads,) + out_block[1:], F32),
        compiler_params=_params(("arbitrary",), est),
        name="bias_tables",
    )(rpb.reshape(-1))


def _attn_body(q_ref, k_ref, v_ref, kc_ref, vc_ref, tab_ref, o_ref, *, rows):
    nt = (((1,), (1,)), ((), ()))
    kc = kc_ref[...]
    vc = vc_ref[...]
    n_blocks = rows // ATT_QROWS
    for blk in range(n_blocks):
        r0 = blk * ATT_QROWS
        ws = min(max(r0 - NA_KH // 2, 0), rows - ATT_KROWS)
        cls = 0 if blk == 0 else (2 if blk == n_blocks - 1 else 1)
        q = q_ref[r0 * GRID_W:r0 * GRID_W + ATT_QB, :]
        kw = k_ref[ws * GRID_W:ws * GRID_W + ATT_KB, :]
        vw = v_ref[ws * GRID_W:ws * GRID_W + ATT_KB, :]
        s_lat = lax.dot_general(q, kw, nt, preferred_element_type=F32) + tab_ref[0, cls]
        s_ctx = lax.dot_general(q, kc, nt, preferred_element_type=F32)
        mx = jnp.maximum(jnp.max(s_lat, axis=-1, keepdims=True),
                         jnp.max(s_ctx, axis=-1, keepdims=True))
        e_lat = jnp.exp(s_lat - mx)
        e_ctx = jnp.exp(s_ctx - mx)
        denom = jnp.sum(e_lat, axis=-1, keepdims=True) + jnp.sum(e_ctx, axis=-1, keepdims=True)
        o = (jnp.dot(e_lat.astype(BF16), vw, preferred_element_type=F32)
             + jnp.dot(e_ctx.astype(BF16), vc, preferred_element_type=F32))
        o_ref[r0 * GRID_W:r0 * GRID_W + ATT_QB, :] = (o / denom).astype(o_ref.dtype)


def _attention(p, kvc, tabs, batch, seq, heads, ctx_len):
    rows = seq // GRID_W
    tab_block = (1,) + tabs.shape[1:]
    est = _vmem_estimate(
        [((seq, HEAD_DIM), BF16)] * 4 + [((ctx_len, HEAD_DIM), BF16)] * 2 + [(tab_block, F32)],
        temps=[((ATT_QB, ATT_KB + ctx_len), F32)] * 12)
    return pl.pallas_call(
        functools.partial(_attn_body, rows=rows),
        grid=(heads, batch),
        in_specs=[
            pl.BlockSpec((seq, HEAD_DIM), lambda h, b: (b, h)),
            pl.BlockSpec((seq, HEAD_DIM), lambda h, b: (b, heads + h)),
            pl.BlockSpec((seq, HEAD_DIM), lambda h, b: (b, 2 * heads + h)),
            pl.BlockSpec((ctx_len, HEAD_DIM), lambda h, b: (b, h)),
            pl.BlockSpec((ctx_len, HEAD_DIM), lambda h, b: (b, heads + h)),
            pl.BlockSpec(tab_block, lambda h, b: (h, 0, 0, 0)),
        ],
        out_specs=pl.BlockSpec((seq, HEAD_DIM), lambda h, b: (b, h)),
        out_shape=jax.ShapeDtypeStruct((batch * seq, heads * HEAD_DIM), BF16),
        compiler_params=_params(("arbitrary", "arbitrary"), est),
        name="nbr_attention",
    )(p, p, p, kvc, kvc, tabs)


HALO = 16


def _mix_body(at_ref, u_ref, b_ref, c_ref, up_ref, cp_ref, un_ref, cn_ref, cw_ref, gg_ref,
              o_ref, *, per_seq, width):
    i = pl.program_id(0)
    ts = u_ref.shape[0]
    has_prev = jnp.where(i % per_seq == 0, 0.0, 1.0).astype(F32)
    has_next = jnp.where(i % per_seq == per_seq - 1, 0.0, 1.0).astype(F32)
    g = c_ref[...].astype(F32) * u_ref[...].astype(F32)
    g_prev = (cp_ref[...].astype(F32) * up_ref[...].astype(F32))[HALO - 1:HALO, :] * has_prev
    g_next = (cn_ref[...].astype(F32) * un_ref[...].astype(F32))[0:1, :] * has_next
    row = lax.broadcasted_iota(jnp.int32, g.shape, 0)
    below = jnp.where(row == 0, g_prev, pltpu.roll(g, 1, 0))
    above = jnp.where(row == ts - 1, g_next, pltpu.roll(g, ts - 1, 0))
    conv = cw_ref[0:1, :] * below + cw_ref[1:2, :] * g + cw_ref[2:3, :] * above
    conv_o = b_ref[...].astype(F32) * conv
    o_ref[:, :width] = _rms(at_ref[...].astype(F32), gg_ref[0:1, :]).astype(o_ref.dtype)
    o_ref[:, width:] = _rms(conv_o, gg_ref[1:2, :]).astype(o_ref.dtype)


def _mix_prep(attn, p, conv_w, group_g, seq, width, col_u, ts=256):
    m = attn.shape[0]
    cu = col_u // width
    hb = ts // HALO
    n_halo = m // HALO
    est = _vmem_estimate(
        [((ts, width), BF16)] * 4 + [((HALO, width), BF16)] * 4 + [((ts, 2 * width), BF16)],
        temps=[((ts, width), F32)] * 8)
    prev_map = lambda c: (lambda i: (jnp.maximum(i * hb - 1, 0), c))
    next_map = lambda c: (lambda i: (jnp.minimum((i + 1) * hb, n_halo - 1), c))
    return pl.pallas_call(
        functools.partial(_mix_body, per_seq=seq // ts, width=width),
        grid=(m // ts,),
        in_specs=[
            pl.BlockSpec((ts, width), lambda i: (i, 0)),
            pl.BlockSpec((ts, width), lambda i: (i, cu)),
            pl.BlockSpec((ts, width), lambda i: (i, cu + 1)),
            pl.BlockSpec((ts, width), lambda i: (i, cu + 2)),
            pl.BlockSpec((HALO, width), prev_map(cu)),
            pl.BlockSpec((HALO, width), prev_map(cu + 2)),
            pl.BlockSpec((HALO, width), next_map(cu)),
            pl.BlockSpec((HALO, width), next_map(cu + 2)),
            pl.BlockSpec((3, width), lambda i: (0, 0)),
            pl.BlockSpec((2, width), lambda i: (0, 0)),
        ],
        out_specs=pl.BlockSpec((ts, 2 * width), lambda i: (i, 0)),
        out_shape=jax.ShapeDtypeStruct((m, 2 * width), BF16),
        compiler_params=_params(("arbitrary",), est),
        name="mix_prep",
    )(attn, p, p, p, p, p, p, p, conv_w, group_g.reshape(2, width))


def _rope_tables(seq):
    t = jnp.arange(seq)
    row_pos = (t // GRID_W).astype(F32)
    col_pos = (t % GRID_W).astype(F32)
    quarter = HEAD_DIM // 4
    freqs = ROPE_THETA ** (-jnp.arange(quarter, dtype=F32) / quarter)
    ang_r = row_pos[:, None] * freqs[None, :]
    ang_c = col_pos[:, None] * freqs[None, :]
    cos = jnp.concatenate([jnp.cos(ang_r)] * 2 + [jnp.cos(ang_c)] * 2, axis=-1)
    sin = jnp.concatenate([-jnp.sin(ang_r), jnp.sin(ang_r), -jnp.sin(ang_c), jnp.sin(ang_c)],
                          axis=-1)
    return cos, sin


def _layer(x, ctx, mod, mod_c, norm1_g, w_in, conv_w, rpb, group_g, w_out, norm2_g,
           w_gate_up, w_down, cos, sin):
    b, s, d = x.shape
    ctx_len = ctx.shape[1]
    heads = rpb.shape[0]
    na_width = heads * HEAD_DIM
    conv_width = conv_w.shape[1]
    hidden = w_down.shape[0]
    m = b * s
    sh1, sc1, g1, sh2, sc2, g2 = [mod[:, i:i + 1, :] for i in range(N_MOD)]
    csh1, csc1 = mod_c[None, 0:1, :], mod_c[None, 1:2, :]

    h = _norm_mod(x, norm1_g, sh1, sc1).reshape(m, d)
    hc = _norm_mod(ctx, norm1_g, csh1, csc1).reshape(b * ctx_len, d)
    p = _inproj(h, w_in, cos, sin, na_width)
    kvc = _ctx_kv(hc, w_in, na_width, 2 * na_width)
    tabs = _bias_tables(rpb, s // GRID_W)
    attn = _attention(p, kvc, tabs, b, s, heads, ctx_len)
    y = _mix_prep(attn, p, conv_w, group_g, s, conv_width, 3 * na_width)
    x1 = _outproj(y, w_out, x.reshape(m, d), g1, s)
    h2 = _norm_mod(x1.reshape(b, s, d), norm2_g, sh2, sc2).reshape(m, d)
    hid = _ffn_up(h2, w_gate_up, hidden)
    ffn = _ffn_down(hid, w_down)
    return x1.reshape(b, s, d), ffn.reshape(b, s, d), g2


def kernel(x, c, ctx, c_ctx, ada_w, ada_b, norm1_g, w_in, conv_w, rpb, group_norm_g, w_out,
           norm2_g, w_gate_up, w_down, final_norm_g):
    b, s, d = x.shape
    depth = ada_w.shape[0]
    assert depth == 1, "context-stream update between layers is not implemented"
    assert conv_w.shape[2] == rpb.shape[1] * HEAD_DIM, "equal-width branches expected"
    cos, sin = _rope_tables(s)
    pad = (-(b + 1)) % 8
    cond = jnp.concatenate([c, c_ctx[None, :], jnp.zeros((pad, d), F32)], axis=0)
    mod_all = _adaln(cond, ada_w[0], ada_b[0])
    mod = mod_all[:b].reshape(b, N_MOD, d)
    mod_c = mod_all[b].reshape(N_MOD, d)
    x1, ffn, g2 = _layer(x, ctx, mod, mod_c, norm1_g[0], w_in[0], conv_w[0], rpb[0],
                         group_norm_g[0], w_out[0], norm2_g[0], w_gate_up[0], w_down[0],
                         cos, sin)
    return _final_norm(x1, ffn, g2, final_norm_g)
```

```python
import functools

import jax
import jax.numpy as jnp
import numpy as np
from jax import lax
from jax.experimental import pallas as pl
from jax.experimental.pallas import tpu as pltpu

F32 = jnp.float32
BF16 = jnp.bfloat16

GRID_W = 64
HEAD_DIM = 128
NA_KH = 8
NA_KW = 16
ROPE_THETA = 10000.0
EPS = 1e-6
N_MOD = 6
RPB_ROWS = 2 * NA_KH - 1
RPB_COLS = 2 * NA_KW - 1

V7X_VMEM_BYTES = 64 * 1024 * 1024
V7X_LANES = 128
V7X_BF16_SUBLANES = 16
VMEM_REQUEST_CAP = 60000 * 1024

ATT_QROWS = 4
ATT_KROWS = ATT_QROWS + NA_KH
ATT_QB = ATT_QROWS * GRID_W
ATT_KB = ATT_KROWS * GRID_W


def _params(semantics, vmem_bytes):
    return pltpu.CompilerParams(
        dimension_semantics=semantics,
        vmem_limit_bytes=int(min(vmem_bytes, VMEM_REQUEST_CAP)),
    )


def _nbytes(shape, dtype):
    return int(np.prod(shape)) * jnp.dtype(dtype).itemsize


def _vmem_estimate(blocks, scratch=(), temps=()):
    total = sum(2 * _nbytes(s, d) for s, d in blocks)
    total += sum(_nbytes(s, d) for s, d in scratch)
    total += sum(_nbytes(s, d) for s, d in temps)
    return total + 2 * 1024 * 1024


def _silu(x):
    return x * (1.0 / (1.0 + jnp.exp(-x)))


def _adaln_body(c_ref, w_ref, b_ref, o_ref):
    s = _silu(c_ref[...]).astype(BF16)
    w = w_ref[...].astype(BF16)
    o_ref[...] = jnp.dot(s, w, preferred_element_type=F32) + b_ref[...]


def _adaln(cond, w, b):
    rows, d = cond.shape
    n = w.shape[1]
    tn = 512
    est = _vmem_estimate(
        [((rows, d), F32), ((d, tn), F32), ((1, tn), F32), ((rows, tn), F32)],
        temps=[((d, tn), BF16)],
    )
    return pl.pallas_call(
        _adaln_body,
        grid=(n // tn,),
        in_specs=[
            pl.BlockSpec((rows, d), lambda j: (0, 0)),
            pl.BlockSpec((d, tn), lambda j: (0, j)),
            pl.BlockSpec((1, tn), lambda j: (0, j)),
        ],
        out_specs=pl.BlockSpec((rows, tn), lambda j: (0, j)),
        out_shape=jax.ShapeDtypeStruct((rows, n), F32),
        compiler_params=_params(("arbitrary",), est),
        name="adaln",
    )(cond, w, b.reshape(1, n))


def _rms(x, g):
    return x * lax.rsqrt(jnp.mean(x * x, axis=-1, keepdims=True) + EPS) * g


def _norm_mod_body(x_ref, g_ref, sh_ref, sc_ref, o_ref):
    y = _rms(x_ref[0], g_ref[...])
    o_ref[0] = (y * (1.0 + sc_ref[0]) + sh_ref[0]).astype(o_ref.dtype)


def _norm_mod(x3, g, shift, scale, ts=256):
    b, s, d = x3.shape
    per_batch = shift.shape[0] == b
    mod_map = (lambda bi, si: (bi, 0, 0)) if per_batch else (lambda bi, si: (0, 0, 0))
    est = _vmem_estimate(
        [((ts, d), F32), ((ts, d), BF16)], temps=[((ts, d), F32)] * 2)
    return pl.pallas_call(
        _norm_mod_body,
        grid=(b, s // ts),
        in_specs=[
            pl.BlockSpec((1, ts, d), lambda bi, si: (bi, si, 0)),
            pl.BlockSpec((1, d), lambda bi, si: (0, 0)),
            pl.BlockSpec((1, 1, d), mod_map),
            pl.BlockSpec((1, 1, d), mod_map),
        ],
        out_specs=pl.BlockSpec((1, ts, d), lambda bi, si: (bi, si, 0)),
        out_shape=jax.ShapeDtypeStruct((b, s, d), BF16),
        compiler_params=_params(("arbitrary", "arbitrary"), est),
        name="norm_mod",
    )(x3, g.reshape(1, d), shift, scale)


def _final_body(x_ref, g_ref, o_ref):
    o_ref[...] = _rms(x_ref[...], g_ref[...])


def _final_norm(x2, g, ts=256):
    m, d = x2.shape
    est = _vmem_estimate([((ts, d), F32)] * 2, temps=[((ts, d), F32)] * 2)
    return pl.pallas_call(
        _final_body,
        grid=(m // ts,),
        in_specs=[
            pl.BlockSpec((ts, d), lambda i: (i, 0)),
            pl.BlockSpec((1, d), lambda i: (0, 0)),
        ],
        out_specs=pl.BlockSpec((ts, d), lambda i: (i, 0)),
        out_shape=jax.ShapeDtypeStruct((m, d), F32),
        compiler_params=_params(("arbitrary",), est),
        name="final_norm",
    )(x2, g.reshape(1, d))


def _ws_body(*refs, n_extra, nj, ni, kc, segs, epilogue):
    a_ref, w_hbm = refs[0], refs[1]
    extra = refs[2:2 + n_extra]
    o_ref = refs[2 + n_extra]
    wbf, stage, sem = refs[3 + n_extra:]
    j = pl.program_id(0)
    i = pl.program_id(1)
    t = j * ni + i

    def aligned(v, multiple):
        return v if isinstance(v, int) else pl.multiple_of(v, multiple)

    def chunk_copies(tile, chunk, slot):
        row0 = aligned(chunk * kc, kc)
        return [
            pltpu.make_async_copy(
                w_hbm.at[pl.ds(row0, kc), pl.ds(aligned(col_fn(tile), V7X_LANES), width)],
                stage.at[slot, :, pl.ds(dst, width)],
                sem.at[slot])
            for col_fn, width, dst in segs
        ]

    def start(tile, chunk, slot):
        for cp in chunk_copies(tile, chunk, slot):
            cp.start()

    def wait(tile, chunk, slot):
        for cp in chunk_copies(tile, chunk, slot):
            cp.wait()

    def round_chunk(chunk, slot, wslot):
        row0 = aligned(chunk * kc, kc)
        wbf[wslot, pl.ds(row0, kc), :] = stage[slot].astype(BF16)

    @pl.when(t == 0)
    def _():
        start(0, 0, 0)
        for c in range(ni):
            if c + 1 < ni:
                start(0, c + 1, (c + 1) % 2)
            wait(0, c, c % 2)
            round_chunk(c, c % 2, 0)
        start(min(1, nj - 1), 0, ni % 2)

    acc = jnp.dot(a_ref[...], wbf[j % 2], preferred_element_type=F32)
    epilogue(acc, j, i, extra, o_ref)

    nxt = jnp.minimum(j + 1, nj - 1)
    slot = (t + ni) % 2
    wait(nxt, i, slot)
    round_chunk(i, slot, (j + 1) % 2)

    @pl.when(t + 1 < nj * ni)
    def _():
        wrap = i + 1 == ni
        tile_next = jnp.where(wrap, jnp.minimum(j + 2, nj - 1), nxt)
        chunk_next = jnp.where(wrap, 0, i + 1)
        start(tile_next, chunk_next, 1 - slot)


def _ws_matmul(a, w, segs, nj, tm, out_shape, out_dtype, out_tn, epilogue, extra=(),
               extra_specs=(), temps=(), name="ws_matmul"):
    m, k = a.shape
    ni = m // tm
    assert ni * tm == m and k % ni == 0
    kc = k // ni
    assert kc % V7X_BF16_SUBLANES == 0
    tn = sum(width for _, width, _ in segs)
    body = functools.partial(_ws_body, n_extra=len(extra), nj=nj, ni=ni, kc=kc, segs=segs,
                             epilogue=epilogue)
    blocks = [((tm, k), BF16), ((tm, out_tn), out_dtype)]
    blocks += [(spec.block_shape, arr.dtype) for spec, arr in zip(extra_specs, extra)]
    est = _vmem_estimate(
        blocks, scratch=[((2, k, tn), BF16), ((2, kc, tn), F32)],
        temps=[((tm, tn), F32), ((tm, tn), F32), ((kc, tn), F32)] + list(temps))
    return pl.pallas_call(
        body,
        grid=(nj, ni),
        in_specs=[pl.BlockSpec((tm, k), lambda j, i: (i, 0)),
                  pl.BlockSpec(memory_space=pl.ANY)] + list(extra_specs),
        out_specs=pl.BlockSpec((tm, out_tn), lambda j, i: (i, j)),
        out_shape=jax.ShapeDtypeStruct(out_shape, out_dtype),
        scratch_shapes=[pltpu.VMEM((2, k, tn), BF16), pltpu.VMEM((2, kc, tn), F32),
                        pltpu.SemaphoreType.DMA((2,))],
        compiler_params=_params(("arbitrary", "arbitrary"), est),
        name=name,
    )(a, w, *extra)


def _rope_tile(acc, cos, sin, factor):
    lane = lax.broadcasted_iota(jnp.int32, (acc.shape[0], HEAD_DIM), 1)
    low = (lane & 32) == 0
    outs = []
    for c in range(acc.shape[1] // HEAD_DIM):
        xc = acc[:, c * HEAD_DIM:(c + 1) * HEAD_DIM]
        partner = jnp.where(low, pltpu.roll(xc, HEAD_DIM - 32, 1), pltpu.roll(xc, 32, 1))
        outs.append((xc * cos + partner * sin) * factor)
    return jnp.concatenate(outs, axis=1)


def _qk_epilogue(acc, j, i, extra, o_ref, *, n_q_tiles, q_scale):
    cos_ref, sin_ref = extra
    factor = jnp.where(j < n_q_tiles, jnp.float32(q_scale), jnp.float32(1.0))
    o_ref[...] = _rope_tile(acc, cos_ref[...], sin_ref[...], factor).astype(o_ref.dtype)


def _cast_epilogue(acc, j, i, extra, o_ref):
    o_ref[...] = acc.astype(o_ref.dtype)


def _residual_epilogue(acc, j, i, extra, o_ref):
    x_ref, g_ref = extra
    o_ref[...] = x_ref[...] + g_ref[0] * acc


def _swiglu_epilogue(acc, j, i, extra, o_ref, *, tg):
    o_ref[...] = (_silu(acc[:, :tg]) * acc[:, tg:]).astype(o_ref.dtype)


def _inproj_qk(a, w, cos, sin, na_width, tm=1024, tn=1024):
    m = a.shape[0]
    per = cos.shape[0] // tm
    table_spec = pl.BlockSpec((tm, HEAD_DIM), lambda j, i: (i % per, 0))
    return _ws_matmul(
        a, w, [(lambda tile: tile * tn, tn, 0)], 2 * na_width // tn, tm,
        (m, 2 * na_width), BF16, tn,
        functools.partial(_qk_epilogue, n_q_tiles=na_width // tn, q_scale=HEAD_DIM ** -0.5),
        extra=(cos, sin), extra_specs=(table_spec, table_spec),
        temps=[((tm, tn), F32)] * 2, name="in_proj_qk")


def _inproj_rest(a, w, col0, tm=1024, tn=1024):
    m = a.shape[0]
    ncols = w.shape[1] - col0
    return _ws_matmul(
        a, w, [(lambda tile: col0 + tile * tn, tn, 0)], ncols // tn, tm,
        (m, ncols), BF16, tn, _cast_epilogue, name="in_proj_rest")


def _outproj(a, w, x, gate, seq, tm, tn, name):
    m = a.shape[0]
    n = w.shape[1]
    per = seq // tm
    return _ws_matmul(
        a, w, [(lambda tile: tile * tn, tn, 0)], n // tn, tm, (m, n), F32, tn,
        _residual_epilogue, extra=(x, gate),
        extra_specs=(pl.BlockSpec((tm, tn), lambda j, i: (i, j)),
                     pl.BlockSpec((1, 1, tn), lambda j, i: (i // per, 0, j))),
        name=name)


def _ffn_up(a, w, hidden, tm=2048, tg=256):
    m = a.shape[0]
    segs = [(lambda tile: tile * tg, tg, 0), (lambda tile: hidden + tile * tg, tg, tg)]
    return _ws_matmul(
        a, w, segs, hidden // tg, tm, (m, hidden), BF16, tg,
        functools.partial(_swiglu_epilogue, tg=tg), temps=[((tm, tg), F32)], name="ffn_up")


def _plain_mm_body(a_ref, w_ref, o_ref, wbf_ref):
    @pl.when(pl.program_id(1) == 0)
    def _():
        wbf_ref[...] = w_ref[...].astype(BF16)

    o_ref[...] = jnp.dot(a_ref[...], wbf_ref[...], preferred_element_type=F32).astype(o_ref.dtype)


def _ctx_kv(a, w, col0, ncols, tm=1024, tn=512):
    m, k = a.shape
    off = col0 // tn
    est = _vmem_estimate(
        [((tm, k), BF16), ((k, tn), F32), ((tm, tn), BF16)],
        scratch=[((k, tn), BF16)], temps=[((tm, tn), F32)])
    return pl.pallas_call(
        _plain_mm_body,
        grid=(ncols // tn, m // tm),
        in_specs=[
            pl.BlockSpec((tm, k), lambda j, i: (i, 0)),
            pl.BlockSpec((k, tn), lambda j, i: (0, j + off)),
        ],
        out_specs=pl.BlockSpec((tm, tn), lambda j, i: (i, j)),
        out_shape=jax.ShapeDtypeStruct((m, ncols), BF16),
        scratch_shapes=[pltpu.VMEM((k, tn), BF16)],
        compiler_params=_params(("arbitrary", "arbitrary"), est),
        name="ctx_kv_proj",
    )(a, w)


def _block_classes(rows):
    last_ws = rows - ATT_KROWS
    first = (0, [0] * ATT_QROWS)
    interior = (NA_KH // 2, list(range(ATT_QROWS)))
    r0 = rows - ATT_QROWS
    last = (r0 - last_ws, [rows - NA_KH - last_ws] * ATT_QROWS)
    return [first, interior, last]


def _bias_body(rpb_ref, o_ref, *, classes):
    base = pl.program_id(0) * (RPB_ROWS * RPB_COLS)
    shape = (GRID_W, 2 * GRID_W)
    lane = lax.broadcasted_iota(jnp.int32, shape, 1)
    qc = lax.broadcasted_iota(jnp.int32, shape, 0)
    kc = lane & (GRID_W - 1)
    right = lane >= GRID_W
    dc_idx = kc - qc + (NA_KW - 1)
    cs = jnp.clip(qc - NA_KW // 2, 0, GRID_W - NA_KW)
    col_ok = (kc >= cs) & (kc < cs + NA_KW)
    neg = jnp.full(shape, -jnp.inf, F32)

    pair = []
    for d in range(-1, RPB_ROWS):
        dl = min(max(d, 0), RPB_ROWS - 1)
        dr = min(max(d + 1, 0), RPB_ROWS - 1)
        acc = jnp.zeros(shape, F32)
        for dc in range(RPB_COLS):
            val = jnp.where(right, rpb_ref[base + dr * RPB_COLS + dc],
                            rpb_ref[base + dl * RPB_COLS + dc])
            acc = jnp.where(dc_idx == dc, val, acc)
        pair.append(acc)

    for ci, (off, rs_rel) in enumerate(classes):
        for i in range(ATT_QROWS):
            for jp in range(ATT_KROWS // 2):
                jl, jr = 2 * jp, 2 * jp + 1
                ok_l = rs_rel[i] <= jl < rs_rel[i] + NA_KH
                ok_r = rs_rel[i] <= jr < rs_rel[i] + NA_KH
                d = jl - i - off + (NA_KH - 1)
                if ok_l and ok_r:
                    blk = jnp.where(col_ok, pair[d + 1], neg)
                elif ok_l:
                    blk = jnp.where(col_ok & jnp.logical_not(right), pair[d + 1], neg)
                elif ok_r:
                    blk = jnp.where(col_ok & right, pair[d + 1], neg)
                else:
                    blk = neg
                o_ref[0, ci, i * GRID_W:(i + 1) * GRID_W,
                      jp * 2 * GRID_W:(jp + 1) * 2 * GRID_W] = blk


def _bias_tables(rpb, rows):
    heads = rpb.shape[0]
    classes = _block_classes(rows)
    out_block = (1, len(classes), ATT_QB, ATT_KB)
    est = _vmem_estimate([(out_block, F32)], temps=[((GRID_W, 2 * GRID_W), F32)] * 32)
    return pl.pallas_call(
        functools.partial(_bias_body, classes=classes),
        grid=(heads,),
        in_specs=[pl.BlockSpec(memory_space=pltpu.SMEM)],
        out_specs=pl.BlockSpec(out_block, lambda h: (h, 0, 0, 0)),
        out_shape=jax.ShapeDtypeStruct((heads,) + out_block[1:], F32),
        compiler_params=_params(("arbitrary",), est),
        name="bias_tables",
    )(rpb.reshape(-1))


def _attn_body(q_ref, k_ref, v_ref, kc_ref, vc_ref, tab_ref, o_ref, *, rows):
    nt = (((1,), (1,)), ((), ()))
    kc = kc_ref[...]
    vc = vc_ref[...]
    n_blocks = rows // ATT_QROWS
    for blk in range(n_blocks):
        r0 = blk * ATT_QROWS
        ws = min(max(r0 - NA_KH // 2, 0), rows - ATT_KROWS)
        cls = 0 if blk == 0 else (2 if blk == n_blocks - 1 else 1)
        q = q_ref[r0 * GRID_W:r0 * GRID_W + ATT_QB, :]
        kw = k_ref[ws * GRID_W:ws * GRID_W + ATT_KB, :]
        vw = v_ref[ws * GRID_W:ws * GRID_W + ATT_KB, :]
        s_lat = lax.dot_general(q, kw, nt, preferred_element_type=F32) + tab_ref[0, cls]
        s_ctx = lax.dot_general(q, kc, nt, preferred_element_type=F32)
        mx = jnp.maximum(jnp.max(s_lat, axis=-1, keepdims=True),
                         jnp.max(s_ctx, axis=-1, keepdims=True))
        e_lat = jnp.exp(s_lat - mx)
        e_ctx = jnp.exp(s_ctx - mx)
        denom = jnp.sum(e_lat, axis=-1, keepdims=True) + jnp.sum(e_ctx, axis=-1, keepdims=True)
        o = (jnp.dot(e_lat.astype(BF16), vw, preferred_element_type=F32)
             + jnp.dot(e_ctx.astype(BF16), vc, preferred_element_type=F32))
        o_ref[r0 * GRID_W:r0 * GRID_W + ATT_QB, :] = (o / denom).astype(o_ref.dtype)


def _attention(qk, rest, kvc, tabs, batch, seq, heads, ctx_len):
    rows = seq // GRID_W
    tab_block = (1,) + tabs.shape[1:]
    est = _vmem_estimate(
        [((seq, HEAD_DIM), BF16)] * 4 + [((ctx_len, HEAD_DIM), BF16)] * 2 + [(tab_block, F32)],
        temps=[((ATT_QB, ATT_KB + ctx_len), F32)] * 12)
    return pl.pallas_call(
        functools.partial(_attn_body, rows=rows),
        grid=(heads, batch),
        in_specs=[
            pl.BlockSpec((seq, HEAD_DIM), lambda h, b: (b, h)),
            pl.BlockSpec((seq, HEAD_DIM), lambda h, b: (b, heads + h)),
            pl.BlockSpec((seq, HEAD_DIM), lambda h, b: (b, h)),
            pl.BlockSpec((ctx_len, HEAD_DIM), lambda h, b: (b, h)),
            pl.BlockSpec((ctx_len, HEAD_DIM), lambda h, b: (b, heads + h)),
            pl.BlockSpec(tab_block, lambda h, b: (h, 0, 0, 0)),
        ],
        out_specs=pl.BlockSpec((seq, HEAD_DIM), lambda h, b: (b, h)),
        out_shape=jax.ShapeDtypeStruct((batch * seq, heads * HEAD_DIM), BF16),
        compiler_params=_params(("arbitrary", "arbitrary"), est),
        name="nbr_attention",
    )(qk, qk, rest, kvc, kvc, tabs)


HALO = V7X_BF16_SUBLANES


def _mix_body(at_ref, u_ref, b_ref, c_ref, up_ref, cp_ref, un_ref, cn_ref, cw_ref, gg_ref,
              o_ref, *, per_seq, width):
    i = pl.program_id(0)
    ts = u_ref.shape[0]
    has_prev = jnp.where(i % per_seq == 0, 0.0, 1.0).astype(F32)
    has_next = jnp.where(i % per_seq == per_seq - 1, 0.0, 1.0).astype(F32)
    g = c_ref[...].astype(F32) * u_ref[...].astype(F32)
    g_prev = (cp_ref[...].astype(F32) * up_ref[...].astype(F32))[HALO - 1:HALO, :] * has_prev
    g_next = (cn_ref[...].astype(F32) * un_ref[...].astype(F32))[0:1, :] * has_next
    row = lax.broadcasted_iota(jnp.int32, g.shape, 0)
    below = jnp.where(row == 0, g_prev, pltpu.roll(g, 1, 0))
    above = jnp.where(row == ts - 1, g_next, pltpu.roll(g, ts - 1, 0))
    conv = cw_ref[0:1, :] * below + cw_ref[1:2, :] * g + cw_ref[2:3, :] * above
    conv_o = b_ref[...].astype(F32) * conv
    o_ref[:, :width] = _rms(at_ref[...].astype(F32), gg_ref[0:1, :]).astype(o_ref.dtype)
    o_ref[:, width:] = _rms(conv_o, gg_ref[1:2, :]).astype(o_ref.dtype)


def _mix_prep(attn, p, conv_w, group_g, seq, width, col_u, ts=256):
    m = attn.shape[0]
    cu = col_u // width
    hb = ts // HALO
    n_halo = m // HALO
    est = _vmem_estimate(
        [((ts, width), BF16)] * 4 + [((HALO, width), BF16)] * 4 + [((ts, 2 * width), BF16)],
        temps=[((ts, width), F32)] * 8)
    prev_map = lambda c: (lambda i: (jnp.maximum(i * hb - 1, 0), c))
    next_map = lambda c: (lambda i: (jnp.minimum((i + 1) * hb, n_halo - 1), c))
    return pl.pallas_call(
        functools.partial(_mix_body, per_seq=seq // ts, width=width),
        grid=(m // ts,),
        in_specs=[
            pl.BlockSpec((ts, width), lambda i: (i, 0)),
            pl.BlockSpec((ts, width), lambda i: (i, cu)),
            pl.BlockSpec((ts, width), lambda i: (i, cu + 1)),
            pl.BlockSpec((ts, width), lambda i: (i, cu + 2)),
            pl.BlockSpec((HALO, width), prev_map(cu)),
            pl.BlockSpec((HALO, width), prev_map(cu + 2)),
            pl.BlockSpec((HALO, width), next_map(cu)),
            pl.BlockSpec((HALO, width), next_map(cu + 2)),
            pl.BlockSpec((3, width), lambda i: (0, 0)),
            pl.BlockSpec((2, width), lambda i: (0, 0)),
        ],
        out_specs=pl.BlockSpec((ts, 2 * width), lambda i: (i, 0)),
        out_shape=jax.ShapeDtypeStruct((m, 2 * width), BF16),
        compiler_params=_params(("arbitrary",), est),
        name="mix_prep",
    )(attn, p, p, p, p, p, p, p, conv_w, group_g.reshape(2, width))


def _rope_tables(seq):
    t = jnp.arange(seq)
    row_pos = (t // GRID_W).astype(F32)
    col_pos = (t % GRID_W).astype(F32)
    quarter = HEAD_DIM // 4
    freqs = ROPE_THETA ** (-jnp.arange(quarter, dtype=F32) / quarter)
    ang_r = row_pos[:, None] * freqs[None, :]
    ang_c = col_pos[:, None] * freqs[None, :]
    cos = jnp.concatenate([jnp.cos(ang_r)] * 2 + [jnp.cos(ang_c)] * 2, axis=-1)
    sin = jnp.concatenate([-jnp.sin(ang_r), jnp.sin(ang_r), -jnp.sin(ang_c), jnp.sin(ang_c)],
                          axis=-1)
    return cos, sin


def _layer(x, ctx, mod, mod_c, norm1_g, w_in, conv_w, rpb, group_g, w_out, norm2_g,
           w_gate_up, w_down, cos, sin):
    b, s, d = x.shape
    ctx_len = ctx.shape[1]
    heads = rpb.shape[0]
    na_width = heads * HEAD_DIM
    conv_width = conv_w.shape[1]
    hidden = w_down.shape[0]
    m = b * s
    sh1, sc1, g1, sh2, sc2, g2 = [mod[:, i:i + 1, :] for i in range(N_MOD)]
    csh1, csc1 = mod_c[None, 0:1, :], mod_c[None, 1:2, :]

    h = _norm_mod(x, norm1_g, sh1, sc1).reshape(m, d)
    hc = _norm_mod(ctx, norm1_g, csh1, csc1).reshape(b * ctx_len, d)
    qk = _inproj_qk(h, w_in, cos, sin, na_width)
    rest = _inproj_rest(h, w_in, 2 * na_width)
    kvc = _ctx_kv(hc, w_in, na_width, 2 * na_width)
    tabs = _bias_tables(rpb, s // GRID_W)
    attn = _attention(qk, rest, kvc, tabs, b, s, heads, ctx_len)
    y = _mix_prep(attn, rest, conv_w, group_g, s, conv_width, na_width)
    x1 = _outproj(y, w_out, x.reshape(m, d), g1, s, tm=1024, tn=512, name="out_proj")
    h2 = _norm_mod(x1.reshape(b, s, d), norm2_g, sh2, sc2).reshape(m, d)
    hid = _ffn_up(h2, w_gate_up, hidden)
    return _outproj(hid, w_down, x1, g2, s, tm=512, tn=512, name="ffn_down")


def kernel(x, c, ctx, c_ctx, ada_w, ada_b, norm1_g, w_in, conv_w, rpb, group_norm_g, w_out,
           norm2_g, w_gate_up, w_down, final_norm_g):
    b, s, d = x.shape
    depth = ada_w.shape[0]
    assert depth == 1, "context-stream update between layers is not implemented"
    assert conv_w.shape[2] == rpb.shape[1] * HEAD_DIM, "equal-width branches expected"
    cos, sin = _rope_tables(s)
    pad = (-(b + 1)) % 8
    cond = jnp.concatenate([c, c_ctx[None, :], jnp.zeros((pad, d), F32)], axis=0)
    mod_all = _adaln(cond, ada_w[0], ada_b[0])
    mod = mod_all[:b].reshape(b, N_MOD, d)
    mod_c = mod_all[b].reshape(N_MOD, d)
    x2 = _layer(x, ctx, mod, mod_c, norm1_g[0], w_in[0], conv_w[0], rpb[0], group_norm_g[0],
                w_out[0], norm2_g[0], w_gate_up[0], w_down[0], cos, sin)
    return _final_norm(x2, final_norm_g).reshape(b, s, d)
```

```python
import functools

import jax
import jax.numpy as jnp
import numpy as np
from jax import lax
from jax.experimental import pallas as pl
from jax.experimental.pallas import tpu as pltpu

F32 = jnp.float32
BF16 = jnp.bfloat16

GRID_W = 64
HEAD_DIM = 128
NA_KH = 8
NA_KW = 16
ROPE_THETA = 10000.0
EPS = 1e-6
N_MOD = 6
RPB_ROWS = 2 * NA_KH - 1
RPB_COLS = 2 * NA_KW - 1

V7X_VMEM_BYTES = 64 * 1024 * 1024
V7X_LANES = 128
V7X_BF16_SUBLANES = 16
VMEM_REQUEST_CAP = 60000 * 1024

ATT_QROWS = 4
ATT_KROWS = ATT_QROWS + NA_KH
ATT_QB = ATT_QROWS * GRID_W
ATT_KB = ATT_KROWS * GRID_W


def _params(semantics, vmem_bytes):
    return pltpu.CompilerParams(
        dimension_semantics=semantics,
        vmem_limit_bytes=int(min(vmem_bytes, VMEM_REQUEST_CAP)),
    )


def _nbytes(shape, dtype):
    return int(np.prod(shape)) * jnp.dtype(dtype).itemsize


def _vmem_estimate(blocks, scratch=(), temps=()):
    total = sum(2 * _nbytes(s, d) for s, d in blocks)
    total += sum(_nbytes(s, d) for s, d in scratch)
    total += sum(_nbytes(s, d) for s, d in temps)
    return total + 2 * 1024 * 1024


def _silu(x):
    return x * (1.0 / (1.0 + jnp.exp(-x)))


def _adaln_body(c_ref, w_ref, b_ref, o_ref):
    s = _silu(c_ref[...]).astype(BF16)
    w = w_ref[...].astype(BF16)
    o_ref[...] = jnp.dot(s, w, preferred_element_type=F32) + b_ref[...]


def _adaln(cond, w, b):
    rows, d = cond.shape
    n = w.shape[1]
    tn = 512
    est = _vmem_estimate(
        [((rows, d), F32), ((d, tn), F32), ((1, tn), F32), ((rows, tn), F32)],
        temps=[((d, tn), BF16)],
    )
    return pl.pallas_call(
        _adaln_body,
        grid=(n // tn,),
        in_specs=[
            pl.BlockSpec((rows, d), lambda j: (0, 0)),
            pl.BlockSpec((d, tn), lambda j: (0, j)),
            pl.BlockSpec((1, tn), lambda j: (0, j)),
        ],
        out_specs=pl.BlockSpec((rows, tn), lambda j: (0, j)),
        out_shape=jax.ShapeDtypeStruct((rows, n), F32),
        compiler_params=_params(("arbitrary",), est),
        name="adaln",
    )(cond, w, b.reshape(1, n))


def _rms(x, g):
    return x * lax.rsqrt(jnp.mean(x * x, axis=-1, keepdims=True) + EPS) * g


def _norm_mod_body(x_ref, g_ref, sh_ref, sc_ref, o_ref, inv_ref):
    inv_ref[...] = lax.rsqrt(jnp.mean(jnp.square(x_ref[0]), axis=-1, keepdims=True) + EPS)
    gain = g_ref[...] * (1.0 + sc_ref[0])
    o_ref[0] = ((x_ref[0] * inv_ref[...]) * gain + sh_ref[0]).astype(o_ref.dtype)


def _norm_mod(x3, g, shift, scale, ts=512):
    b, s, d = x3.shape
    ts = min(ts, s)
    per_batch = shift.shape[0] == b
    mod_map = (lambda bi, si: (bi, 0, 0)) if per_batch else (lambda bi, si: (0, 0, 0))
    est = _vmem_estimate(
        [((ts, d), F32), ((ts, d), BF16)], temps=[((ts, d), F32)] * 3)
    return pl.pallas_call(
        _norm_mod_body,
        grid=(b, s // ts),
        in_specs=[
            pl.BlockSpec((1, ts, d), lambda bi, si: (bi, si, 0)),
            pl.BlockSpec((1, d), lambda bi, si: (0, 0)),
            pl.BlockSpec((1, 1, d), mod_map),
            pl.BlockSpec((1, 1, d), mod_map),
        ],
        out_specs=pl.BlockSpec((1, ts, d), lambda bi, si: (bi, si, 0)),
        out_shape=jax.ShapeDtypeStruct((b, s, d), BF16),
        scratch_shapes=[pltpu.VMEM((ts, 1), F32)],
        compiler_params=_params(("arbitrary", "arbitrary"), est),
        name="norm_mod",
    )(x3, g.reshape(1, d), shift, scale)


def _final_body(x_ref, g_ref, o_ref, inv_ref):
    inv_ref[...] = lax.rsqrt(jnp.mean(jnp.square(x_ref[...]), axis=-1, keepdims=True) + EPS)
    o_ref[...] = (x_ref[...] * inv_ref[...]) * g_ref[...]


def _final_norm(x2, g, ts=512):
    m, d = x2.shape
    est = _vmem_estimate([((ts, d), F32)] * 2, temps=[((ts, d), F32)] * 3)
    return pl.pallas_call(
        _final_body,
        grid=(m // ts,),
        in_specs=[
            pl.BlockSpec((ts, d), lambda i: (i, 0)),
            pl.BlockSpec((1, d), lambda i: (0, 0)),
        ],
        out_specs=pl.BlockSpec((ts, d), lambda i: (i, 0)),
        out_shape=jax.ShapeDtypeStruct((m, d), F32),
        scratch_shapes=[pltpu.VMEM((ts, 1), F32)],
        compiler_params=_params(("arbitrary",), est),
        name="final_norm",
    )(x2, g.reshape(1, d))


def _ws_body(*refs, n_extra, nj, ni, kc, segs, epilogue, row_splits):
    a_ref, w_hbm = refs[0], refs[1]
    extra = refs[2:2 + n_extra]
    o_ref = refs[2 + n_extra]
    wbf, stage, sem = refs[3 + n_extra:]
    j = pl.program_id(0)
    i = pl.program_id(1)
    t = j * ni + i

    def aligned(v, multiple):
        return v if isinstance(v, int) else pl.multiple_of(v, multiple)

    def chunk_copies(tile, chunk, slot):
        row0 = aligned(chunk * kc, kc)
        return [
            pltpu.make_async_copy(
                w_hbm.at[pl.ds(row0, kc), pl.ds(aligned(col_fn(tile), V7X_LANES), width)],
                stage.at[slot, :, pl.ds(dst, width)],
                sem.at[slot])
            for col_fn, width, dst in segs
        ]

    def start(tile, chunk, slot):
        for cp in chunk_copies(tile, chunk, slot):
            cp.start()

    def wait(tile, chunk, slot):
        for cp in chunk_copies(tile, chunk, slot):
            cp.wait()

    def round_chunk(chunk, slot, wslot):
        row0 = aligned(chunk * kc, kc)
        wbf[wslot, pl.ds(row0, kc), :] = stage[slot].astype(BF16)

    @pl.when(t == 0)
    def _():
        start(0, 0, 0)
        for c in range(ni):
            if c + 1 < ni:
                start(0, c + 1, (c + 1) % 2)
            wait(0, c, c % 2)
            round_chunk(c, c % 2, 0)
        start(min(1, nj - 1), 0, ni % 2)

    tm = a_ref.shape[0]
    for r in range(row_splits):
        rows = slice(r * tm // row_splits, (r + 1) * tm // row_splits)
        acc = jnp.dot(a_ref[rows, :], wbf[j % 2], preferred_element_type=F32)
        epilogue(acc, rows, j, extra, o_ref)

    nxt = jnp.minimum(j + 1, nj - 1)
    slot = (t + ni) % 2
    wait(nxt, i, slot)
    round_chunk(i, slot, (j + 1) % 2)

    @pl.when(t + 1 < nj * ni)
    def _():
        wrap = i + 1 == ni
        tile_next = jnp.where(wrap, jnp.minimum(j + 2, nj - 1), nxt)
        chunk_next = jnp.where(wrap, 0, i + 1)
        start(tile_next, chunk_next, 1 - slot)


def _ws_matmul(a, w, segs, nj, tm, out_shape, out_dtype, out_tn, epilogue, extra=(),
               extra_specs=(), temps=(), row_splits=1, name="ws_matmul"):
    m, k = a.shape
    ni = m // tm
    assert ni * tm == m and k % ni == 0
    kc = k // ni
    assert kc % V7X_BF16_SUBLANES == 0 and tm % (row_splits * V7X_BF16_SUBLANES) == 0
    tn = sum(width for _, width, _ in segs)
    body = functools.partial(_ws_body, n_extra=len(extra), nj=nj, ni=ni, kc=kc, segs=segs,
                             epilogue=epilogue, row_splits=row_splits)
    blocks = [((tm, k), BF16), ((tm, out_tn), out_dtype)]
    blocks += [(spec.block_shape, arr.dtype) for spec, arr in zip(extra_specs, extra)]
    est = _vmem_estimate(
        blocks, scratch=[((2, k, tn), BF16), ((2, kc, tn), F32)],
        temps=[((tm, tn), F32), ((tm, tn), F32), ((kc, tn), F32)] + list(temps))
    return pl.pallas_call(
        body,
        grid=(nj, ni),
        in_specs=[pl.BlockSpec((tm, k), lambda j, i: (i, 0)),
                  pl.BlockSpec(memory_space=pl.ANY)] + list(extra_specs),
        out_specs=pl.BlockSpec((tm, out_tn), lambda j, i: (i, j)),
        out_shape=jax.ShapeDtypeStruct(out_shape, out_dtype),
        scratch_shapes=[pltpu.VMEM((2, k, tn), BF16), pltpu.VMEM((2, kc, tn), F32),
                        pltpu.SemaphoreType.DMA((2,))],
        compiler_params=_params(("arbitrary", "arbitrary"), est),
        name=name,
    )(a, w, *extra)


def _rope_tile(acc, cos, sin, factor):
    lane = lax.broadcasted_iota(jnp.int32, (acc.shape[0], HEAD_DIM), 1)
    low = (lane & 32) == 0
    outs = []
    for c in range(acc.shape[1] // HEAD_DIM):
        xc = acc[:, c * HEAD_DIM:(c + 1) * HEAD_DIM]
        partner = jnp.where(low, pltpu.roll(xc, HEAD_DIM - 32, 1), pltpu.roll(xc, 32, 1))
        outs.append((xc * cos + partner * sin) * factor)
    return jnp.concatenate(outs, axis=1)


def _qk_epilogue(acc, rows, j, extra, o_ref, *, n_q_tiles, q_scale):
    cos_ref, sin_ref = extra
    factor = jnp.where(j < n_q_tiles, jnp.float32(q_scale), jnp.float32(1.0))
    roped = _rope_tile(acc, cos_ref[rows, :], sin_ref[rows, :], factor)
    o_ref[rows, :] = roped.astype(o_ref.dtype)


def _cast_epilogue(acc, rows, j, extra, o_ref):
    o_ref[rows, :] = acc.astype(o_ref.dtype)


def _residual_epilogue(acc, rows, j, extra, o_ref):
    x_ref, g_ref = extra
    o_ref[rows, :] = x_ref[rows, :] + g_ref[0] * acc


def _swiglu_epilogue(acc, rows, j, extra, o_ref, *, tg):
    o_ref[rows, :] = (_silu(acc[:, :tg]) * acc[:, tg:]).astype(o_ref.dtype)


def _inproj_qk(a, w, cos, sin, na_width, tm=1024, tn=1024):
    m = a.shape[0]
    per = cos.shape[0] // tm
    table_spec = pl.BlockSpec((tm, HEAD_DIM), lambda j, i: (i % per, 0))
    return _ws_matmul(
        a, w, [(lambda tile: tile * tn, tn, 0)], 2 * na_width // tn, tm,
        (m, 2 * na_width), BF16, tn,
        functools.partial(_qk_epilogue, n_q_tiles=na_width // tn, q_scale=HEAD_DIM ** -0.5),
        extra=(cos, sin), extra_specs=(table_spec, table_spec),
        temps=[((tm, tn), F32)] * 2, row_splits=4, name="in_proj_qk")


def _inproj_rest(a, w, col0, tm=1024, tn=1024):
    m = a.shape[0]
    ncols = w.shape[1] - col0
    return _ws_matmul(
        a, w, [(lambda tile: col0 + tile * tn, tn, 0)], ncols // tn, tm,
        (m, ncols), BF16, tn, _cast_epilogue, row_splits=2, name="in_proj_rest")


def _outproj(a, w, x, gate, seq, tm, tn, name):
    m = a.shape[0]
    n = w.shape[1]
    per = seq // tm
    return _ws_matmul(
        a, w, [(lambda tile: tile * tn, tn, 0)], n // tn, tm, (m, n), F32, tn,
        _residual_epilogue, extra=(x, gate),
        extra_specs=(pl.BlockSpec((tm, tn), lambda j, i: (i, j)),
                     pl.BlockSpec((1, 1, tn), lambda j, i: (i // per, 0, j))),
        row_splits=2, name=name)


def _ffn_up(a, w, hidden, tm=2048, tg=256):
    m = a.shape[0]
    segs = [(lambda tile: tile * tg, tg, 0), (lambda tile: hidden + tile * tg, tg, tg)]
    return _ws_matmul(
        a, w, segs, hidden // tg, tm, (m, hidden), BF16, tg,
        functools.partial(_swiglu_epilogue, tg=tg), temps=[((tm, tg), F32)], row_splits=4,
        name="ffn_up")


def _plain_mm_body(a_ref, w_ref, o_ref, wbf_ref):
    @pl.when(pl.program_id(1) == 0)
    def _():
        wbf_ref[...] = w_ref[...].astype(BF16)

    o_ref[...] = jnp.dot(a_ref[...], wbf_ref[...], preferred_element_type=F32).astype(o_ref.dtype)


def _ctx_kv(a, w, col0, ncols, tm=1024, tn=512):
    m, k = a.shape
    off = col0 // tn
    est = _vmem_estimate(
        [((tm, k), BF16), ((k, tn), F32), ((tm, tn), BF16)],
        scratch=[((k, tn), BF16)], temps=[((tm, tn), F32)])
    return pl.pallas_call(
        _plain_mm_body,
        grid=(ncols // tn, m // tm),
        in_specs=[
            pl.BlockSpec((tm, k), lambda j, i: (i, 0)),
            pl.BlockSpec((k, tn), lambda j, i: (0, j + off)),
        ],
        out_specs=pl.BlockSpec((tm, tn), lambda j, i: (i, j)),
        out_shape=jax.ShapeDtypeStruct((m, ncols), BF16),
        scratch_shapes=[pltpu.VMEM((k, tn), BF16)],
        compiler_params=_params(("arbitrary", "arbitrary"), est),
        name="ctx_kv_proj",
    )(a, w)


def _block_classes(rows):
    last_ws = rows - ATT_KROWS
    first = (0, [0] * ATT_QROWS)
    interior = (NA_KH // 2, list(range(ATT_QROWS)))
    r0 = rows - ATT_QROWS
    last = (r0 - last_ws, [rows - NA_KH - last_ws] * ATT_QROWS)
    return [first, interior, last]


def _bias_body(rpb_ref, o_ref, *, classes):
    base = pl.program_id(0) * (RPB_ROWS * RPB_COLS)
    shape = (GRID_W, 2 * GRID_W)
    lane = lax.broadcasted_iota(jnp.int32, shape, 1)
    qc = lax.broadcasted_iota(jnp.int32, shape, 0)
    kc = lane & (GRID_W - 1)
    right = lane >= GRID_W
    dc_idx = kc - qc + (NA_KW - 1)
    cs = jnp.clip(qc - NA_KW // 2, 0, GRID_W - NA_KW)
    col_ok = (kc >= cs) & (kc < cs + NA_KW)
    neg = jnp.full(shape, -jnp.inf, F32)

    pair = []
    for d in range(-1, RPB_ROWS):
        dl = min(max(d, 0), RPB_ROWS - 1)
        dr = min(max(d + 1, 0), RPB_ROWS - 1)
        acc = jnp.zeros(shape, F32)
        for dc in range(RPB_COLS):
            val = jnp.where(right, rpb_ref[base + dr * RPB_COLS + dc],
                            rpb_ref[base + dl * RPB_COLS + dc])
            acc = jnp.where(dc_idx == dc, val, acc)
        pair.append(acc)

    for ci, (off, rs_rel) in enumerate(classes):
        for i in range(ATT_QROWS):
            for jp in range(ATT_KROWS // 2):
                jl, jr = 2 * jp, 2 * jp + 1
                ok_l = rs_rel[i] <= jl < rs_rel[i] + NA_KH
                ok_r = rs_rel[i] <= jr < rs_rel[i] + NA_KH
                d = jl - i - off + (NA_KH - 1)
                if ok_l and ok_r:
                    blk = jnp.where(col_ok, pair[d + 1], neg)
                elif ok_l:
                    blk = jnp.where(col_ok & jnp.logical_not(right), pair[d + 1], neg)
                elif ok_r:
                    blk = jnp.where(col_ok & right, pair[d + 1], neg)
                else:
                    blk = neg
                o_ref[0, ci, i * GRID_W:(i + 1) * GRID_W,
                      jp * 2 * GRID_W:(jp + 1) * 2 * GRID_W] = blk


def _bias_tables(rpb, rows):
    heads = rpb.shape[0]
    classes = _block_classes(rows)
    out_block = (1, len(classes), ATT_QB, ATT_KB)
    est = _vmem_estimate([(out_block, F32)], temps=[((GRID_W, 2 * GRID_W), F32)] * 32)
    return pl.pallas_call(
        functools.partial(_bias_body, classes=classes),
        grid=(heads,),
        in_specs=[pl.BlockSpec(memory_space=pltpu.SMEM)],
        out_specs=pl.BlockSpec(out_block, lambda h: (h, 0, 0, 0)),
        out_shape=jax.ShapeDtypeStruct((heads,) + out_block[1:], F32),
        compiler_params=_params(("arbitrary",), est),
        name="bias_tables",
    )(rpb.reshape(-1))


def _attn_body(q_ref, k_ref, v_ref, kc_ref, vc_ref, tab_ref, o_ref, *, rows):
    nt = (((1,), (1,)), ((), ()))
    kc = kc_ref[...]
    vc = vc_ref[...]
    n_blocks = rows // ATT_QROWS

    def window(blk):
        r0 = blk * ATT_QROWS
        ws = min(max(r0 - NA_KH // 2, 0), rows - ATT_KROWS)
        return r0 * GRID_W, ws * GRID_W

    def scores(blk):
        q0, k0 = window(blk)
        cls = 0 if blk == 0 else (2 if blk == n_blocks - 1 else 1)
        q = q_ref[q0:q0 + ATT_QB, :]
        s_lat = lax.dot_general(q, k_ref[k0:k0 + ATT_KB, :], nt, preferred_element_type=F32)
        s_ctx = lax.dot_general(q, kc, nt, preferred_element_type=F32)
        return s_lat + tab_ref[0, cls], s_ctx

    def probs(s_lat, s_ctx):
        mx = jnp.maximum(jnp.max(s_lat, axis=-1, keepdims=True),
                         jnp.max(s_ctx, axis=-1, keepdims=True))
        e_lat = jnp.exp(s_lat - mx)
        e_ctx = jnp.exp(s_ctx - mx)
        denom = jnp.sum(e_lat, axis=-1, keepdims=True) + jnp.sum(e_ctx, axis=-1, keepdims=True)
        return e_lat.astype(BF16), e_ctx.astype(BF16), denom

    def finish(blk, e_lat, e_ctx, denom):
        q0, k0 = window(blk)
        o = (jnp.dot(e_lat, v_ref[k0:k0 + ATT_KB, :], preferred_element_type=F32)
             + jnp.dot(e_ctx, vc, preferred_element_type=F32))
        o_ref[q0:q0 + ATT_QB, :] = (o / denom).astype(o_ref.dtype)

    nxt = scores(0)
    pending = None
    for blk in range(n_blocks):
        cur = nxt
        if blk + 1 < n_blocks:
            nxt = scores(blk + 1)
        p = probs(*cur)
        if pending is not None:
            finish(blk - 1, *pending)
        pending = p
    finish(n_blocks - 1, *pending)


def _attention(qk, rest, kvc, tabs, batch, seq, heads, ctx_len):
    rows = seq // GRID_W
    tab_block = (1,) + tabs.shape[1:]
    est = _vmem_estimate(
        [((seq, HEAD_DIM), BF16)] * 4 + [((ctx_len, HEAD_DIM), BF16)] * 2 + [(tab_block, F32)],
        temps=[((ATT_QB, ATT_KB + ctx_len), F32)] * 12)
    return pl.pallas_call(
        functools.partial(_attn_body, rows=rows),
        grid=(heads, batch),
        in_specs=[
            pl.BlockSpec((seq, HEAD_DIM), lambda h, b: (b, h)),
            pl.BlockSpec((seq, HEAD_DIM), lambda h, b: (b, heads + h)),
            pl.BlockSpec((seq, HEAD_DIM), lambda h, b: (b, h)),
            pl.BlockSpec((ctx_len, HEAD_DIM), lambda h, b: (b, h)),
            pl.BlockSpec((ctx_len, HEAD_DIM), lambda h, b: (b, heads + h)),
            pl.BlockSpec(tab_block, lambda h, b: (h, 0, 0, 0)),
        ],
        out_specs=pl.BlockSpec((seq, HEAD_DIM), lambda h, b: (b, h)),
        out_shape=jax.ShapeDtypeStruct((batch * seq, heads * HEAD_DIM), BF16),
        compiler_params=_params(("arbitrary", "arbitrary"), est),
        name="nbr_attention",
    )(qk, qk, rest, kvc, kvc, tabs)


HALO = V7X_BF16_SUBLANES


def _mix_body(at_ref, u_ref, b_ref, c_ref, up_ref, cp_ref, un_ref, cn_ref, cw_ref, gg_ref,
              o_ref, *, per_seq, width):
    i = pl.program_id(0)
    ts = u_ref.shape[0]
    has_prev = jnp.where(i % per_seq == 0, 0.0, 1.0).astype(F32)
    has_next = jnp.where(i % per_seq == per_seq - 1, 0.0, 1.0).astype(F32)
    g = c_ref[...].astype(F32) * u_ref[...].astype(F32)
    g_prev = (cp_ref[...].astype(F32) * up_ref[...].astype(F32))[HALO - 1:HALO, :] * has_prev
    g_next = (cn_ref[...].astype(F32) * un_ref[...].astype(F32))[0:1, :] * has_next
    row = lax.broadcasted_iota(jnp.int32, g.shape, 0)
    below = jnp.where(row == 0, g_prev, pltpu.roll(g, 1, 0))
    above = jnp.where(row == ts - 1, g_next, pltpu.roll(g, ts - 1, 0))
    conv = cw_ref[0:1, :] * below + cw_ref[1:2, :] * g + cw_ref[2:3, :] * above
    conv_o = b_ref[...].astype(F32) * conv
    o_ref[:, :width] = _rms(at_ref[...].astype(F32), gg_ref[0:1, :]).astype(o_ref.dtype)
    o_ref[:, width:] = _rms(conv_o, gg_ref[1:2, :]).astype(o_ref.dtype)


def _mix_prep(attn, p, conv_w, group_g, seq, width, col_u, ts=256):
    m = attn.shape[0]
    cu = col_u // width
    hb = ts // HALO
    n_halo = m // HALO
    est = _vmem_estimate(
        [((ts, width), BF16)] * 4 + [((HALO, width), BF16)] * 4 + [((ts, 2 * width), BF16)],
        temps=[((ts, width), F32)] * 8)
    prev_map = lambda c: (lambda i: (jnp.maximum(i * hb - 1, 0), c))
    next_map = lambda c: (lambda i: (jnp.minimum((i + 1) * hb, n_halo - 1), c))
    return pl.pallas_call(
        functools.partial(_mix_body, per_seq=seq // ts, width=width),
        grid=(m // ts,),
        in_specs=[
            pl.BlockSpec((ts, width), lambda i: (i, 0)),
            pl.BlockSpec((ts, width), lambda i: (i, cu)),
            pl.BlockSpec((ts, width), lambda i: (i, cu + 1)),
            pl.BlockSpec((ts, width), lambda i: (i, cu + 2)),
            pl.BlockSpec((HALO, width), prev_map(cu)),
            pl.BlockSpec((HALO, width), prev_map(cu + 2)),
            pl.BlockSpec((HALO, width), next_map(cu)),
            pl.BlockSpec((HALO, width), next_map(cu + 2)),
            pl.BlockSpec((3, width), lambda i: (0, 0)),
            pl.BlockSpec((2, width), lambda i: (0, 0)),
        ],
        out_specs=pl.BlockSpec((ts, 2 * width), lambda i: (i, 0)),
        out_shape=jax.ShapeDtypeStruct((m, 2 * width), BF16),
        compiler_params=_params(("arbitrary",), est),
        name="mix_prep",
    )(attn, p, p, p, p, p, p, p, conv_w, group_g.reshape(2, width))


def _rope_tables(seq):
    t = jnp.arange(seq)
    row_pos = (t // GRID_W).astype(F32)
    col_pos = (t % GRID_W).astype(F32)
    quarter = HEAD_DIM // 4
    freqs = ROPE_THETA ** (-jnp.arange(quarter, dtype=F32) / quarter)
    ang_r = row_pos[:, None] * freqs[None, :]
    ang_c = col_pos[:, None] * freqs[None, :]
    cos = jnp.concatenate([jnp.cos(ang_r)] * 2 + [jnp.cos(ang_c)] * 2, axis=-1)
    sin = jnp.concatenate([-jnp.sin(ang_r), jnp.sin(ang_r), -jnp.sin(ang_c), jnp.sin(ang_c)],
                          axis=-1)
    return cos, sin


def _layer(x, ctx, mod, mod_c, norm1_g, w_in, conv_w, rpb, group_g, w_out, norm2_g,
           w_gate_up, w_down, cos, sin):
    b, s, d = x.shape
    ctx_len = ctx.shape[1]
    heads = rpb.shape[0]
    na_width = heads * HEAD_DIM
    conv_width = conv_w.shape[1]
    hidden = w_down.shape[0]
    m = b * s
    sh1, sc1, g1, sh2, sc2, g2 = [mod[:, i:i + 1, :] for i in range(N_MOD)]
    csh1, csc1 = mod_c[None, 0:1, :], mod_c[None, 1:2, :]

    h = _norm_mod(x, norm1_g, sh1, sc1).reshape(m, d)
    hc = _norm_mod(ctx, norm1_g, csh1, csc1).reshape(b * ctx_len, d)
    qk = _inproj_qk(h, w_in, cos, sin, na_width)
    rest = _inproj_rest(h, w_in, 2 * na_width)
    kvc = _ctx_kv(hc, w_in, na_width, 2 * na_width)
    tabs = _bias_tables(rpb, s // GRID_W)
    attn = _attention(qk, rest, kvc, tabs, b, s, heads, ctx_len)
    y = _mix_prep(attn, rest, conv_w, group_g, s, conv_width, na_width)
    x1 = _outproj(y, w_out, x.reshape(m, d), g1, s, tm=512, tn=1024, name="out_proj")
    h2 = _norm_mod(x1.reshape(b, s, d), norm2_g, sh2, sc2).reshape(m, d)
    hid = _ffn_up(h2, w_gate_up, hidden)
    return _outproj(hid, w_down, x1, g2, s, tm=512, tn=512, name="ffn_down")


def kernel(x, c, ctx, c_ctx, ada_w, ada_b, norm1_g, w_in, conv_w, rpb, group_norm_g, w_out,
           norm2_g, w_gate_up, w_down, final_norm_g):
    b, s, d = x.shape
    depth = ada_w.shape[0]
    assert depth == 1, "context-stream update between layers is not implemented"
    assert conv_w.shape[2] == rpb.shape[1] * HEAD_DIM, "equal-width branches expected"
    cos, sin = _rope_tables(s)
    pad = (-(b + 1)) % 8
    cond = jnp.concatenate([c, c_ctx[None, :], jnp.zeros((pad, d), F32)], axis=0)
    mod_all = _adaln(cond, ada_w[0], ada_b[0])
    mod = mod_all[:b].reshape(b, N_MOD, d)
    mod_c = mod_all[b].reshape(N_MOD, d)
    x2 = _layer(x, ctx, mod, mod_c, norm1_g[0], w_in[0], conv_w[0], rpb[0], group_norm_g[0],
                w_out[0], norm2_g[0], w_gate_up[0], w_down[0], cos, sin)
    return _final_norm(x2, final_norm_g).reshape(b, s, d)
```

```python
import functools

import jax
import jax.numpy as jnp
import numpy as np
from jax import lax
from jax.experimental import pallas as pl
from jax.experimental.pallas import tpu as pltpu

F32 = jnp.float32
BF16 = jnp.bfloat16

GRID_W = 64
HEAD_DIM = 128
NA_KH = 8
NA_KW = 16
ROPE_THETA = 10000.0
EPS = 1e-6
N_MOD = 6
RPB_ROWS = 2 * NA_KH - 1
RPB_COLS = 2 * NA_KW - 1

V7X_VMEM_BYTES = 64 * 1024 * 1024
V7X_LANES = 128
V7X_BF16_SUBLANES = 16
VMEM_REQUEST_CAP = 60000 * 1024

ATT_QROWS = 4
ATT_KROWS = ATT_QROWS + NA_KH
ATT_QB = ATT_QROWS * GRID_W
ATT_KB = ATT_KROWS * GRID_W


def _params(semantics, vmem_bytes):
    return pltpu.CompilerParams(
        dimension_semantics=semantics,
        vmem_limit_bytes=int(min(vmem_bytes, VMEM_REQUEST_CAP)),
    )


def _nbytes(shape, dtype):
    return int(np.prod(shape)) * jnp.dtype(dtype).itemsize


def _vmem_estimate(blocks, scratch=(), temps=()):
    total = sum(2 * _nbytes(s, d) for s, d in blocks)
    total += sum(_nbytes(s, d) for s, d in scratch)
    total += sum(_nbytes(s, d) for s, d in temps)
    return total + 2 * 1024 * 1024


def _silu(x):
    return x * (1.0 / (1.0 + jnp.exp(-x)))


def _adaln_body(c_ref, w_ref, b_ref, o_ref):
    s = _silu(c_ref[...]).astype(BF16)
    w = w_ref[...].astype(BF16)
    o_ref[...] = jnp.dot(s, w, preferred_element_type=F32) + b_ref[...]


def _adaln(cond, w, b):
    rows, d = cond.shape
    n = w.shape[1]
    tn = 512
    est = _vmem_estimate(
        [((rows, d), F32), ((d, tn), F32), ((1, tn), F32), ((rows, tn), F32)],
        temps=[((d, tn), BF16)],
    )
    return pl.pallas_call(
        _adaln_body,
        grid=(n // tn,),
        in_specs=[
            pl.BlockSpec((rows, d), lambda j: (0, 0)),
            pl.BlockSpec((d, tn), lambda j: (0, j)),
            pl.BlockSpec((1, tn), lambda j: (0, j)),
        ],
        out_specs=pl.BlockSpec((rows, tn), lambda j: (0, j)),
        out_shape=jax.ShapeDtypeStruct((rows, n), F32),
        compiler_params=_params(("arbitrary",), est),
        name="adaln",
    )(cond, w, b.reshape(1, n))


def _rms(x, g):
    return x * lax.rsqrt(jnp.mean(x * x, axis=-1, keepdims=True) + EPS) * g


def _norm_mod_body(x_ref, g_ref, sh_ref, sc_ref, o_ref, inv_ref):
    inv_ref[...] = lax.rsqrt(jnp.mean(jnp.square(x_ref[0]), axis=-1, keepdims=True) + EPS)
    gain = g_ref[...] * (1.0 + sc_ref[0])
    o_ref[0] = ((x_ref[0] * inv_ref[...]) * gain + sh_ref[0]).astype(o_ref.dtype)


def _norm_mod(x3, g, shift, scale, ts=512):
    b, s, d = x3.shape
    ts = min(ts, s)
    per_batch = shift.shape[0] == b
    mod_map = (lambda bi, si: (bi, 0, 0)) if per_batch else (lambda bi, si: (0, 0, 0))
    est = _vmem_estimate(
        [((ts, d), F32), ((ts, d), BF16)], temps=[((ts, d), F32)] * 3)
    return pl.pallas_call(
        _norm_mod_body,
        grid=(b, s // ts),
        in_specs=[
            pl.BlockSpec((1, ts, d), lambda bi, si: (bi, si, 0)),
            pl.BlockSpec((1, d), lambda bi, si: (0, 0)),
            pl.BlockSpec((1, 1, d), mod_map),
            pl.BlockSpec((1, 1, d), mod_map),
        ],
        out_specs=pl.BlockSpec((1, ts, d), lambda bi, si: (bi, si, 0)),
        out_shape=jax.ShapeDtypeStruct((b, s, d), BF16),
        scratch_shapes=[pltpu.VMEM((ts, 1), F32)],
        compiler_params=_params(("arbitrary", "arbitrary"), est),
        name="norm_mod",
    )(x3, g.reshape(1, d), shift, scale)


def _final_body(x_ref, g_ref, o_ref, inv_ref):
    inv_ref[...] = lax.rsqrt(jnp.mean(jnp.square(x_ref[...]), axis=-1, keepdims=True) + EPS)
    o_ref[...] = (x_ref[...] * inv_ref[...]) * g_ref[...]


def _final_norm(x2, g, ts=512):
    m, d = x2.shape
    est = _vmem_estimate([((ts, d), F32)] * 2, temps=[((ts, d), F32)] * 3)
    return pl.pallas_call(
        _final_body,
        grid=(m // ts,),
        in_specs=[
            pl.BlockSpec((ts, d), lambda i: (i, 0)),
            pl.BlockSpec((1, d), lambda i: (0, 0)),
        ],
        out_specs=pl.BlockSpec((ts, d), lambda i: (i, 0)),
        out_shape=jax.ShapeDtypeStruct((m, d), F32),
        scratch_shapes=[pltpu.VMEM((ts, 1), F32)],
        compiler_params=_params(("arbitrary",), est),
        name="final_norm",
    )(x2, g.reshape(1, d))


def _ws_body(*refs, n_extra, nj, ni, kc, segs, epilogue, row_splits):
    a_ref, w_hbm = refs[0], refs[1]
    extra = refs[2:2 + n_extra]
    o_ref = refs[2 + n_extra]
    wbf_even, wbf_odd, stage, sem = refs[3 + n_extra:]
    j = pl.program_id(0)
    i = pl.program_id(1)
    t = j * ni + i

    def aligned(v, multiple):
        return v if isinstance(v, int) else pl.multiple_of(v, multiple)

    def chunk_copies(tile, chunk, slot):
        row0 = aligned(chunk * kc, kc)
        return [
            pltpu.make_async_copy(
                w_hbm.at[pl.ds(row0, kc), pl.ds(aligned(col_fn(tile), V7X_LANES), width)],
                stage.at[slot, :, pl.ds(dst, width)],
                sem.at[slot])
            for col_fn, width, dst in segs
        ]

    def start(tile, chunk, slot):
        for cp in chunk_copies(tile, chunk, slot):
            cp.start()

    def wait(tile, chunk, slot):
        for cp in chunk_copies(tile, chunk, slot):
            cp.wait()

    def round_chunk(chunk, slot, dst_ref):
        row0 = aligned(chunk * kc, kc)
        dst_ref[pl.ds(row0, kc), :] = stage[slot].astype(BF16)

    @pl.when(t == 0)
    def _():
        start(0, 0, 0)
        for c in range(ni):
            if c + 1 < ni:
                start(0, c + 1, (c + 1) % 2)
            wait(0, c, c % 2)
            round_chunk(c, c % 2, wbf_even)
        start(min(1, nj - 1), 0, ni % 2)

    nxt = jnp.minimum(j + 1, nj - 1)
    slot = (t + ni) % 2
    wait(nxt, i, slot)

    def compute(cur_ref, idle_ref):
        tm = a_ref.shape[0]
        for r in range(row_splits):
            rows = slice(r * tm // row_splits, (r + 1) * tm // row_splits)
            acc = jnp.dot(a_ref[rows, :], cur_ref[...], preferred_element_type=F32)
            epilogue(acc, rows, j, extra, o_ref)
        round_chunk(i, slot, idle_ref)

    @pl.when(j % 2 == 0)
    def _():
        compute(wbf_even, wbf_odd)

    @pl.when(j % 2 == 1)
    def _():
        compute(wbf_odd, wbf_even)

    @pl.when(t + 1 < nj * ni)
    def _():
        wrap = i + 1 == ni
        tile_next = jnp.where(wrap, jnp.minimum(j + 2, nj - 1), nxt)
        chunk_next = jnp.where(wrap, 0, i + 1)
        start(tile_next, chunk_next, 1 - slot)


def _ws_matmul(a, w, segs, nj, tm, out_shape, out_dtype, out_tn, epilogue, extra=(),
               extra_specs=(), temps=(), row_splits=1, name="ws_matmul"):
    m, k = a.shape
    ni = m // tm
    assert ni * tm == m and k % ni == 0
    kc = k // ni
    assert kc % V7X_BF16_SUBLANES == 0 and tm % (row_splits * V7X_BF16_SUBLANES) == 0
    tn = sum(width for _, width, _ in segs)
    body = functools.partial(_ws_body, n_extra=len(extra), nj=nj, ni=ni, kc=kc, segs=segs,
                             epilogue=epilogue, row_splits=row_splits)
    blocks = [((tm, k), BF16), ((tm, out_tn), out_dtype)]
    blocks += [(spec.block_shape, arr.dtype) for spec, arr in zip(extra_specs, extra)]
    est = _vmem_estimate(
        blocks, scratch=[((k, tn), BF16), ((k, tn), BF16), ((2, kc, tn), F32)],
        temps=[((tm, tn), F32), ((tm, tn), F32), ((kc, tn), F32)] + list(temps))
    return pl.pallas_call(
        body,
        grid=(nj, ni),
        in_specs=[pl.BlockSpec((tm, k), lambda j, i: (i, 0)),
                  pl.BlockSpec(memory_space=pl.ANY)] + list(extra_specs),
        out_specs=pl.BlockSpec((tm, out_tn), lambda j, i: (i, j)),
        out_shape=jax.ShapeDtypeStruct(out_shape, out_dtype),
        scratch_shapes=[pltpu.VMEM((k, tn), BF16), pltpu.VMEM((k, tn), BF16),
                        pltpu.VMEM((2, kc, tn), F32), pltpu.SemaphoreType.DMA((2,))],
        compiler_params=_params(("arbitrary", "arbitrary"), est),
        name=name,
    )(a, w, *extra)


def _rope_tile(acc, cos, sin, factor):
    lane = lax.broadcasted_iota(jnp.int32, (acc.shape[0], HEAD_DIM), 1)
    low = (lane & 32) == 0
    outs = []
    for c in range(acc.shape[1] // HEAD_DIM):
        xc = acc[:, c * HEAD_DIM:(c + 1) * HEAD_DIM]
        partner = jnp.where(low, pltpu.roll(xc, HEAD_DIM - 32, 1), pltpu.roll(xc, 32, 1))
        outs.append((xc * cos + partner * sin) * factor)
    return jnp.concatenate(outs, axis=1)


def _qk_epilogue(acc, rows, j, extra, o_ref, *, n_q_tiles, q_scale):
    cos_ref, sin_ref = extra
    factor = jnp.where(j < n_q_tiles, jnp.float32(q_scale), jnp.float32(1.0))
    roped = _rope_tile(acc, cos_ref[rows, :], sin_ref[rows, :], factor)
    o_ref[rows, :] = roped.astype(o_ref.dtype)


def _cast_epilogue(acc, rows, j, extra, o_ref):
    o_ref[rows, :] = acc.astype(o_ref.dtype)


def _residual_epilogue(acc, rows, j, extra, o_ref):
    x_ref, g_ref = extra
    o_ref[rows, :] = x_ref[rows, :] + g_ref[0] * acc


def _swiglu_epilogue(acc, rows, j, extra, o_ref, *, tg):
    o_ref[rows, :] = (_silu(acc[:, :tg]) * acc[:, tg:]).astype(o_ref.dtype)


def _inproj_qk(a, w, cos, sin, na_width, tm=1024, tn=1024):
    m = a.shape[0]
    per = cos.shape[0] // tm
    table_spec = pl.BlockSpec((tm, HEAD_DIM), lambda j, i: (i % per, 0))
    return _ws_matmul(
        a, w, [(lambda tile: tile * tn, tn, 0)], 2 * na_width // tn, tm,
        (m, 2 * na_width), BF16, tn,
        functools.partial(_qk_epilogue, n_q_tiles=na_width // tn, q_scale=HEAD_DIM ** -0.5),
        extra=(cos, sin), extra_specs=(table_spec, table_spec),
        temps=[((tm, tn), F32)] * 2, row_splits=4, name="in_proj_qk")


def _inproj_rest(a, w, col0, tm=1024, tn=1024):
    m = a.shape[0]
    ncols = w.shape[1] - col0
    return _ws_matmul(
        a, w, [(lambda tile: col0 + tile * tn, tn, 0)], ncols // tn, tm,
        (m, ncols), BF16, tn, _cast_epilogue, row_splits=2, name="in_proj_rest")


def _outproj(a, w, x, gate, seq, tm, tn, name):
    m = a.shape[0]
    n = w.shape[1]
    per = seq // tm
    return _ws_matmul(
        a, w, [(lambda tile: tile * tn, tn, 0)], n // tn, tm, (m, n), F32, tn,
        _residual_epilogue, extra=(x, gate),
        extra_specs=(pl.BlockSpec((tm, tn), lambda j, i: (i, j)),
                     pl.BlockSpec((1, 1, tn), lambda j, i: (i // per, 0, j))),
        row_splits=2, name=name)


def _ffn_up(a, w, hidden, tm=2048, tg=256):
    m = a.shape[0]
    segs = [(lambda tile: tile * tg, tg, 0), (lambda tile: hidden + tile * tg, tg, tg)]
    return _ws_matmul(
        a, w, segs, hidden // tg, tm, (m, hidden), BF16, tg,
        functools.partial(_swiglu_epilogue, tg=tg), temps=[((tm, tg), F32)], row_splits=4,
        name="ffn_up")


def _plain_mm_body(a_ref, w_ref, o_ref, wbf_ref):
    @pl.when(pl.program_id(1) == 0)
    def _():
        wbf_ref[...] = w_ref[...].astype(BF16)

    o_ref[...] = jnp.dot(a_ref[...], wbf_ref[...], preferred_element_type=F32).astype(o_ref.dtype)


def _ctx_kv(a, w, col0, ncols, tm=1024, tn=512):
    m, k = a.shape
    off = col0 // tn
    est = _vmem_estimate(
        [((tm, k), BF16), ((k, tn), F32), ((tm, tn), BF16)],
        scratch=[((k, tn), BF16)], temps=[((tm, tn), F32)])
    return pl.pallas_call(
        _plain_mm_body,
        grid=(ncols // tn, m // tm),
        in_specs=[
            pl.BlockSpec((tm, k), lambda j, i: (i, 0)),
            pl.BlockSpec((k, tn), lambda j, i: (0, j + off)),
        ],
        out_specs=pl.BlockSpec((tm, tn), lambda j, i: (i, j)),
        out_shape=jax.ShapeDtypeStruct((m, ncols), BF16),
        scratch_shapes=[pltpu.VMEM((k, tn), BF16)],
        compiler_params=_params(("arbitrary", "arbitrary"), est),
        name="ctx_kv_proj",
    )(a, w)


def _block_classes(rows):
    last_ws = rows - ATT_KROWS
    first = (0, [0] * ATT_QROWS)
    interior = (NA_KH // 2, list(range(ATT_QROWS)))
    r0 = rows - ATT_QROWS
    last = (r0 - last_ws, [rows - NA_KH - last_ws] * ATT_QROWS)
    return [first, interior, last]


def _bias_body(rpb_ref, o_ref, *, classes):
    base = pl.program_id(0) * (RPB_ROWS * RPB_COLS)
    shape = (GRID_W, 2 * GRID_W)
    lane = lax.broadcasted_iota(jnp.int32, shape, 1)
    qc = lax.broadcasted_iota(jnp.int32, shape, 0)
    kc = lane & (GRID_W - 1)
    right = lane >= GRID_W
    dc_idx = kc - qc + (NA_KW - 1)
    cs = jnp.clip(qc - NA_KW // 2, 0, GRID_W - NA_KW)
    col_ok = (kc >= cs) & (kc < cs + NA_KW)
    neg = jnp.full(shape, -jnp.inf, F32)

    pair = []
    for d in range(-1, RPB_ROWS):
        dl = min(max(d, 0), RPB_ROWS - 1)
        dr = min(max(d + 1, 0), RPB_ROWS - 1)
        acc = jnp.zeros(shape, F32)
        for dc in range(RPB_COLS):
            val = jnp.where(right, rpb_ref[base + dr * RPB_COLS + dc],
                            rpb_ref[base + dl * RPB_COLS + dc])
            acc = jnp.where(dc_idx == dc, val, acc)
        pair.append(acc)

    for ci, (off, rs_rel) in enumerate(classes):
        for i in range(ATT_QROWS):
            for jp in range(ATT_KROWS // 2):
                jl, jr = 2 * jp, 2 * jp + 1
                ok_l = rs_rel[i] <= jl < rs_rel[i] + NA_KH
                ok_r = rs_rel[i] <= jr < rs_rel[i] + NA_KH
                d = jl - i - off + (NA_KH - 1)
                if ok_l and ok_r:
                    blk = jnp.where(col_ok, pair[d + 1], neg)
                elif ok_l:
                    blk = jnp.where(col_ok & jnp.logical_not(right), pair[d + 1], neg)
                elif ok_r:
                    blk = jnp.where(col_ok & right, pair[d + 1], neg)
                else:
                    blk = neg
                o_ref[0, ci, i * GRID_W:(i + 1) * GRID_W,
                      jp * 2 * GRID_W:(jp + 1) * 2 * GRID_W] = blk


def _bias_tables(rpb, rows):
    heads = rpb.shape[0]
    classes = _block_classes(rows)
    out_block = (1, len(classes), ATT_QB, ATT_KB)
    est = _vmem_estimate([(out_block, F32)], temps=[((GRID_W, 2 * GRID_W), F32)] * 32)
    return pl.pallas_call(
        functools.partial(_bias_body, classes=classes),
        grid=(heads,),
        in_specs=[pl.BlockSpec(memory_space=pltpu.SMEM)],
        out_specs=pl.BlockSpec(out_block, lambda h: (h, 0, 0, 0)),
        out_shape=jax.ShapeDtypeStruct((heads,) + out_block[1:], F32),
        compiler_params=_params(("arbitrary",), est),
        name="bias_tables",
    )(rpb.reshape(-1))


def _attn_body(q_ref, k_ref, v_ref, kc_ref, vc_ref, tab_ref, o_ref, *, rows):
    nt = (((1,), (1,)), ((), ()))
    kc = kc_ref[...]
    vc = vc_ref[...]
    n_blocks = rows // ATT_QROWS

    def window(blk):
        r0 = blk * ATT_QROWS
        ws = min(max(r0 - NA_KH // 2, 0), rows - ATT_KROWS)
        return r0 * GRID_W, ws * GRID_W

    def scores(blk):
        q0, k0 = window(blk)
        cls = 0 if blk == 0 else (2 if blk == n_blocks - 1 else 1)
        q = q_ref[q0:q0 + ATT_QB, :]
        s_lat = lax.dot_general(q, k_ref[k0:k0 + ATT_KB, :], nt, preferred_element_type=F32)
        s_ctx = lax.dot_general(q, kc, nt, preferred_element_type=F32)
        return s_lat + tab_ref[0, cls], s_ctx

    def probs(s_lat, s_ctx):
        mx = jnp.maximum(jnp.max(s_lat, axis=-1, keepdims=True),
                         jnp.max(s_ctx, axis=-1, keepdims=True))
        e_lat = jnp.exp(s_lat - mx)
        e_ctx = jnp.exp(s_ctx - mx)
        denom = jnp.sum(e_lat, axis=-1, keepdims=True) + jnp.sum(e_ctx, axis=-1, keepdims=True)
        return e_lat.astype(BF16), e_ctx.astype(BF16), denom

    def finish(blk, e_lat, e_ctx, denom):
        q0, k0 = window(blk)
        o = (jnp.dot(e_lat, v_ref[k0:k0 + ATT_KB, :], preferred_element_type=F32)
             + jnp.dot(e_ctx, vc, preferred_element_type=F32))
        o_ref[q0:q0 + ATT_QB, :] = (o / denom).astype(o_ref.dtype)

    nxt = scores(0)
    pending = None
    for blk in range(n_blocks):
        cur = nxt
        if blk + 1 < n_blocks:
            nxt = scores(blk + 1)
        p = probs(*cur)
        if pending is not None:
            finish(blk - 1, *pending)
        pending = p
    finish(n_blocks - 1, *pending)


def _attention(qk, rest, kvc, tabs, batch, seq, heads, ctx_len):
    rows = seq // GRID_W
    tab_block = (1,) + tabs.shape[1:]
    est = _vmem_estimate(
        [((seq, HEAD_DIM), BF16)] * 4 + [((ctx_len, HEAD_DIM), BF16)] * 2 + [(tab_block, F32)],
        temps=[((ATT_QB, ATT_KB + ctx_len), F32)] * 12)
    return pl.pallas_call(
        functools.partial(_attn_body, rows=rows),
        grid=(heads, batch),
        in_specs=[
            pl.BlockSpec((seq, HEAD_DIM), lambda h, b: (b, h)),
            pl.BlockSpec((seq, HEAD_DIM), lambda h, b: (b, heads + h)),
            pl.BlockSpec((seq, HEAD_DIM), lambda h, b: (b, h)),
            pl.BlockSpec((ctx_len, HEAD_DIM), lambda h, b: (b, h)),
            pl.BlockSpec((ctx_len, HEAD_DIM), lambda h, b: (b, heads + h)),
            pl.BlockSpec(tab_block, lambda h, b: (h, 0, 0, 0)),
        ],
        out_specs=pl.BlockSpec((seq, HEAD_DIM), lambda h, b: (b, h)),
        out_shape=jax.ShapeDtypeStruct((batch * seq, heads * HEAD_DIM), BF16),
        compiler_params=_params(("arbitrary", "arbitrary"), est),
        name="nbr_attention",
    )(qk, qk, rest, kvc, kvc, tabs)


HALO = V7X_BF16_SUBLANES


def _mix_body(at_ref, u_ref, b_ref, c_ref, up_ref, cp_ref, un_ref, cn_ref, cw_ref, gg_ref,
              o_ref, *, per_seq, width):
    i = pl.program_id(0)
    ts = u_ref.shape[0]
    has_prev = jnp.where(i % per_seq == 0, 0.0, 1.0).astype(F32)
    has_next = jnp.where(i % per_seq == per_seq - 1, 0.0, 1.0).astype(F32)
    g = c_ref[...].astype(F32) * u_ref[...].astype(F32)
    g_prev = (cp_ref[...].astype(F32) * up_ref[...].astype(F32))[HALO - 1:HALO, :] * has_prev
    g_next = (cn_ref[...].astype(F32) * un_ref[...].astype(F32))[0:1, :] * has_next
    row = lax.broadcasted_iota(jnp.int32, g.shape, 0)
    below = jnp.where(row == 0, g_prev, pltpu.roll(g, 1, 0))
    above = jnp.where(row == ts - 1, g_next, pltpu.roll(g, ts - 1, 0))
    conv = cw_ref[0:1, :] * below + cw_ref[1:2, :] * g + cw_ref[2:3, :] * above
    conv_o = b_ref[...].astype(F32) * conv
    o_ref[:, :width] = _rms(at_ref[...].astype(F32), gg_ref[0:1, :]).astype(o_ref.dtype)
    o_ref[:, width:] = _rms(conv_o, gg_ref[1:2, :]).astype(o_ref.dtype)


def _mix_prep(attn, p, conv_w, group_g, seq, width, col_u, ts=256):
    m = attn.shape[0]
    cu = col_u // width
    hb = ts // HALO
    n_halo = m // HALO
    est = _vmem_estimate(
        [((ts, width), BF16)] * 4 + [((HALO, width), BF16)] * 4 + [((ts, 2 * width), BF16)],
        temps=[((ts, width), F32)] * 8)
    prev_map = lambda c: (lambda i: (jnp.maximum(i * hb - 1, 0), c))
    next_map = lambda c: (lambda i: (jnp.minimum((i + 1) * hb, n_halo - 1), c))
    return pl.pallas_call(
        functools.partial(_mix_body, per_seq=seq // ts, width=width),
        grid=(m // ts,),
        in_specs=[
            pl.BlockSpec((ts, width), lambda i: (i, 0)),
            pl.BlockSpec((ts, width), lambda i: (i, cu)),
            pl.BlockSpec((ts, width), lambda i: (i, cu + 1)),
            pl.BlockSpec((ts, width), lambda i: (i, cu + 2)),
            pl.BlockSpec((HALO, width), prev_map(cu)),
            pl.BlockSpec((HALO, width), prev_map(cu + 2)),
            pl.BlockSpec((HALO, width), next_map(cu)),
            pl.BlockSpec((HALO, width), next_map(cu + 2)),
            pl.BlockSpec((3, width), lambda i: (0, 0)),
            pl.BlockSpec((2, width), lambda i: (0, 0)),
        ],
        out_specs=pl.BlockSpec((ts, 2 * width), lambda i: (i, 0)),
        out_shape=jax.ShapeDtypeStruct((m, 2 * width), BF16),
        compiler_params=_params(("arbitrary",), est),
        name="mix_prep",
    )(attn, p, p, p, p, p, p, p, conv_w, group_g.reshape(2, width))


def _rope_tables(seq):
    rows = seq // GRID_W
    quarter = HEAD_DIM // 4
    freqs = ROPE_THETA ** (-jnp.arange(quarter, dtype=F32) / quarter)
    ang_r = jnp.arange(rows, dtype=F32)[:, None] * freqs[None, :]
    ang_c = jnp.arange(GRID_W, dtype=F32)[:, None] * freqs[None, :]
    per_row = lambda a: jnp.repeat(a, GRID_W, axis=0)
    per_col = lambda a: jnp.tile(a, (rows, 1))
    cos_r, sin_r = per_row(jnp.cos(ang_r)), per_row(jnp.sin(ang_r))
    cos_c, sin_c = per_col(jnp.cos(ang_c)), per_col(jnp.sin(ang_c))
    cos = jnp.concatenate([cos_r, cos_r, cos_c, cos_c], axis=-1)
    sin = jnp.concatenate([-sin_r, sin_r, -sin_c, sin_c], axis=-1)
    return cos, sin


def _layer(x, ctx, mod, mod_c, norm1_g, w_in, conv_w, rpb, group_g, w_out, norm2_g,
           w_gate_up, w_down, cos, sin):
    b, s, d = x.shape
    ctx_len = ctx.shape[1]
    heads = rpb.shape[0]
    na_width = heads * HEAD_DIM
    conv_width = conv_w.shape[1]
    hidden = w_down.shape[0]
    m = b * s
    sh1, sc1, g1, sh2, sc2, g2 = [mod[:, i:i + 1, :] for i in range(N_MOD)]
    csh1, csc1 = mod_c[None, 0:1, :], mod_c[None, 1:2, :]

    h = _norm_mod(x, norm1_g, sh1, sc1).reshape(m, d)
    hc = _norm_mod(ctx, norm1_g, csh1, csc1).reshape(b * ctx_len, d)
    qk = _inproj_qk(h, w_in, cos, sin, na_width)
    rest = _inproj_rest(h, w_in, 2 * na_width)
    kvc = _ctx_kv(hc, w_in, na_width, 2 * na_width)
    tabs = _bias_tables(rpb, s // GRID_W)
    attn = _attention(qk, rest, kvc, tabs, b, s, heads, ctx_len)
    y = _mix_prep(attn, rest, conv_w, group_g, s, conv_width, na_width)
    x1 = _outproj(y, w_out, x.reshape(m, d), g1, s, tm=512, tn=1024, name="out_proj")
    h2 = _norm_mod(x1.reshape(b, s, d), norm2_g, sh2, sc2).reshape(m, d)
    hid = _ffn_up(h2, w_gate_up, hidden)
    return _outproj(hid, w_down, x1, g2, s, tm=512, tn=512, name="ffn_down")


def kernel(x, c, ctx, c_ctx, ada_w, ada_b, norm1_g, w_in, conv_w, rpb, group_norm_g, w_out,
           norm2_g, w_gate_up, w_down, final_norm_g):
    b, s, d = x.shape
    depth = ada_w.shape[0]
    assert depth == 1, "context-stream update between layers is not implemented"
    assert conv_w.shape[2] == rpb.shape[1] * HEAD_DIM, "equal-width branches expected"
    cos, sin = _rope_tables(s)
    pad = (-(b + 1)) % 8
    cond = jnp.concatenate([c, c_ctx[None, :], jnp.zeros((pad, d), F32)], axis=0)
    mod_all = _adaln(cond, ada_w[0], ada_b[0])
    mod = mod_all[:b].reshape(b, N_MOD, d)
    mod_c = mod_all[b].reshape(N_MOD, d)
    x2 = _layer(x, ctx, mod, mod_c, norm1_g[0], w_in[0], conv_w[0], rpb[0], group_norm_g[0],
                w_out[0], norm2_g[0], w_gate_up[0], w_down[0], cos, sin)
    return _final_norm(x2, final_norm_g).reshape(b, s, d)
```

```python
import functools

import jax
import jax.numpy as jnp
import numpy as np
from jax import lax
from jax.experimental import pallas as pl
from jax.experimental.pallas import tpu as pltpu

F32 = jnp.float32
BF16 = jnp.bfloat16

GRID_W = 64
HEAD_DIM = 128
NA_KH = 8
NA_KW = 16
ROPE_THETA = 10000.0
EPS = 1e-6
N_MOD = 6
RPB_ROWS = 2 * NA_KH - 1
RPB_COLS = 2 * NA_KW - 1

V7X_VMEM_BYTES = 64 * 1024 * 1024
V7X_LANES = 128
V7X_BF16_SUBLANES = 16
VMEM_REQUEST_CAP = 60000 * 1024

ATT_QROWS = 4
ATT_KROWS = ATT_QROWS + NA_KH
ATT_QB = ATT_QROWS * GRID_W
ATT_KB = ATT_KROWS * GRID_W


def _params(semantics, vmem_bytes):
    return pltpu.CompilerParams(
        dimension_semantics=semantics,
        vmem_limit_bytes=int(min(vmem_bytes, VMEM_REQUEST_CAP)),
    )


def _nbytes(shape, dtype):
    return int(np.prod(shape)) * jnp.dtype(dtype).itemsize


def _vmem_estimate(blocks, scratch=(), temps=()):
    total = sum(2 * _nbytes(s, d) for s, d in blocks)
    total += sum(_nbytes(s, d) for s, d in scratch)
    total += sum(_nbytes(s, d) for s, d in temps)
    return total + 2 * 1024 * 1024


def _silu(x):
    return x * (1.0 / (1.0 + jnp.exp(-x)))


def _adaln_body(c_ref, w_ref, b_ref, o_ref):
    s = _silu(c_ref[...]).astype(BF16)
    w = w_ref[...].astype(BF16)
    o_ref[...] = jnp.dot(s, w, preferred_element_type=F32) + b_ref[...]


def _adaln(cond, w, b):
    rows, d = cond.shape
    n = w.shape[1]
    tn = 512
    est = _vmem_estimate(
        [((rows, d), F32), ((d, tn), F32), ((1, tn), F32), ((rows, tn), F32)],
        temps=[((d, tn), BF16)],
    )
    return pl.pallas_call(
        _adaln_body,
        grid=(n // tn,),
        in_specs=[
            pl.BlockSpec((rows, d), lambda j: (0, 0)),
            pl.BlockSpec((d, tn), lambda j: (0, j)),
            pl.BlockSpec((1, tn), lambda j: (0, j)),
        ],
        out_specs=pl.BlockSpec((rows, tn), lambda j: (0, j)),
        out_shape=jax.ShapeDtypeStruct((rows, n), F32),
        compiler_params=_params(("arbitrary",), est),
        name="adaln",
    )(cond, w, b.reshape(1, n))


def _rms(x, g):
    return x * lax.rsqrt(jnp.mean(x * x, axis=-1, keepdims=True) + EPS) * g


def _norm_mod_body(x_ref, g_ref, sh_ref, sc_ref, o_ref, inv_ref):
    inv_ref[...] = lax.rsqrt(jnp.mean(jnp.square(x_ref[0]), axis=-1, keepdims=True) + EPS)
    gain = g_ref[...] * (1.0 + sc_ref[0])
    o_ref[0] = ((x_ref[0] * inv_ref[...]) * gain + sh_ref[0]).astype(o_ref.dtype)


def _norm_mod(x3, g, shift, scale, ts=512):
    b, s, d = x3.shape
    ts = min(ts, s)
    per_batch = shift.shape[0] == b
    mod_map = (lambda bi, si: (bi, 0, 0)) if per_batch else (lambda bi, si: (0, 0, 0))
    est = _vmem_estimate(
        [((ts, d), F32), ((ts, d), BF16)], temps=[((ts, d), F32)] * 3)
    return pl.pallas_call(
        _norm_mod_body,
        grid=(b, s // ts),
        in_specs=[
            pl.BlockSpec((1, ts, d), lambda bi, si: (bi, si, 0)),
            pl.BlockSpec((1, d), lambda bi, si: (0, 0)),
            pl.BlockSpec((1, 1, d), mod_map),
            pl.BlockSpec((1, 1, d), mod_map),
        ],
        out_specs=pl.BlockSpec((1, ts, d), lambda bi, si: (bi, si, 0)),
        out_shape=jax.ShapeDtypeStruct((b, s, d), BF16),
        scratch_shapes=[pltpu.VMEM((ts, 1), F32)],
        compiler_params=_params(("arbitrary", "arbitrary"), est),
        name="norm_mod",
    )(x3, g.reshape(1, d), shift, scale)


def _final_body(x_ref, g_ref, o_ref, inv_ref):
    inv_ref[...] = lax.rsqrt(jnp.mean(jnp.square(x_ref[...]), axis=-1, keepdims=True) + EPS)
    o_ref[...] = (x_ref[...] * inv_ref[...]) * g_ref[...]


def _final_norm(x2, g, ts=512):
    m, d = x2.shape
    est = _vmem_estimate([((ts, d), F32)] * 2, temps=[((ts, d), F32)] * 3)
    return pl.pallas_call(
        _final_body,
        grid=(m // ts,),
        in_specs=[
            pl.BlockSpec((ts, d), lambda i: (i, 0)),
            pl.BlockSpec((1, d), lambda i: (0, 0)),
        ],
        out_specs=pl.BlockSpec((ts, d), lambda i: (i, 0)),
        out_shape=jax.ShapeDtypeStruct((m, d), F32),
        scratch_shapes=[pltpu.VMEM((ts, 1), F32)],
        compiler_params=_params(("arbitrary",), est),
        name="final_norm",
    )(x2, g.reshape(1, d))


def _ws_body(*refs, n_extra, nj, ni, kc, segs, epilogue, row_splits):
    a_ref, w_hbm = refs[0], refs[1]
    extra = refs[2:2 + n_extra]
    o_ref = refs[2 + n_extra]
    wbf_even, wbf_odd, stage, sem = refs[3 + n_extra:]
    j = pl.program_id(0)
    i = pl.program_id(1)
    t = j * ni + i

    def aligned(v, multiple):
        return v if isinstance(v, int) else pl.multiple_of(v, multiple)

    def chunk_copies(tile, chunk, slot):
        row0 = aligned(chunk * kc, kc)
        return [
            pltpu.make_async_copy(
                w_hbm.at[pl.ds(row0, kc), pl.ds(aligned(col_fn(tile), V7X_LANES), width)],
                stage.at[slot, :, pl.ds(dst, width)],
                sem.at[slot])
            for col_fn, width, dst in segs
        ]

    def start(tile, chunk, slot):
        for cp in chunk_copies(tile, chunk, slot):
            cp.start()

    def wait(tile, chunk, slot):
        for cp in chunk_copies(tile, chunk, slot):
            cp.wait()

    def round_chunk(chunk, slot, dst_ref):
        row0 = aligned(chunk * kc, kc)
        dst_ref[pl.ds(row0, kc), :] = stage[slot].astype(BF16)

    @pl.when(t == 0)
    def _():
        start(0, 0, 0)
        for c in range(ni):
            if c + 1 < ni:
                start(0, c + 1, (c + 1) % 2)
            wait(0, c, c % 2)
            round_chunk(c, c % 2, wbf_even)
        start(min(1, nj - 1), 0, ni % 2)

    nxt = jnp.minimum(j + 1, nj - 1)
    slot = (t + ni) % 2
    wait(nxt, i, slot)

    @pl.when(t + 1 < nj * ni)
    def _():
        wrap = i + 1 == ni
        tile_next = jnp.where(wrap, jnp.minimum(j + 2, nj - 1), nxt)
        chunk_next = jnp.where(wrap, 0, i + 1)
        start(tile_next, chunk_next, 1 - slot)

    def compute(cur_ref, idle_ref):
        tm = a_ref.shape[0]
        for r in range(row_splits):
            rows = slice(r * tm // row_splits, (r + 1) * tm // row_splits)
            acc = jnp.dot(a_ref[rows, :], cur_ref[...], preferred_element_type=F32)
            epilogue(acc, rows, j, extra, o_ref)
        round_chunk(i, slot, idle_ref)

    @pl.when(j % 2 == 0)
    def _():
        compute(wbf_even, wbf_odd)

    @pl.when(j % 2 == 1)
    def _():
        compute(wbf_odd, wbf_even)


def _ws_matmul(a, w, segs, nj, tm, out_shape, out_dtype, out_tn, epilogue, extra=(),
               extra_specs=(), temps=(), row_splits=1, name="ws_matmul"):
    m, k = a.shape
    ni = m // tm
    assert ni * tm == m and k % ni == 0
    kc = k // ni
    assert kc % V7X_BF16_SUBLANES == 0 and tm % (row_splits * V7X_BF16_SUBLANES) == 0
    tn = sum(width for _, width, _ in segs)
    body = functools.partial(_ws_body, n_extra=len(extra), nj=nj, ni=ni, kc=kc, segs=segs,
                             epilogue=epilogue, row_splits=row_splits)
    blocks = [((tm, k), BF16), ((tm, out_tn), out_dtype)]
    blocks += [(spec.block_shape, arr.dtype) for spec, arr in zip(extra_specs, extra)]
    est = _vmem_estimate(
        blocks, scratch=[((k, tn), BF16), ((k, tn), BF16), ((2, kc, tn), F32)],
        temps=[((tm, tn), F32), ((tm, tn), F32), ((kc, tn), F32)] + list(temps))
    return pl.pallas_call(
        body,
        grid=(nj, ni),
        in_specs=[pl.BlockSpec((tm, k), lambda j, i: (i, 0)),
                  pl.BlockSpec(memory_space=pl.ANY)] + list(extra_specs),
        out_specs=pl.BlockSpec((tm, out_tn), lambda j, i: (i, j)),
        out_shape=jax.ShapeDtypeStruct(out_shape, out_dtype),
        scratch_shapes=[pltpu.VMEM((k, tn), BF16), pltpu.VMEM((k, tn), BF16),
                        pltpu.VMEM((2, kc, tn), F32), pltpu.SemaphoreType.DMA((2,))],
        compiler_params=_params(("arbitrary", "arbitrary"), est),
        name=name,
    )(a, w, *extra)


def _rope_tile(acc, cos, sin, factor):
    lane = lax.broadcasted_iota(jnp.int32, (acc.shape[0], HEAD_DIM), 1)
    low = (lane & 32) == 0
    outs = []
    for c in range(acc.shape[1] // HEAD_DIM):
        xc = acc[:, c * HEAD_DIM:(c + 1) * HEAD_DIM]
        partner = jnp.where(low, pltpu.roll(xc, HEAD_DIM - 32, 1), pltpu.roll(xc, 32, 1))
        outs.append((xc * cos + partner * sin) * factor)
    return jnp.concatenate(outs, axis=1)


def _qk_epilogue(acc, rows, j, extra, o_ref, *, n_q_tiles, q_scale):
    cos_ref, sin_ref = extra
    factor = jnp.where(j < n_q_tiles, jnp.float32(q_scale), jnp.float32(1.0))
    roped = _rope_tile(acc, cos_ref[rows, :], sin_ref[rows, :], factor)
    o_ref[rows, :] = roped.astype(o_ref.dtype)


def _cast_epilogue(acc, rows, j, extra, o_ref):
    o_ref[rows, :] = acc.astype(o_ref.dtype)


def _residual_epilogue(acc, rows, j, extra, o_ref):
    x_ref, g_ref = extra
    o_ref[rows, :] = x_ref[rows, :] + g_ref[0] * acc


def _swiglu_epilogue(acc, rows, j, extra, o_ref, *, tg):
    o_ref[rows, :] = (_silu(acc[:, :tg]) * acc[:, tg:]).astype(o_ref.dtype)


def _inproj_qk(a, w, cos, sin, na_width, tm=1024, tn=1024):
    m = a.shape[0]
    per = cos.shape[0] // tm
    table_spec = pl.BlockSpec((tm, HEAD_DIM), lambda j, i: (i % per, 0))
    return _ws_matmul(
        a, w, [(lambda tile: tile * tn, tn, 0)], 2 * na_width // tn, tm,
        (m, 2 * na_width), BF16, tn,
        functools.partial(_qk_epilogue, n_q_tiles=na_width // tn, q_scale=HEAD_DIM ** -0.5),
        extra=(cos, sin), extra_specs=(table_spec, table_spec),
        temps=[((tm, tn), F32)] * 2, row_splits=4, name="in_proj_qk")


def _inproj_rest(a, w, col0, tm=1024, tn=1024):
    m = a.shape[0]
    ncols = w.shape[1] - col0
    return _ws_matmul(
        a, w, [(lambda tile: col0 + tile * tn, tn, 0)], ncols // tn, tm,
        (m, ncols), BF16, tn, _cast_epilogue, row_splits=2, name="in_proj_rest")


def _outproj(a, w, x, gate, seq, tm, tn, name):
    m = a.shape[0]
    n = w.shape[1]
    per = seq // tm
    return _ws_matmul(
        a, w, [(lambda tile: tile * tn, tn, 0)], n // tn, tm, (m, n), F32, tn,
        _residual_epilogue, extra=(x, gate),
        extra_specs=(pl.BlockSpec((tm, tn), lambda j, i: (i, j)),
                     pl.BlockSpec((1, 1, tn), lambda j, i: (i // per, 0, j))),
        row_splits=2, name=name)


def _ffn_up(a, w, hidden, tm=2048, tg=256):
    m = a.shape[0]
    segs = [(lambda tile: tile * tg, tg, 0), (lambda tile: hidden + tile * tg, tg, tg)]
    return _ws_matmul(
        a, w, segs, hidden // tg, tm, (m, hidden), BF16, tg,
        functools.partial(_swiglu_epilogue, tg=tg), temps=[((tm, tg), F32)], row_splits=4,
        name="ffn_up")


def _plain_mm_body(a_ref, w_ref, o_ref, wbf_ref):
    @pl.when(pl.program_id(1) == 0)
    def _():
        wbf_ref[...] = w_ref[...].astype(BF16)

    o_ref[...] = jnp.dot(a_ref[...], wbf_ref[...], preferred_element_type=F32).astype(o_ref.dtype)


def _ctx_kv(a, w, col0, ncols, tm=1024, tn=512):
    m, k = a.shape
    off = col0 // tn
    est = _vmem_estimate(
        [((tm, k), BF16), ((k, tn), F32), ((tm, tn), BF16)],
        scratch=[((k, tn), BF16)], temps=[((tm, tn), F32)])
    return pl.pallas_call(
        _plain_mm_body,
        grid=(ncols // tn, m // tm),
        in_specs=[
            pl.BlockSpec((tm, k), lambda j, i: (i, 0)),
            pl.BlockSpec((k, tn), lambda j, i: (0, j + off)),
        ],
        out_specs=pl.BlockSpec((tm, tn), lambda j, i: (i, j)),
        out_shape=jax.ShapeDtypeStruct((m, ncols), BF16),
        scratch_shapes=[pltpu.VMEM((k, tn), BF16)],
        compiler_params=_params(("arbitrary", "arbitrary"), est),
        name="ctx_kv_proj",
    )(a, w)


def _block_classes(rows):
    last_ws = rows - ATT_KROWS
    first = (0, [0] * ATT_QROWS)
    interior = (NA_KH // 2, list(range(ATT_QROWS)))
    r0 = rows - ATT_QROWS
    last = (r0 - last_ws, [rows - NA_KH - last_ws] * ATT_QROWS)
    return [first, interior, last]


def _bias_body(rpb_ref, o_ref, *, classes):
    base = pl.program_id(0) * (RPB_ROWS * RPB_COLS)
    shape = (GRID_W, 2 * GRID_W)
    lane = lax.broadcasted_iota(jnp.int32, shape, 1)
    qc = lax.broadcasted_iota(jnp.int32, shape, 0)
    kc = lane & (GRID_W - 1)
    right = lane >= GRID_W
    dc_idx = kc - qc + (NA_KW - 1)
    cs = jnp.clip(qc - NA_KW // 2, 0, GRID_W - NA_KW)
    col_ok = (kc >= cs) & (kc < cs + NA_KW)
    neg = jnp.full(shape, -jnp.inf, F32)

    pair = []
    for d in range(-1, RPB_ROWS):
        dl = min(max(d, 0), RPB_ROWS - 1)
        dr = min(max(d + 1, 0), RPB_ROWS - 1)
        acc = jnp.zeros(shape, F32)
        for dc in range(RPB_COLS):
            val = jnp.where(right, rpb_ref[base + dr * RPB_COLS + dc],
                            rpb_ref[base + dl * RPB_COLS + dc])
            acc = jnp.where(dc_idx == dc, val, acc)
        pair.append(acc)

    for ci, (off, rs_rel) in enumerate(classes):
        for i in range(ATT_QROWS):
            for jp in range(ATT_KROWS // 2):
                jl, jr = 2 * jp, 2 * jp + 1
                ok_l = rs_rel[i] <= jl < rs_rel[i] + NA_KH
                ok_r = rs_rel[i] <= jr < rs_rel[i] + NA_KH
                d = jl - i - off + (NA_KH - 1)
                if ok_l and ok_r:
                    blk = jnp.where(col_ok, pair[d + 1], neg)
                elif ok_l:
                    blk = jnp.where(col_ok & jnp.logical_not(right), pair[d + 1], neg)
                elif ok_r:
                    blk = jnp.where(col_ok & right, pair[d + 1], neg)
                else:
                    blk = neg
                o_ref[0, ci, i * GRID_W:(i + 1) * GRID_W,
                      jp * 2 * GRID_W:(jp + 1) * 2 * GRID_W] = blk


def _bias_tables(rpb, rows):
    heads = rpb.shape[0]
    classes = _block_classes(rows)
    out_block = (1, len(classes), ATT_QB, ATT_KB)
    est = _vmem_estimate([(out_block, F32)], temps=[((GRID_W, 2 * GRID_W), F32)] * 32)
    return pl.pallas_call(
        functools.partial(_bias_body, classes=classes),
        grid=(heads,),
        in_specs=[pl.BlockSpec(memory_space=pltpu.SMEM)],
        out_specs=pl.BlockSpec(out_block, lambda h: (h, 0, 0, 0)),
        out_shape=jax.ShapeDtypeStruct((heads,) + out_block[1:], F32),
        compiler_params=_params(("arbitrary",), est),
        name="bias_tables",
    )(rpb.reshape(-1))


def _attn_body(q_ref, k_ref, v_ref, kc_ref, vc_ref, tab_ref, o_ref, *, rows):
    nt = (((1,), (1,)), ((), ()))
    kc = kc_ref[...]
    vc = vc_ref[...]
    n_blocks = rows // ATT_QROWS

    def window(blk):
        r0 = blk * ATT_QROWS
        ws = min(max(r0 - NA_KH // 2, 0), rows - ATT_KROWS)
        return r0 * GRID_W, ws * GRID_W

    def scores(blk):
        q0, k0 = window(blk)
        cls = 0 if blk == 0 else (2 if blk == n_blocks - 1 else 1)
        q = q_ref[q0:q0 + ATT_QB, :]
        s_lat = lax.dot_general(q, k_ref[k0:k0 + ATT_KB, :], nt, preferred_element_type=F32)
        s_ctx = lax.dot_general(q, kc, nt, preferred_element_type=F32)
        return s_lat + tab_ref[0, cls], s_ctx

    def probs(s_lat, s_ctx):
        mx = jnp.maximum(jnp.max(s_lat, axis=-1, keepdims=True),
                         jnp.max(s_ctx, axis=-1, keepdims=True))
        e_lat = jnp.exp(s_lat - mx)
        e_ctx = jnp.exp(s_ctx - mx)
        denom = jnp.sum(e_lat, axis=-1, keepdims=True) + jnp.sum(e_ctx, axis=-1, keepdims=True)
        return e_lat.astype(BF16), e_ctx.astype(BF16), denom

    def finish(blk, e_lat, e_ctx, denom):
        q0, k0 = window(blk)
        o = (jnp.dot(e_lat, v_ref[k0:k0 + ATT_KB, :], preferred_element_type=F32)
             + jnp.dot(e_ctx, vc, preferred_element_type=F32))
        o_ref[q0:q0 + ATT_QB, :] = (o / denom).astype(o_ref.dtype)

    nxt = scores(0)
    pending = None
    for blk in range(n_blocks):
        cur = nxt
        if blk + 1 < n_blocks:
            nxt = scores(blk + 1)
        p = probs(*cur)
        if pending is not None:
            finish(blk - 1, *pending)
        pending = p
    finish(n_blocks - 1, *pending)


def _attention(qk, rest, kvc, tabs, batch, seq, heads, ctx_len):
    rows = seq // GRID_W
    tab_block = (1,) + tabs.shape[1:]
    est = _vmem_estimate(
        [((seq, HEAD_DIM), BF16)] * 4 + [((ctx_len, HEAD_DIM), BF16)] * 2 + [(tab_block, F32)],
        temps=[((ATT_QB, ATT_KB + ctx_len), F32)] * 12)
    return pl.pallas_call(
        functools.partial(_attn_body, rows=rows),
        grid=(heads, batch),
        in_specs=[
            pl.BlockSpec((seq, HEAD_DIM), lambda h, b: (b, h)),
            pl.BlockSpec((seq, HEAD_DIM), lambda h, b: (b, heads + h)),
            pl.BlockSpec((seq, HEAD_DIM), lambda h, b: (b, h)),
            pl.BlockSpec((ctx_len, HEAD_DIM), lambda h, b: (b, h)),
            pl.BlockSpec((ctx_len, HEAD_DIM), lambda h, b: (b, heads + h)),
            pl.BlockSpec(tab_block, lambda h, b: (h, 0, 0, 0)),
        ],
        out_specs=pl.BlockSpec((seq, HEAD_DIM), lambda h, b: (b, h)),
        out_shape=jax.ShapeDtypeStruct((batch * seq, heads * HEAD_DIM), BF16),
        compiler_params=_params(("arbitrary", "arbitrary"), est),
        name="nbr_attention",
    )(qk, qk, rest, kvc, kvc, tabs)


HALO = V7X_BF16_SUBLANES


def _mix_body(at_ref, u_ref, b_ref, c_ref, up_ref, cp_ref, un_ref, cn_ref, cw_ref, gg_ref,
              o_ref, *, per_seq, width):
    i = pl.program_id(0)
    ts = u_ref.shape[0]
    has_prev = jnp.where(i % per_seq == 0, 0.0, 1.0).astype(F32)
    has_next = jnp.where(i % per_seq == per_seq - 1, 0.0, 1.0).astype(F32)
    g = c_ref[...].astype(F32) * u_ref[...].astype(F32)
    g_prev = (cp_ref[...].astype(F32) * up_ref[...].astype(F32))[HALO - 1:HALO, :] * has_prev
    g_next = (cn_ref[...].astype(F32) * un_ref[...].astype(F32))[0:1, :] * has_next
    row = lax.broadcasted_iota(jnp.int32, g.shape, 0)
    below = jnp.where(row == 0, g_prev, pltpu.roll(g, 1, 0))
    above = jnp.where(row == ts - 1, g_next, pltpu.roll(g, ts - 1, 0))
    conv = cw_ref[0:1, :] * below + cw_ref[1:2, :] * g + cw_ref[2:3, :] * above
    conv_o = b_ref[...].astype(F32) * conv
    o_ref[:, :width] = _rms(at_ref[...].astype(F32), gg_ref[0:1, :]).astype(o_ref.dtype)
    o_ref[:, width:] = _rms(conv_o, gg_ref[1:2, :]).astype(o_ref.dtype)


def _mix_prep(attn, p, conv_w, group_g, seq, width, col_u, ts=256):
    m = attn.shape[0]
    cu = col_u // width
    hb = ts // HALO
    n_halo = m // HALO
    est = _vmem_estimate(
        [((ts, width), BF16)] * 4 + [((HALO, width), BF16)] * 4 + [((ts, 2 * width), BF16)],
        temps=[((ts, width), F32)] * 8)
    prev_map = lambda c: (lambda i: (jnp.maximum(i * hb - 1, 0), c))
    next_map = lambda c: (lambda i: (jnp.minimum((i + 1) * hb, n_halo - 1), c))
    return pl.pallas_call(
        functools.partial(_mix_body, per_seq=seq // ts, width=width),
        grid=(m // ts,),
        in_specs=[
            pl.BlockSpec((ts, width), lambda i: (i, 0)),
            pl.BlockSpec((ts, width), lambda i: (i, cu)),
            pl.BlockSpec((ts, width), lambda i: (i, cu + 1)),
            pl.BlockSpec((ts, width), lambda i: (i, cu + 2)),
            pl.BlockSpec((HALO, width), prev_map(cu)),
            pl.BlockSpec((HALO, width), prev_map(cu + 2)),
            pl.BlockSpec((HALO, width), next_map(cu)),
            pl.BlockSpec((HALO, width), next_map(cu + 2)),
            pl.BlockSpec((3, width), lambda i: (0, 0)),
            pl.BlockSpec((2, width), lambda i: (0, 0)),
        ],
        out_specs=pl.BlockSpec((ts, 2 * width), lambda i: (i, 0)),
        out_shape=jax.ShapeDtypeStruct((m, 2 * width), BF16),
        compiler_params=_params(("arbitrary",), est),
        name="mix_prep",
    )(attn, p, p, p, p, p, p, p, conv_w, group_g.reshape(2, width))


def _rope_tables(seq):
    rows = seq // GRID_W
    quarter = HEAD_DIM // 4
    freqs = ROPE_THETA ** (-jnp.arange(quarter, dtype=F32) / quarter)
    ang_r = jnp.arange(rows, dtype=F32)[:, None] * freqs[None, :]
    ang_c = jnp.arange(GRID_W, dtype=F32)[:, None] * freqs[None, :]
    per_row = lambda a: jnp.repeat(a, GRID_W, axis=0)
    per_col = lambda a: jnp.tile(a, (rows, 1))
    cos_r, sin_r = per_row(jnp.cos(ang_r)), per_row(jnp.sin(ang_r))
    cos_c, sin_c = per_col(jnp.cos(ang_c)), per_col(jnp.sin(ang_c))
    cos = jnp.concatenate([cos_r, cos_r, cos_c, cos_c], axis=-1)
    sin = jnp.concatenate([-sin_r, sin_r, -sin_c, sin_c], axis=-1)
    return cos, sin


def _layer(x, ctx, mod, mod_c, norm1_g, w_in, conv_w, rpb, group_g, w_out, norm2_g,
           w_gate_up, w_down, cos, sin):
    b, s, d = x.shape
    ctx_len = ctx.shape[1]
    heads = rpb.shape[0]
    na_width = heads * HEAD_DIM
    conv_width = conv_w.shape[1]
    hidden = w_down.shape[0]
    m = b * s
    sh1, sc1, g1, sh2, sc2, g2 = [mod[:, i:i + 1, :] for i in range(N_MOD)]
    csh1, csc1 = mod_c[None, 0:1, :], mod_c[None, 1:2, :]

    h = _norm_mod(x, norm1_g, sh1, sc1).reshape(m, d)
    hc = _norm_mod(ctx, norm1_g, csh1, csc1).reshape(b * ctx_len, d)
    qk = _inproj_qk(h, w_in, cos, sin, na_width)
    rest = _inproj_rest(h, w_in, 2 * na_width)
    kvc = _ctx_kv(hc, w_in, na_width, 2 * na_width)
    tabs = _bias_tables(rpb, s // GRID_W)
    attn = _attention(qk, rest, kvc, tabs, b, s, heads, ctx_len)
    y = _mix_prep(attn, rest, conv_w, group_g, s, conv_width, na_width)
    x1 = _outproj(y, w_out, x.reshape(m, d), g1, s, tm=512, tn=1024, name="out_proj")
    h2 = _norm_mod(x1.reshape(b, s, d), norm2_g, sh2, sc2).reshape(m, d)
    hid = _ffn_up(h2, w_gate_up, hidden)
    return _outproj(hid, w_down, x1, g2, s, tm=512, tn=512, name="ffn_down")


def kernel(x, c, ctx, c_ctx, ada_w, ada_b, norm1_g, w_in, conv_w, rpb, group_norm_g, w_out,
           norm2_g, w_gate_up, w_down, final_norm_g):
    b, s, d = x.shape
    depth = ada_w.shape[0]
    assert depth == 1, "context-stream update between layers is not implemented"
    assert conv_w.shape[2] == rpb.shape[1] * HEAD_DIM, "equal-width branches expected"
    cos, sin = _rope_tables(s)
    pad = (-(b + 1)) % 8
    cond = jnp.concatenate([c, c_ctx[None, :], jnp.zeros((pad, d), F32)], axis=0)
    mod_all = _adaln(cond, ada_w[0], ada_b[0])
    mod = mod_all[:b].reshape(b, N_MOD, d)
    mod_c = mod_all[b].reshape(N_MOD, d)
    x2 = _layer(x, ctx, mod, mod_c, norm1_g[0], w_in[0], conv_w[0], rpb[0], group_norm_g[0],
                w_out[0], norm2_g[0], w_gate_up[0], w_down[0], cos, sin)
    return _final_norm(x2, final_norm_g).reshape(b, s, d)
```

```python
import functools

import jax
import jax.numpy as jnp
import numpy as np
from jax import lax
from jax.experimental import pallas as pl
from jax.experimental.pallas import tpu as pltpu

F32 = jnp.float32
BF16 = jnp.bfloat16

GRID_W = 64
HEAD_DIM = 128
NA_KH = 8
NA_KW = 16
ROPE_THETA = 10000.0
EPS = 1e-6
N_MOD = 6
RPB_ROWS = 2 * NA_KH - 1
RPB_COLS = 2 * NA_KW - 1

V7X_VMEM_BYTES = 64 * 1024 * 1024
V7X_LANES = 128
V7X_BF16_SUBLANES = 16
VMEM_REQUEST_CAP = 60000 * 1024

ATT_QROWS = 4
ATT_KROWS = ATT_QROWS + NA_KH
ATT_QB = ATT_QROWS * GRID_W
ATT_KB = ATT_KROWS * GRID_W


def _params(semantics, vmem_bytes):
    return pltpu.CompilerParams(
        dimension_semantics=semantics,
        vmem_limit_bytes=int(min(vmem_bytes, VMEM_REQUEST_CAP)),
    )


def _nbytes(shape, dtype):
    return int(np.prod(shape)) * jnp.dtype(dtype).itemsize


def _vmem_estimate(blocks, scratch=(), temps=()):
    total = sum(2 * _nbytes(s, d) for s, d in blocks)
    total += sum(_nbytes(s, d) for s, d in scratch)
    total += sum(_nbytes(s, d) for s, d in temps)
    return total + 2 * 1024 * 1024


def _silu(x):
    return x * (1.0 / (1.0 + jnp.exp(-x)))


def _adaln_body(c_ref, w_ref, b_ref, o_ref):
    s = _silu(c_ref[...]).astype(BF16)
    w = w_ref[...].astype(BF16)
    o_ref[...] = jnp.dot(s, w, preferred_element_type=F32) + b_ref[...]


def _adaln(cond, w, b, n):
    rows, d = cond.shape
    tn = 512
    est = _vmem_estimate(
        [((rows, d), F32), ((d, tn), F32), ((1, tn), F32), ((rows, tn), F32)],
        temps=[((d, tn), BF16)],
    )
    return pl.pallas_call(
        _adaln_body,
        grid=(n // tn,),
        in_specs=[
            pl.BlockSpec((rows, d), lambda j: (0, 0)),
            pl.BlockSpec((d, tn), lambda j: (0, j)),
            pl.BlockSpec((1, tn), lambda j: (0, j)),
        ],
        out_specs=pl.BlockSpec((rows, tn), lambda j: (0, j)),
        out_shape=jax.ShapeDtypeStruct((rows, n), F32),
        compiler_params=_params(("arbitrary",), est),
        name="adaln",
    )(cond, w, b.reshape(1, -1))


def _rms(x, g):
    return x * lax.rsqrt(jnp.mean(x * x, axis=-1, keepdims=True) + EPS) * g


def _norm_mod_body(x_ref, g_ref, sh_ref, sc_ref, o_ref, inv_ref):
    inv_ref[...] = lax.rsqrt(jnp.mean(jnp.square(x_ref[0]), axis=-1, keepdims=True) + EPS)
    gain = g_ref[...] * (1.0 + sc_ref[0])
    o_ref[0] = ((x_ref[0] * inv_ref[...]) * gain + sh_ref[0]).astype(o_ref.dtype)


def _norm_mod(x3, g, shift, scale, ts=512):
    b, s, d = x3.shape
    ts = min(ts, s)
    per_batch = shift.shape[0] == b
    mod_map = (lambda bi, si: (bi, 0, 0)) if per_batch else (lambda bi, si: (0, 0, 0))
    est = _vmem_estimate(
        [((ts, d), F32), ((ts, d), BF16)], temps=[((ts, d), F32)] * 3)
    return pl.pallas_call(
        _norm_mod_body,
        grid=(b, s // ts),
        in_specs=[
            pl.BlockSpec((1, ts, d), lambda bi, si: (bi, si, 0)),
            pl.BlockSpec((1, d), lambda bi, si: (0, 0)),
            pl.BlockSpec((1, 1, d), mod_map),
            pl.BlockSpec((1, 1, d), mod_map),
        ],
        out_specs=pl.BlockSpec((1, ts, d), lambda bi, si: (bi, si, 0)),
        out_shape=jax.ShapeDtypeStruct((b, s, d), BF16),
        scratch_shapes=[pltpu.VMEM((ts, 1), F32)],
        compiler_params=_params(("arbitrary", "arbitrary"), est),
        name="norm_mod",
    )(x3, g.reshape(1, d), shift, scale)


def _final_body(x_ref, g_ref, o_ref, inv_ref):
    inv_ref[...] = lax.rsqrt(jnp.mean(jnp.square(x_ref[...]), axis=-1, keepdims=True) + EPS)
    o_ref[...] = (x_ref[...] * inv_ref[...]) * g_ref[...]


def _final_norm(x2, g, ts=512):
    m, d = x2.shape
    est = _vmem_estimate([((ts, d), F32)] * 2, temps=[((ts, d), F32)] * 3)
    return pl.pallas_call(
        _final_body,
        grid=(m // ts,),
        in_specs=[
            pl.BlockSpec((ts, d), lambda i: (i, 0)),
            pl.BlockSpec((1, d), lambda i: (0, 0)),
        ],
        out_specs=pl.BlockSpec((ts, d), lambda i: (i, 0)),
        out_shape=jax.ShapeDtypeStruct((m, d), F32),
        scratch_shapes=[pltpu.VMEM((ts, 1), F32)],
        compiler_params=_params(("arbitrary",), est),
        name="final_norm",
    )(x2, g.reshape(1, d))


def _ws_body(*refs, n_extra, nj, ni, kc, segs, epilogue, row_splits):
    a_ref, w_hbm = refs[0], refs[1]
    extra = refs[2:2 + n_extra]
    o_ref = refs[2 + n_extra]
    wbf_even, wbf_odd, stage, sem = refs[3 + n_extra:]
    j = pl.program_id(0)
    i = pl.program_id(1)
    t = j * ni + i

    def aligned(v, multiple):
        return v if isinstance(v, int) else pl.multiple_of(v, multiple)

    def chunk_copies(tile, chunk, slot):
        row0 = aligned(chunk * kc, kc)
        return [
            pltpu.make_async_copy(
                w_hbm.at[pl.ds(row0, kc), pl.ds(aligned(col_fn(tile), V7X_LANES), width)],
                stage.at[slot, :, pl.ds(dst, width)],
                sem.at[slot])
            for col_fn, width, dst in segs
        ]

    def start(tile, chunk, slot):
        for cp in chunk_copies(tile, chunk, slot):
            cp.start()

    def wait(tile, chunk, slot):
        for cp in chunk_copies(tile, chunk, slot):
            cp.wait()

    def round_chunk(chunk, slot, dst_ref):
        row0 = aligned(chunk * kc, kc)
        dst_ref[pl.ds(row0, kc), :] = stage[slot].astype(BF16)

    @pl.when(t == 0)
    def _():
        start(0, 0, 0)
        for c in range(ni):
            if c + 1 < ni:
                start(0, c + 1, (c + 1) % 2)
            wait(0, c, c % 2)
            round_chunk(c, c % 2, wbf_even)
        start(min(1, nj - 1), 0, ni % 2)

    nxt = jnp.minimum(j + 1, nj - 1)
    slot = (t + ni) % 2
    wait(nxt, i, slot)

    @pl.when(t + 1 < nj * ni)
    def _():
        wrap = i + 1 == ni
        tile_next = jnp.where(wrap, jnp.minimum(j + 2, nj - 1), nxt)
        chunk_next = jnp.where(wrap, 0, i + 1)
        start(tile_next, chunk_next, 1 - slot)

    def compute(cur_ref, idle_ref):
        tm = a_ref.shape[0]
        for r in range(row_splits):
            rows = slice(r * tm // row_splits, (r + 1) * tm // row_splits)
            acc = jnp.dot(a_ref[rows, :], cur_ref[...], preferred_element_type=F32)
            epilogue(acc, rows, j, extra, o_ref)
        round_chunk(i, slot, idle_ref)

    @pl.when(j % 2 == 0)
    def _():
        compute(wbf_even, wbf_odd)

    @pl.when(j % 2 == 1)
    def _():
        compute(wbf_odd, wbf_even)


def _ws_matmul(a, w, segs, nj, tm, out_shape, out_dtype, out_tn, epilogue, extra=(),
               extra_specs=(), temps=(), row_splits=1, name="ws_matmul"):
    m, k = a.shape
    ni = m // tm
    assert ni * tm == m and k % ni == 0
    kc = k // ni
    assert kc % V7X_BF16_SUBLANES == 0 and tm % (row_splits * V7X_BF16_SUBLANES) == 0
    tn = sum(width for _, width, _ in segs)
    body = functools.partial(_ws_body, n_extra=len(extra), nj=nj, ni=ni, kc=kc, segs=segs,
                             epilogue=epilogue, row_splits=row_splits)
    blocks = [((tm, k), BF16), ((tm, out_tn), out_dtype)]
    blocks += [(spec.block_shape, arr.dtype) for spec, arr in zip(extra_specs, extra)]
    est = _vmem_estimate(
        blocks, scratch=[((k, tn), BF16), ((k, tn), BF16), ((2, kc, tn), F32)],
        temps=[((tm, tn), F32), ((tm, tn), F32), ((kc, tn), F32)] + list(temps))
    return pl.pallas_call(
        body,
        grid=(nj, ni),
        in_specs=[pl.BlockSpec((tm, k), lambda j, i: (i, 0)),
                  pl.BlockSpec(memory_space=pl.ANY)] + list(extra_specs),
        out_specs=pl.BlockSpec((tm, out_tn), lambda j, i: (i, j)),
        out_shape=jax.ShapeDtypeStruct(out_shape, out_dtype),
        scratch_shapes=[pltpu.VMEM((k, tn), BF16), pltpu.VMEM((k, tn), BF16),
                        pltpu.VMEM((2, kc, tn), F32), pltpu.SemaphoreType.DMA((2,))],
        compiler_params=_params(("arbitrary", "arbitrary"), est),
        name=name,
    )(a, w, *extra)


def _rope_tile(acc, cos, sin, factor):
    lane = lax.broadcasted_iota(jnp.int32, (acc.shape[0], HEAD_DIM), 1)
    low = (lane & 32) == 0
    outs = []
    for c in range(acc.shape[1] // HEAD_DIM):
        xc = acc[:, c * HEAD_DIM:(c + 1) * HEAD_DIM]
        partner = jnp.where(low, pltpu.roll(xc, HEAD_DIM - 32, 1), pltpu.roll(xc, 32, 1))
        outs.append((xc * cos + partner * sin) * factor)
    return jnp.concatenate(outs, axis=1)


def _qk_epilogue(acc, rows, j, extra, o_ref, *, n_q_tiles, q_scale):
    cos_ref, sin_ref = extra
    factor = jnp.where(j < n_q_tiles, jnp.float32(q_scale), jnp.float32(1.0))
    roped = _rope_tile(acc, cos_ref[rows, :], sin_ref[rows, :], factor)
    o_ref[rows, :] = roped.astype(o_ref.dtype)


def _cast_epilogue(acc, rows, j, extra, o_ref):
    o_ref[rows, :] = acc.astype(o_ref.dtype)


def _residual_epilogue(acc, rows, j, extra, o_ref):
    x_ref, g_ref = extra
    o_ref[rows, :] = x_ref[rows, :] + g_ref[0] * acc


def _swiglu_epilogue(acc, rows, j, extra, o_ref, *, tg):
    o_ref[rows, :] = (_silu(acc[:, :tg]) * acc[:, tg:]).astype(o_ref.dtype)


def _inproj_qk(a, w, cos, sin, na_width, tm=1024, tn=1024):
    m = a.shape[0]
    per = cos.shape[0] // tm
    table_spec = pl.BlockSpec((tm, HEAD_DIM), lambda j, i: (i % per, 0))
    return _ws_matmul(
        a, w, [(lambda tile: tile * tn, tn, 0)], 2 * na_width // tn, tm,
        (m, 2 * na_width), BF16, tn,
        functools.partial(_qk_epilogue, n_q_tiles=na_width // tn, q_scale=HEAD_DIM ** -0.5),
        extra=(cos, sin), extra_specs=(table_spec, table_spec),
        temps=[((tm, tn), F32)] * 2, row_splits=4, name="in_proj_qk")


def _inproj_rest(a, w, col0, tm=1024, tn=1024):
    m = a.shape[0]
    ncols = w.shape[1] - col0
    return _ws_matmul(
        a, w, [(lambda tile: col0 + tile * tn, tn, 0)], ncols // tn, tm,
        (m, ncols), BF16, tn, _cast_epilogue, row_splits=2, name="in_proj_rest")


def _outproj(a, w, x, gate, seq, tm, tn, name):
    m = a.shape[0]
    n = w.shape[1]
    per = seq // tm
    return _ws_matmul(
        a, w, [(lambda tile: tile * tn, tn, 0)], n // tn, tm, (m, n), F32, tn,
        _residual_epilogue, extra=(x, gate),
        extra_specs=(pl.BlockSpec((tm, tn), lambda j, i: (i, j)),
                     pl.BlockSpec((1, 1, tn), lambda j, i: (i // per, 0, j))),
        row_splits=2, name=name)


def _ffn_up(a, w, hidden, tm=1024, tg=512):
    m = a.shape[0]
    half = tg // 2
    nj = pl.cdiv(hidden, tg)
    assert hidden % half == 0 and nj * tg <= 2 * hidden
    last_col = 2 * hidden - half

    def clamp(col):
        return min(col, last_col) if isinstance(col, int) else jnp.minimum(col, last_col)

    segs = [(lambda tile: tile * tg, tg, 0),
            (lambda tile: hidden + tile * tg, half, tg),
            (lambda tile: clamp(hidden + tile * tg + half), half, tg + half)]
    return _ws_matmul(
        a, w, segs, nj, tm, (m, hidden), BF16, tg,
        functools.partial(_swiglu_epilogue, tg=tg), temps=[((tm, tg), F32)], row_splits=4,
        name="ffn_up")


def _plain_mm_body(a_ref, w_ref, o_ref, wbf_ref):
    @pl.when(pl.program_id(1) == 0)
    def _():
        wbf_ref[...] = w_ref[...].astype(BF16)

    o_ref[...] = jnp.dot(a_ref[...], wbf_ref[...], preferred_element_type=F32).astype(o_ref.dtype)


def _ctx_kv(a, w, col0, ncols, tm=1024, tn=512):
    m, k = a.shape
    off = col0 // tn
    est = _vmem_estimate(
        [((tm, k), BF16), ((k, tn), F32), ((tm, tn), BF16)],
        scratch=[((k, tn), BF16)], temps=[((tm, tn), F32)])
    return pl.pallas_call(
        _plain_mm_body,
        grid=(ncols // tn, m // tm),
        in_specs=[
            pl.BlockSpec((tm, k), lambda j, i: (i, 0)),
            pl.BlockSpec((k, tn), lambda j, i: (0, j + off)),
        ],
        out_specs=pl.BlockSpec((tm, tn), lambda j, i: (i, j)),
        out_shape=jax.ShapeDtypeStruct((m, ncols), BF16),
        scratch_shapes=[pltpu.VMEM((k, tn), BF16)],
        compiler_params=_params(("arbitrary", "arbitrary"), est),
        name="ctx_kv_proj",
    )(a, w)


def _block_classes(rows):
    last_ws = rows - ATT_KROWS
    first = (0, [0] * ATT_QROWS)
    interior = (NA_KH // 2, list(range(ATT_QROWS)))
    r0 = rows - ATT_QROWS
    last = (r0 - last_ws, [rows - NA_KH - last_ws] * ATT_QROWS)
    return [first, interior, last]


def _bias_body(rpb_ref, o_ref, *, classes):
    base = pl.program_id(0) * (RPB_ROWS * RPB_COLS)
    shape = (GRID_W, 2 * GRID_W)
    lane = lax.broadcasted_iota(jnp.int32, shape, 1)
    qc = lax.broadcasted_iota(jnp.int32, shape, 0)
    kc = lane & (GRID_W - 1)
    right = lane >= GRID_W
    dc_idx = kc - qc + (NA_KW - 1)
    cs = jnp.clip(qc - NA_KW // 2, 0, GRID_W - NA_KW)
    col_ok = (kc >= cs) & (kc < cs + NA_KW)
    neg = jnp.full(shape, -jnp.inf, F32)

    pair = []
    for d in range(-1, RPB_ROWS):
        dl = min(max(d, 0), RPB_ROWS - 1)
        dr = min(max(d + 1, 0), RPB_ROWS - 1)
        acc = jnp.zeros(shape, F32)
        for dc in range(RPB_COLS):
            val = jnp.where(right, rpb_ref[base + dr * RPB_COLS + dc],
                            rpb_ref[base + dl * RPB_COLS + dc])
            acc = jnp.where(dc_idx == dc, val, acc)
        pair.append(acc)

    for ci, (off, rs_rel) in enumerate(classes):
        for i in range(ATT_QROWS):
            for jp in range(ATT_KROWS // 2):
                jl, jr = 2 * jp, 2 * jp + 1
                ok_l = rs_rel[i] <= jl < rs_rel[i] + NA_KH
                ok_r = rs_rel[i] <= jr < rs_rel[i] + NA_KH
                d = jl - i - off + (NA_KH - 1)
                if ok_l and ok_r:
                    blk = jnp.where(col_ok, pair[d + 1], neg)
                elif ok_l:
                    blk = jnp.where(col_ok & jnp.logical_not(right), pair[d + 1], neg)
                elif ok_r:
                    blk = jnp.where(col_ok & right, pair[d + 1], neg)
                else:
                    blk = neg
                o_ref[0, ci, i * GRID_W:(i + 1) * GRID_W,
                      jp * 2 * GRID_W:(jp + 1) * 2 * GRID_W] = blk


def _bias_tables(rpb, rows):
    heads = rpb.shape[0]
    classes = _block_classes(rows)
    out_block = (1, len(classes), ATT_QB, ATT_KB)
    est = _vmem_estimate([(out_block, F32)], temps=[((GRID_W, 2 * GRID_W), F32)] * 32)
    return pl.pallas_call(
        functools.partial(_bias_body, classes=classes),
        grid=(heads,),
        in_specs=[pl.BlockSpec(memory_space=pltpu.SMEM)],
        out_specs=pl.BlockSpec(out_block, lambda h: (h, 0, 0, 0)),
        out_shape=jax.ShapeDtypeStruct((heads,) + out_block[1:], F32),
        compiler_params=_params(("arbitrary",), est),
        name="bias_tables",
    )(rpb.reshape(-1))


def _attn_body(q_ref, k_ref, v_ref, kc_ref, vc_ref, tab_ref, cond_ref, adaw_hbm, adab_ref,
               o_ref, mod_ref, wstage, sem, *, rows, batch, n_steps, mod_col0, mod_tn):
    step = pl.program_id(0) * batch + pl.program_id(1)
    slot = step % 2

    def chunk_copy(s, sl):
        col = mod_col0 + s * mod_tn
        col = col if isinstance(col, int) else pl.multiple_of(col, mod_tn)
        return pltpu.make_async_copy(
            adaw_hbm.at[pl.ds(0, wstage.shape[1]), pl.ds(col, mod_tn)], wstage.at[sl], sem.at[sl])

    @pl.when(step == 0)
    def _():
        chunk_copy(0, 0).start()

    chunk_copy(step, slot).wait()

    @pl.when(step + 1 < n_steps)
    def _():
        chunk_copy(step + 1, 1 - slot).start()

    mod_ref[...] = jnp.dot(_silu(cond_ref[...]).astype(BF16), wstage[slot].astype(BF16),
                           preferred_element_type=F32) + adab_ref[...]

    nt = (((1,), (1,)), ((), ()))
    kc = kc_ref[...]
    vc = vc_ref[...]
    n_blocks = rows // ATT_QROWS

    def window(blk):
        r0 = blk * ATT_QROWS
        ws = min(max(r0 - NA_KH // 2, 0), rows - ATT_KROWS)
        return r0 * GRID_W, ws * GRID_W

    def scores(blk):
        q0, k0 = window(blk)
        cls = 0 if blk == 0 else (2 if blk == n_blocks - 1 else 1)
        q = q_ref[q0:q0 + ATT_QB, :]
        s_lat = lax.dot_general(q, k_ref[k0:k0 + ATT_KB, :], nt, preferred_element_type=F32)
        s_ctx = lax.dot_general(q, kc, nt, preferred_element_type=F32)
        return s_lat + tab_ref[0, cls], s_ctx

    def probs(s_lat, s_ctx):
        mx = jnp.maximum(jnp.max(s_lat, axis=-1, keepdims=True),
                         jnp.max(s_ctx, axis=-1, keepdims=True))
        e_lat = jnp.exp(s_lat - mx)
        e_ctx = jnp.exp(s_ctx - mx)
        denom = jnp.sum(e_lat, axis=-1, keepdims=True) + jnp.sum(e_ctx, axis=-1, keepdims=True)
        return e_lat.astype(BF16), e_ctx.astype(BF16), denom

    def finish(blk, e_lat, e_ctx, denom):
        q0, k0 = window(blk)
        o = (jnp.dot(e_lat, v_ref[k0:k0 + ATT_KB, :], preferred_element_type=F32)
             + jnp.dot(e_ctx, vc, preferred_element_type=F32))
        o_ref[q0:q0 + ATT_QB, :] = (o / denom).astype(o_ref.dtype)

    nxt = scores(0)
    pending = None
    for blk in range(n_blocks):
        cur = nxt
        if blk + 1 < n_blocks:
            nxt = scores(blk + 1)
        p = probs(*cur)
        if pending is not None:
            finish(blk - 1, *pending)
        pending = p
    finish(n_blocks - 1, *pending)


def _attention(qk, rest, kvc, tabs, cond, ada_w, ada_b, mod_col0, batch, seq, heads, ctx_len):
    rows = seq // GRID_W
    tab_block = (1,) + tabs.shape[1:]
    n_steps = heads * batch
    crows, d = cond.shape
    mod_cols = ada_w.shape[1] - mod_col0
    mod_tn = mod_cols // n_steps
    assert mod_tn * n_steps == mod_cols and mod_tn % V7X_LANES == 0 and mod_col0 % mod_tn == 0
    est = _vmem_estimate(
        [((seq, HEAD_DIM), BF16)] * 4 + [((ctx_len, HEAD_DIM), BF16)] * 2 + [(tab_block, F32)],
        scratch=[((2, d, mod_tn), F32)],
        temps=[((ATT_QB, ATT_KB + ctx_len), F32)] * 12 + [((d, mod_tn), F32)])
    step = lambda h, b: h * batch + b
    return pl.pallas_call(
        functools.partial(_attn_body, rows=rows, batch=batch, n_steps=n_steps,
                          mod_col0=mod_col0, mod_tn=mod_tn),
        grid=(heads, batch),
        in_specs=[
            pl.BlockSpec((seq, HEAD_DIM), lambda h, b: (b, h)),
            pl.BlockSpec((seq, HEAD_DIM), lambda h, b: (b, heads + h)),
            pl.BlockSpec((seq, HEAD_DIM), lambda h, b: (b, h)),
            pl.BlockSpec((ctx_len, HEAD_DIM), lambda h, b: (b, h)),
            pl.BlockSpec((ctx_len, HEAD_DIM), lambda h, b: (b, heads + h)),
            pl.BlockSpec(tab_block, lambda h, b: (h, 0, 0, 0)),
            pl.BlockSpec((crows, d), lambda h, b: (0, 0)),
            pl.BlockSpec(memory_space=pl.ANY),
            pl.BlockSpec((1, mod_tn), lambda h, b: (0, mod_col0 // mod_tn + step(h, b))),
        ],
        out_specs=[
            pl.BlockSpec((seq, HEAD_DIM), lambda h, b: (b, h)),
            pl.BlockSpec((crows, mod_tn), lambda h, b: (0, step(h, b))),
        ],
        out_shape=[
            jax.ShapeDtypeStruct((batch * seq, heads * HEAD_DIM), BF16),
            jax.ShapeDtypeStruct((crows, mod_cols), F32),
        ],
        scratch_shapes=[pltpu.VMEM((2, d, mod_tn), F32), pltpu.SemaphoreType.DMA((2,))],
        compiler_params=_params(("arbitrary", "arbitrary"), est),
        name="nbr_attention",
    )(qk, qk, rest, kvc, kvc, tabs, cond, ada_w, ada_b.reshape(1, -1))


HALO = V7X_BF16_SUBLANES


def _mix_body(at_ref, u_ref, b_ref, c_ref, up_ref, cp_ref, un_ref, cn_ref, cw_ref, gg_ref,
              o_ref, *, per_seq, width):
    i = pl.program_id(0)
    ts = u_ref.shape[0]
    has_prev = jnp.where(i % per_seq == 0, 0.0, 1.0).astype(F32)
    has_next = jnp.where(i % per_seq == per_seq - 1, 0.0, 1.0).astype(F32)
    g = c_ref[...].astype(F32) * u_ref[...].astype(F32)
    g_prev = (cp_ref[...].astype(F32) * up_ref[...].astype(F32))[HALO - 1:HALO, :] * has_prev
    g_next = (cn_ref[...].astype(F32) * un_ref[...].astype(F32))[0:1, :] * has_next
    row = lax.broadcasted_iota(jnp.int32, g.shape, 0)
    below = jnp.where(row == 0, g_prev, pltpu.roll(g, 1, 0))
    above = jnp.where(row == ts - 1, g_next, pltpu.roll(g, ts - 1, 0))
    conv = cw_ref[0:1, :] * below + cw_ref[1:2, :] * g + cw_ref[2:3, :] * above
    conv_o = b_ref[...].astype(F32) * conv
    o_ref[:, :width] = _rms(at_ref[...].astype(F32), gg_ref[0:1, :]).astype(o_ref.dtype)
    o_ref[:, width:] = _rms(conv_o, gg_ref[1:2, :]).astype(o_ref.dtype)


def _mix_prep(attn, p, conv_w, group_g, seq, width, col_u, ts=256):
    m = attn.shape[0]
    cu = col_u // width
    hb = ts // HALO
    n_halo = m // HALO
    est = _vmem_estimate(
        [((ts, width), BF16)] * 4 + [((HALO, width), BF16)] * 4 + [((ts, 2 * width), BF16)],
        temps=[((ts, width), F32)] * 8)
    prev_map = lambda c: (lambda i: (jnp.maximum(i * hb - 1, 0), c))
    next_map = lambda c: (lambda i: (jnp.minimum((i + 1) * hb, n_halo - 1), c))
    return pl.pallas_call(
        functools.partial(_mix_body, per_seq=seq // ts, width=width),
        grid=(m // ts,),
        in_specs=[
            pl.BlockSpec((ts, width), lambda i: (i, 0)),
            pl.BlockSpec((ts, width), lambda i: (i, cu)),
            pl.BlockSpec((ts, width), lambda i: (i, cu + 1)),
            pl.BlockSpec((ts, width), lambda i: (i, cu + 2)),
            pl.BlockSpec((HALO, width), prev_map(cu)),
            pl.BlockSpec((HALO, width), prev_map(cu + 2)),
            pl.BlockSpec((HALO, width), next_map(cu)),
            pl.BlockSpec((HALO, width), next_map(cu + 2)),
            pl.BlockSpec((3, width), lambda i: (0, 0)),
            pl.BlockSpec((2, width), lambda i: (0, 0)),
        ],
        out_specs=pl.BlockSpec((ts, 2 * width), lambda i: (i, 0)),
        out_shape=jax.ShapeDtypeStruct((m, 2 * width), BF16),
        compiler_params=_params(("arbitrary",), est),
        name="mix_prep",
    )(attn, p, p, p, p, p, p, p, conv_w, group_g.reshape(2, width))


def _rope_tables(seq):
    rows = seq // GRID_W
    quarter = HEAD_DIM // 4
    freqs = ROPE_THETA ** (-jnp.arange(quarter, dtype=F32) / quarter)
    ang_r = jnp.arange(rows, dtype=F32)[:, None] * freqs[None, :]
    ang_c = jnp.arange(GRID_W, dtype=F32)[:, None] * freqs[None, :]
    per_row = lambda a: jnp.repeat(a, GRID_W, axis=0)
    per_col = lambda a: jnp.tile(a, (rows, 1))
    cos_r, sin_r = per_row(jnp.cos(ang_r)), per_row(jnp.sin(ang_r))
    cos_c, sin_c = per_col(jnp.cos(ang_c)), per_col(jnp.sin(ang_c))
    cos = jnp.concatenate([cos_r, cos_r, cos_c, cos_c], axis=-1)
    sin = jnp.concatenate([-sin_r, sin_r, -sin_c, sin_c], axis=-1)
    return cos, sin


def _layer(x, ctx, cond, ada_w, ada_b, norm1_g, w_in, conv_w, rpb, group_g, w_out, norm2_g,
           w_gate_up, w_down, cos, sin):
    b, s, d = x.shape
    ctx_len = ctx.shape[1]
    heads = rpb.shape[0]
    na_width = heads * HEAD_DIM
    conv_width = conv_w.shape[1]
    hidden = w_down.shape[0]
    m = b * s
    n_early = 2
    early = _adaln(cond, ada_w, ada_b, n_early * d)
    sh1, sc1 = [early[:b, None, i * d:(i + 1) * d] for i in range(n_early)]
    csh1, csc1 = [early[b:b + 1, None, i * d:(i + 1) * d] for i in range(n_early)]

    h = _norm_mod(x, norm1_g, sh1, sc1).reshape(m, d)
    hc = _norm_mod(ctx, norm1_g, csh1, csc1).reshape(b * ctx_len, d)
    qk = _inproj_qk(h, w_in, cos, sin, na_width)
    rest = _inproj_rest(h, w_in, 2 * na_width)
    kvc = _ctx_kv(hc, w_in, na_width, 2 * na_width)
    tabs = _bias_tables(rpb, s // GRID_W)
    attn, late = _attention(qk, rest, kvc, tabs, cond, ada_w, ada_b, n_early * d, b, s, heads,
                            ctx_len)
    g1, sh2, sc2, g2 = [late[:b, None, i * d:(i + 1) * d] for i in range(N_MOD - n_early)]
    y = _mix_prep(attn, rest, conv_w, group_g, s, conv_width, na_width)
    x1 = _outproj(y, w_out, x.reshape(m, d), g1, s, tm=512, tn=1024, name="out_proj")
    h2 = _norm_mod(x1.reshape(b, s, d), norm2_g, sh2, sc2).reshape(m, d)
    hid = _ffn_up(h2, w_gate_up, hidden)
    return _outproj(hid, w_down, x1, g2, s, tm=512, tn=512, name="ffn_down")


def kernel(x, c, ctx, c_ctx, ada_w, ada_b, norm1_g, w_in, conv_w, rpb, group_norm_g, w_out,
           norm2_g, w_gate_up, w_down, final_norm_g):
    b, s, d = x.shape
    depth = ada_w.shape[0]
    assert depth == 1, "context-stream update between layers is not implemented"
    assert conv_w.shape[2] == rpb.shape[1] * HEAD_DIM, "equal-width branches expected"
    cos, sin = _rope_tables(s)
    pad = (-(b + 1)) % 8
    cond = jnp.concatenate([c, c_ctx[None, :], jnp.zeros((pad, d), F32)], axis=0)
    x2 = _layer(x, ctx, cond, ada_w[0], ada_b[0], norm1_g[0], w_in[0], conv_w[0], rpb[0],
                group_norm_g[0], w_out[0], norm2_g[0], w_gate_up[0], w_down[0], cos, sin)
    return _final_norm(x2, final_norm_g).reshape(b, s, d)
```

```python
import functools

import jax
import jax.numpy as jnp
import numpy as np
from jax import lax
from jax.experimental import pallas as pl
from jax.experimental.pallas import tpu as pltpu

F32 = jnp.float32
BF16 = jnp.bfloat16

GRID_W = 64
HEAD_DIM = 128
NA_KH = 8
NA_KW = 16
ROPE_THETA = 10000.0
EPS = 1e-6
N_MOD = 6
RPB_ROWS = 2 * NA_KH - 1
RPB_COLS = 2 * NA_KW - 1

V7X_VMEM_BYTES = 64 * 1024 * 1024
V7X_LANES = 128
V7X_BF16_SUBLANES = 16
VMEM_REQUEST_CAP = 60000 * 1024

ATT_QROWS = 4
ATT_KROWS = ATT_QROWS + NA_KH
ATT_QB = ATT_QROWS * GRID_W
ATT_KB = ATT_KROWS * GRID_W


def _params(semantics, vmem_bytes, keep_operands_in_hbm=False):
    request = VMEM_REQUEST_CAP if keep_operands_in_hbm else min(vmem_bytes, VMEM_REQUEST_CAP)
    return pltpu.CompilerParams(dimension_semantics=semantics, vmem_limit_bytes=int(request))


def _nbytes(shape, dtype):
    return int(np.prod(shape)) * jnp.dtype(dtype).itemsize


def _vmem_estimate(blocks, scratch=(), temps=()):
    total = sum(2 * _nbytes(s, d) for s, d in blocks)
    total += sum(_nbytes(s, d) for s, d in scratch)
    total += sum(_nbytes(s, d) for s, d in temps)
    return total + 2 * 1024 * 1024


def _silu(x):
    return x * (1.0 / (1.0 + jnp.exp(-x)))


def _adaln_body(c_ref, w_ref, b_ref, o_ref):
    s = _silu(c_ref[...]).astype(BF16)
    w = w_ref[...].astype(BF16)
    o_ref[...] = jnp.dot(s, w, preferred_element_type=F32) + b_ref[...]


def _adaln(cond, w, b, n):
    rows, d = cond.shape
    tn = 512
    est = _vmem_estimate(
        [((rows, d), F32), ((d, tn), F32), ((1, tn), F32), ((rows, tn), F32)],
        temps=[((d, tn), BF16)],
    )
    return pl.pallas_call(
        _adaln_body,
        grid=(n // tn,),
        in_specs=[
            pl.BlockSpec((rows, d), lambda j: (0, 0)),
            pl.BlockSpec((d, tn), lambda j: (0, j)),
            pl.BlockSpec((1, tn), lambda j: (0, j)),
        ],
        out_specs=pl.BlockSpec((rows, tn), lambda j: (0, j)),
        out_shape=jax.ShapeDtypeStruct((rows, n), F32),
        compiler_params=_params(("arbitrary",), est),
        name="adaln",
    )(cond, w, b.reshape(1, -1))


def _rms(x, g):
    return x * lax.rsqrt(jnp.mean(x * x, axis=-1, keepdims=True) + EPS) * g


def _norm_mod_body(x_ref, g_ref, sh_ref, sc_ref, o_ref, inv_ref):
    inv_ref[...] = lax.rsqrt(jnp.mean(jnp.square(x_ref[0]), axis=-1, keepdims=True) + EPS)
    gain = g_ref[...] * (1.0 + sc_ref[0])
    o_ref[0] = ((x_ref[0] * inv_ref[...]) * gain + sh_ref[0]).astype(o_ref.dtype)


def _norm_mod(x3, g, shift, scale, ts=512):
    b, s, d = x3.shape
    ts = min(ts, s)
    per_batch = shift.shape[0] == b
    mod_map = (lambda bi, si: (bi, 0, 0)) if per_batch else (lambda bi, si: (0, 0, 0))
    est = _vmem_estimate(
        [((ts, d), F32), ((ts, d), BF16)], temps=[((ts, d), F32)] * 3)
    return pl.pallas_call(
        _norm_mod_body,
        grid=(b, s // ts),
        in_specs=[
            pl.BlockSpec((1, ts, d), lambda bi, si: (bi, si, 0)),
            pl.BlockSpec((1, d), lambda bi, si: (0, 0)),
            pl.BlockSpec((1, 1, d), mod_map),
            pl.BlockSpec((1, 1, d), mod_map),
        ],
        out_specs=pl.BlockSpec((1, ts, d), lambda bi, si: (bi, si, 0)),
        out_shape=jax.ShapeDtypeStruct((b, s, d), BF16),
        scratch_shapes=[pltpu.VMEM((ts, 1), F32)],
        compiler_params=_params(("arbitrary", "arbitrary"), est, keep_operands_in_hbm=True),
        name="norm_mod",
    )(x3, g.reshape(1, d), shift, scale)


def _final_body(x_ref, g_ref, o_ref, inv_ref):
    inv_ref[...] = lax.rsqrt(jnp.mean(jnp.square(x_ref[...]), axis=-1, keepdims=True) + EPS)
    o_ref[...] = (x_ref[...] * inv_ref[...]) * g_ref[...]


def _final_norm(x2, g, ts=512):
    m, d = x2.shape
    est = _vmem_estimate([((ts, d), F32)] * 2, temps=[((ts, d), F32)] * 3)
    return pl.pallas_call(
        _final_body,
        grid=(m // ts,),
        in_specs=[
            pl.BlockSpec((ts, d), lambda i: (i, 0)),
            pl.BlockSpec((1, d), lambda i: (0, 0)),
        ],
        out_specs=pl.BlockSpec((ts, d), lambda i: (i, 0)),
        out_shape=jax.ShapeDtypeStruct((m, d), F32),
        scratch_shapes=[pltpu.VMEM((ts, 1), F32)],
        compiler_params=_params(("arbitrary",), est),
        name="final_norm",
    )(x2, g.reshape(1, d))


def _ws_body(*refs, n_extra, nj, ni, kc, segs, epilogue, row_splits):
    a_ref, w_hbm = refs[0], refs[1]
    extra = refs[2:2 + n_extra]
    o_ref = refs[2 + n_extra]
    wbf_even, wbf_odd, stage, sem = refs[3 + n_extra:]
    j = pl.program_id(0)
    i = pl.program_id(1)
    t = j * ni + i

    def aligned(v, multiple):
        return v if isinstance(v, int) else pl.multiple_of(v, multiple)

    def chunk_copies(tile, chunk, slot):
        row0 = aligned(chunk * kc, kc)
        return [
            pltpu.make_async_copy(
                w_hbm.at[pl.ds(row0, kc), pl.ds(aligned(col_fn(tile), V7X_LANES), width)],
                stage.at[slot, :, pl.ds(dst, width)],
                sem.at[slot])
            for col_fn, width, dst in segs
        ]

    def start(tile, chunk, slot):
        for cp in chunk_copies(tile, chunk, slot):
            cp.start()

    def wait(tile, chunk, slot):
        for cp in chunk_copies(tile, chunk, slot):
            cp.wait()

    def round_chunk(chunk, slot, dst_ref):
        row0 = aligned(chunk * kc, kc)
        dst_ref[pl.ds(row0, kc), :] = stage[slot].astype(BF16)

    @pl.when(t == 0)
    def _():
        start(0, 0, 0)
        for c in range(ni):
            if c + 1 < ni:
                start(0, c + 1, (c + 1) % 2)
            wait(0, c, c % 2)
            round_chunk(c, c % 2, wbf_even)
        start(min(1, nj - 1), 0, ni % 2)

    nxt = jnp.minimum(j + 1, nj - 1)
    slot = (t + ni) % 2
    wait(nxt, i, slot)

    @pl.when(t + 1 < nj * ni)
    def _():
        wrap = i + 1 == ni
        tile_next = jnp.where(wrap, jnp.minimum(j + 2, nj - 1), nxt)
        chunk_next = jnp.where(wrap, 0, i + 1)
        start(tile_next, chunk_next, 1 - slot)

    def compute(cur_ref, idle_ref):
        tm = a_ref.shape[0]
        for r in range(row_splits):
            rows = slice(r * tm // row_splits, (r + 1) * tm // row_splits)
            acc = jnp.dot(a_ref[rows, :], cur_ref[...], preferred_element_type=F32)
            epilogue(acc, rows, j, extra, o_ref)
        round_chunk(i, slot, idle_ref)

    @pl.when(j % 2 == 0)
    def _():
        compute(wbf_even, wbf_odd)

    @pl.when(j % 2 == 1)
    def _():
        compute(wbf_odd, wbf_even)


def _ws_matmul(a, w, segs, nj, tm, out_shape, out_dtype, out_tn, epilogue, extra=(),
               extra_specs=(), temps=(), row_splits=1, name="ws_matmul"):
    m, k = a.shape
    ni = m // tm
    assert ni * tm == m and k % ni == 0
    kc = k // ni
    assert kc % V7X_BF16_SUBLANES == 0 and tm % (row_splits * V7X_BF16_SUBLANES) == 0
    tn = sum(width for _, width, _ in segs)
    body = functools.partial(_ws_body, n_extra=len(extra), nj=nj, ni=ni, kc=kc, segs=segs,
                             epilogue=epilogue, row_splits=row_splits)
    blocks = [((tm, k), BF16), ((tm, out_tn), out_dtype)]
    blocks += [(spec.block_shape, arr.dtype) for spec, arr in zip(extra_specs, extra)]
    est = _vmem_estimate(
        blocks, scratch=[((k, tn), BF16), ((k, tn), BF16), ((2, kc, tn), F32)],
        temps=[((tm, tn), F32), ((tm, tn), F32), ((kc, tn), F32)] + list(temps))
    return pl.pallas_call(
        body,
        grid=(nj, ni),
        in_specs=[pl.BlockSpec((tm, k), lambda j, i: (i, 0)),
                  pl.BlockSpec(memory_space=pl.ANY)] + list(extra_specs),
        out_specs=pl.BlockSpec((tm, out_tn), lambda j, i: (i, j)),
        out_shape=jax.ShapeDtypeStruct(out_shape, out_dtype),
        scratch_shapes=[pltpu.VMEM((k, tn), BF16), pltpu.VMEM((k, tn), BF16),
                        pltpu.VMEM((2, kc, tn), F32), pltpu.SemaphoreType.DMA((2,))],
        compiler_params=_params(("arbitrary", "arbitrary"), est),
        name=name,
    )(a, w, *extra)


def _rope_tile(acc, cos, sin, factor):
    lane = lax.broadcasted_iota(jnp.int32, (acc.shape[0], HEAD_DIM), 1)
    low = (lane & 32) == 0
    outs = []
    for c in range(acc.shape[1] // HEAD_DIM):
        xc = acc[:, c * HEAD_DIM:(c + 1) * HEAD_DIM]
        partner = jnp.where(low, pltpu.roll(xc, HEAD_DIM - 32, 1), pltpu.roll(xc, 32, 1))
        outs.append((xc * cos + partner * sin) * factor)
    return jnp.concatenate(outs, axis=1)


def _qk_epilogue(acc, rows, j, extra, o_ref, *, n_q_tiles, q_scale):
    cos_ref, sin_ref = extra
    factor = jnp.where(j < n_q_tiles, jnp.float32(q_scale), jnp.float32(1.0))
    roped = _rope_tile(acc, cos_ref[rows, :], sin_ref[rows, :], factor)
    o_ref[rows, :] = roped.astype(o_ref.dtype)


def _cast_epilogue(acc, rows, j, extra, o_ref):
    o_ref[rows, :] = acc.astype(o_ref.dtype)


def _residual_epilogue(acc, rows, j, extra, o_ref):
    x_ref, g_ref = extra
    o_ref[rows, :] = x_ref[rows, :] + g_ref[0] * acc


def _swiglu_epilogue(acc, rows, j, extra, o_ref, *, tg):
    o_ref[rows, :] = (_silu(acc[:, :tg]) * acc[:, tg:]).astype(o_ref.dtype)


def _inproj_qk(a, w, cos, sin, na_width, tm=1024, tn=1024):
    m = a.shape[0]
    per = cos.shape[0] // tm
    table_spec = pl.BlockSpec((tm, HEAD_DIM), lambda j, i: (i % per, 0))
    return _ws_matmul(
        a, w, [(lambda tile: tile * tn, tn, 0)], 2 * na_width // tn, tm,
        (m, 2 * na_width), BF16, tn,
        functools.partial(_qk_epilogue, n_q_tiles=na_width // tn, q_scale=HEAD_DIM ** -0.5),
        extra=(cos, sin), extra_specs=(table_spec, table_spec),
        temps=[((tm, tn), F32)] * 2, row_splits=4, name="in_proj_qk")


def _inproj_rest(a, w, col0, tm=1024, tn=1024):
    m = a.shape[0]
    ncols = w.shape[1] - col0
    return _ws_matmul(
        a, w, [(lambda tile: col0 + tile * tn, tn, 0)], ncols // tn, tm,
        (m, ncols), BF16, tn, _cast_epilogue, row_splits=2, name="in_proj_rest")


def _outproj(a, w, x, gate, seq, tm, tn, name):
    m = a.shape[0]
    n = w.shape[1]
    per = seq // tm
    return _ws_matmul(
        a, w, [(lambda tile: tile * tn, tn, 0)], n // tn, tm, (m, n), F32, tn,
        _residual_epilogue, extra=(x, gate),
        extra_specs=(pl.BlockSpec((tm, tn), lambda j, i: (i, j)),
                     pl.BlockSpec((1, 1, tn), lambda j, i: (i // per, 0, j))),
        row_splits=2, name=name)


def _ffn_up(a, w, hidden, tm=2048, tg=256):
    m = a.shape[0]
    segs = [(lambda tile: tile * tg, tg, 0), (lambda tile: hidden + tile * tg, tg, tg)]
    return _ws_matmul(
        a, w, segs, hidden // tg, tm, (m, hidden), BF16, tg,
        functools.partial(_swiglu_epilogue, tg=tg), temps=[((tm, tg), F32)], row_splits=4,
        name="ffn_up")


def _plain_mm_body(a_ref, w_ref, o_ref, wbf_ref):
    @pl.when(pl.program_id(1) == 0)
    def _():
        wbf_ref[...] = w_ref[...].astype(BF16)

    o_ref[...] = jnp.dot(a_ref[...], wbf_ref[...], preferred_element_type=F32).astype(o_ref.dtype)


def _ctx_kv(a, w, col0, ncols, tm=1024, tn=512):
    m, k = a.shape
    off = col0 // tn
    est = _vmem_estimate(
        [((tm, k), BF16), ((k, tn), F32), ((tm, tn), BF16)],
        scratch=[((k, tn), BF16)], temps=[((tm, tn), F32)])
    return pl.pallas_call(
        _plain_mm_body,
        grid=(ncols // tn, m // tm),
        in_specs=[
            pl.BlockSpec((tm, k), lambda j, i: (i, 0)),
            pl.BlockSpec((k, tn), lambda j, i: (0, j + off)),
        ],
        out_specs=pl.BlockSpec((tm, tn), lambda j, i: (i, j)),
        out_shape=jax.ShapeDtypeStruct((m, ncols), BF16),
        scratch_shapes=[pltpu.VMEM((k, tn), BF16)],
        compiler_params=_params(("arbitrary", "arbitrary"), est, keep_operands_in_hbm=True),
        name="ctx_kv_proj",
    )(a, w)


def _block_classes(rows):
    last_ws = rows - ATT_KROWS
    first = (0, [0] * ATT_QROWS)
    interior = (NA_KH // 2, list(range(ATT_QROWS)))
    r0 = rows - ATT_QROWS
    last = (r0 - last_ws, [rows - NA_KH - last_ws] * ATT_QROWS)
    return [first, interior, last]


def _bias_body(rpb_ref, o_ref, *, classes):
    base = pl.program_id(0) * (RPB_ROWS * RPB_COLS)
    shape = (GRID_W, 2 * GRID_W)
    lane = lax.broadcasted_iota(jnp.int32, shape, 1)
    qc = lax.broadcasted_iota(jnp.int32, shape, 0)
    kc = lane & (GRID_W - 1)
    right = lane >= GRID_W
    dc_idx = kc - qc + (NA_KW - 1)
    cs = jnp.clip(qc - NA_KW // 2, 0, GRID_W - NA_KW)
    col_ok = (kc >= cs) & (kc < cs + NA_KW)
    neg = jnp.full(shape, -jnp.inf, F32)

    pair = []
    for d in range(-1, RPB_ROWS):
        dl = min(max(d, 0), RPB_ROWS - 1)
        dr = min(max(d + 1, 0), RPB_ROWS - 1)
        acc = jnp.zeros(shape, F32)
        for dc in range(RPB_COLS):
            val = jnp.where(right, rpb_ref[base + dr * RPB_COLS + dc],
                            rpb_ref[base + dl * RPB_COLS + dc])
            acc = jnp.where(dc_idx == dc, val, acc)
        pair.append(acc)

    for ci, (off, rs_rel) in enumerate(classes):
        for i in range(ATT_QROWS):
            for jp in range(ATT_KROWS // 2):
                jl, jr = 2 * jp, 2 * jp + 1
                ok_l = rs_rel[i] <= jl < rs_rel[i] + NA_KH
                ok_r = rs_rel[i] <= jr < rs_rel[i] + NA_KH
                d = jl - i - off + (NA_KH - 1)
                if ok_l and ok_r:
                    blk = jnp.where(col_ok, pair[d + 1], neg)
                elif ok_l:
                    blk = jnp.where(col_ok & jnp.logical_not(right), pair[d + 1], neg)
                elif ok_r:
                    blk = jnp.where(col_ok & right, pair[d + 1], neg)
                else:
                    blk = neg
                o_ref[0, ci, i * GRID_W:(i + 1) * GRID_W,
                      jp * 2 * GRID_W:(jp + 1) * 2 * GRID_W] = blk


def _bias_tables(rpb, rows):
    heads = rpb.shape[0]
    classes = _block_classes(rows)
    out_block = (1, len(classes), ATT_QB, ATT_KB)
    est = _vmem_estimate([(out_block, F32)], temps=[((GRID_W, 2 * GRID_W), F32)] * 32)
    return pl.pallas_call(
        functools.partial(_bias_body, classes=classes),
        grid=(heads,),
        in_specs=[pl.BlockSpec(memory_space=pltpu.SMEM)],
        out_specs=pl.BlockSpec(out_block, lambda h: (h, 0, 0, 0)),
        out_shape=jax.ShapeDtypeStruct((heads,) + out_block[1:], F32),
        compiler_params=_params(("arbitrary",), est, keep_operands_in_hbm=True),
        name="bias_tables",
    )(rpb.reshape(-1))


def _attn_body(q_ref, k_ref, v_ref, kc_ref, vc_ref, tab_ref, cond_ref, adaw_hbm, adab_ref,
               o_ref, mod_ref, wstage, sem, *, rows, batch, n_steps, mod_col0, mod_tn):
    step = pl.program_id(0) * batch + pl.program_id(1)
    slot = step % 2

    def chunk_copy(s, sl):
        col = mod_col0 + s * mod_tn
        col = col if isinstance(col, int) else pl.multiple_of(col, mod_tn)
        return pltpu.make_async_copy(
            adaw_hbm.at[pl.ds(0, wstage.shape[1]), pl.ds(col, mod_tn)], wstage.at[sl], sem.at[sl])

    @pl.when(step == 0)
    def _():
        chunk_copy(0, 0).start()

    chunk_copy(step, slot).wait()

    @pl.when(step + 1 < n_steps)
    def _():
        chunk_copy(step + 1, 1 - slot).start()

    mod_ref[...] = jnp.dot(_silu(cond_ref[...]).astype(BF16), wstage[slot].astype(BF16),
                           preferred_element_type=F32) + adab_ref[...]

    nt = (((1,), (1,)), ((), ()))
    kc = kc_ref[...]
    vc = vc_ref[...]
    n_blocks = rows // ATT_QROWS

    def window(blk):
        r0 = blk * ATT_QROWS
        ws = min(max(r0 - NA_KH // 2, 0), rows - ATT_KROWS)
        return r0 * GRID_W, ws * GRID_W

    def scores(blk):
        q0, k0 = window(blk)
        cls = 0 if blk == 0 else (2 if blk == n_blocks - 1 else 1)
        q = q_ref[q0:q0 + ATT_QB, :]
        s_lat = lax.dot_general(q, k_ref[k0:k0 + ATT_KB, :], nt, preferred_element_type=F32)
        s_ctx = lax.dot_general(q, kc, nt, preferred_element_type=F32)
        return s_lat + tab_ref[0, cls], s_ctx

    def probs(s_lat, s_ctx):
        mx = jnp.maximum(jnp.max(s_lat, axis=-1, keepdims=True),
                         jnp.max(s_ctx, axis=-1, keepdims=True))
        e_lat = jnp.exp(s_lat - mx)
        e_ctx = jnp.exp(s_ctx - mx)
        denom = jnp.sum(e_lat, axis=-1, keepdims=True) + jnp.sum(e_ctx, axis=-1, keepdims=True)
        return e_lat.astype(BF16), e_ctx.astype(BF16), denom

    def finish(blk, e_lat, e_ctx, denom):
        q0, k0 = window(blk)
        o = (jnp.dot(e_lat, v_ref[k0:k0 + ATT_KB, :], preferred_element_type=F32)
             + jnp.dot(e_ctx, vc, preferred_element_type=F32))
        o_ref[q0:q0 + ATT_QB, :] = (o / denom).astype(o_ref.dtype)

    nxt = scores(0)
    pending = None
    for blk in range(n_blocks):
        cur = nxt
        if blk + 1 < n_blocks:
            nxt = scores(blk + 1)
        p = probs(*cur)
        if pending is not None:
            finish(blk - 1, *pending)
        pending = p
    finish(n_blocks - 1, *pending)


def _attention(qk, rest, kvc, tabs, cond, ada_w, ada_b, mod_col0, batch, seq, heads, ctx_len):
    rows = seq // GRID_W
    tab_block = (1,) + tabs.shape[1:]
    n_steps = heads * batch
    crows, d = cond.shape
    mod_cols = ada_w.shape[1] - mod_col0
    mod_tn = mod_cols // n_steps
    assert mod_tn * n_steps == mod_cols and mod_tn % V7X_LANES == 0 and mod_col0 % mod_tn == 0
    est = _vmem_estimate(
        [((seq, HEAD_DIM), BF16)] * 4 + [((ctx_len, HEAD_DIM), BF16)] * 2 + [(tab_block, F32)],
        scratch=[((2, d, mod_tn), F32)],
        temps=[((ATT_QB, ATT_KB + ctx_len), F32)] * 12 + [((d, mod_tn), F32)])
    step = lambda h, b: h * batch + b
    return pl.pallas_call(
        functools.partial(_attn_body, rows=rows, batch=batch, n_steps=n_steps,
                          mod_col0=mod_col0, mod_tn=mod_tn),
        grid=(heads, batch),
        in_specs=[
            pl.BlockSpec((seq, HEAD_DIM), lambda h, b: (b, h)),
            pl.BlockSpec((seq, HEAD_DIM), lambda h, b: (b, heads + h)),
            pl.BlockSpec((seq, HEAD_DIM), lambda h, b: (b, h)),
            pl.BlockSpec((ctx_len, HEAD_DIM), lambda h, b: (b, h)),
            pl.BlockSpec((ctx_len, HEAD_DIM), lambda h, b: (b, heads + h)),
            pl.BlockSpec(tab_block, lambda h, b: (h, 0, 0, 0)),
            pl.BlockSpec((crows, d), lambda h, b: (0, 0)),
            pl.BlockSpec(memory_space=pl.ANY),
            pl.BlockSpec((1, mod_tn), lambda h, b: (0, mod_col0 // mod_tn + step(h, b))),
        ],
        out_specs=[
            pl.BlockSpec((seq, HEAD_DIM), lambda h, b: (b, h)),
            pl.BlockSpec((crows, mod_tn), lambda h, b: (0, step(h, b))),
        ],
        out_shape=[
            jax.ShapeDtypeStruct((batch * seq, heads * HEAD_DIM), BF16),
            jax.ShapeDtypeStruct((crows, mod_cols), F32),
        ],
        scratch_shapes=[pltpu.VMEM((2, d, mod_tn), F32), pltpu.SemaphoreType.DMA((2,))],
        compiler_params=_params(("arbitrary", "arbitrary"), est),
        name="nbr_attention",
    )(qk, qk, rest, kvc, kvc, tabs, cond, ada_w, ada_b.reshape(1, -1))


HALO = V7X_BF16_SUBLANES


def _mix_body(at_ref, u_ref, b_ref, c_ref, up_ref, cp_ref, un_ref, cn_ref, cw_ref, gg_ref,
              o_ref, *, per_seq, width):
    i = pl.program_id(0)
    ts = u_ref.shape[0]
    has_prev = jnp.where(i % per_seq == 0, 0.0, 1.0).astype(F32)
    has_next = jnp.where(i % per_seq == per_seq - 1, 0.0, 1.0).astype(F32)
    g = c_ref[...].astype(F32) * u_ref[...].astype(F32)
    g_prev = (cp_ref[...].astype(F32) * up_ref[...].astype(F32))[HALO - 1:HALO, :] * has_prev
    g_next = (cn_ref[...].astype(F32) * un_ref[...].astype(F32))[0:1, :] * has_next
    row = lax.broadcasted_iota(jnp.int32, g.shape, 0)
    below = jnp.where(row == 0, g_prev, pltpu.roll(g, 1, 0))
    above = jnp.where(row == ts - 1, g_next, pltpu.roll(g, ts - 1, 0))
    conv = cw_ref[0:1, :] * below + cw_ref[1:2, :] * g + cw_ref[2:3, :] * above
    conv_o = b_ref[...].astype(F32) * conv
    o_ref[:, :width] = _rms(at_ref[...].astype(F32), gg_ref[0:1, :]).astype(o_ref.dtype)
    o_ref[:, width:] = _rms(conv_o, gg_ref[1:2, :]).astype(o_ref.dtype)


def _mix_prep(attn, p, conv_w, group_g, seq, width, col_u, ts=256):
    m = attn.shape[0]
    cu = col_u // width
    hb = ts // HALO
    n_halo = m // HALO
    est = _vmem_estimate(
        [((ts, width), BF16)] * 4 + [((HALO, width), BF16)] * 4 + [((ts, 2 * width), BF16)],
        temps=[((ts, width), F32)] * 8)
    prev_map = lambda c: (lambda i: (jnp.maximum(i * hb - 1, 0), c))
    next_map = lambda c: (lambda i: (jnp.minimum((i + 1) * hb, n_halo - 1), c))
    return pl.pallas_call(
        functools.partial(_mix_body, per_seq=seq // ts, width=width),
        grid=(m // ts,),
        in_specs=[
            pl.BlockSpec((ts, width), lambda i: (i, 0)),
            pl.BlockSpec((ts, width), lambda i: (i, cu)),
            pl.BlockSpec((ts, width), lambda i: (i, cu + 1)),
            pl.BlockSpec((ts, width), lambda i: (i, cu + 2)),
            pl.BlockSpec((HALO, width), prev_map(cu)),
            pl.BlockSpec((HALO, width), prev_map(cu + 2)),
            pl.BlockSpec((HALO, width), next_map(cu)),
            pl.BlockSpec((HALO, width), next_map(cu + 2)),
            pl.BlockSpec((3, width), lambda i: (0, 0)),
            pl.BlockSpec((2, width), lambda i: (0, 0)),
        ],
        out_specs=pl.BlockSpec((ts, 2 * width), lambda i: (i, 0)),
        out_shape=jax.ShapeDtypeStruct((m, 2 * width), BF16),
        compiler_params=_params(("arbitrary",), est),
        name="mix_prep",
    )(attn, p, p, p, p, p, p, p, conv_w, group_g.reshape(2, width))


def _rope_tables(seq):
    rows = seq // GRID_W
    quarter = HEAD_DIM // 4
    freqs = ROPE_THETA ** (-jnp.arange(quarter, dtype=F32) / quarter)
    ang_r = jnp.arange(rows, dtype=F32)[:, None] * freqs[None, :]
    ang_c = jnp.arange(GRID_W, dtype=F32)[:, None] * freqs[None, :]
    per_row = lambda a: jnp.repeat(a, GRID_W, axis=0)
    per_col = lambda a: jnp.tile(a, (rows, 1))
    cos_r, sin_r = per_row(jnp.cos(ang_r)), per_row(jnp.sin(ang_r))
    cos_c, sin_c = per_col(jnp.cos(ang_c)), per_col(jnp.sin(ang_c))
    cos = jnp.concatenate([cos_r, cos_r, cos_c, cos_c], axis=-1)
    sin = jnp.concatenate([-sin_r, sin_r, -sin_c, sin_c], axis=-1)
    return cos, sin


def _layer(x, ctx, cond, ada_w, ada_b, norm1_g, w_in, conv_w, rpb, group_g, w_out, norm2_g,
           w_gate_up, w_down, cos, sin):
    b, s, d = x.shape
    ctx_len = ctx.shape[1]
    heads = rpb.shape[0]
    na_width = heads * HEAD_DIM
    conv_width = conv_w.shape[1]
    hidden = w_down.shape[0]
    m = b * s
    n_early = 2
    early = _adaln(cond, ada_w, ada_b, n_early * d)
    sh1, sc1 = [early[:b, None, i * d:(i + 1) * d] for i in range(n_early)]
    csh1, csc1 = [early[b:b + 1, None, i * d:(i + 1) * d] for i in range(n_early)]

    h = _norm_mod(x, norm1_g, sh1, sc1).reshape(m, d)
    hc = _norm_mod(ctx, norm1_g, csh1, csc1).reshape(b * ctx_len, d)
    qk = _inproj_qk(h, w_in, cos, sin, na_width)
    rest = _inproj_rest(h, w_in, 2 * na_width)
    kvc = _ctx_kv(hc, w_in, na_width, 2 * na_width)
    tabs = _bias_tables(rpb, s // GRID_W)
    attn, late = _attention(qk, rest, kvc, tabs, cond, ada_w, ada_b, n_early * d, b, s, heads,
                            ctx_len)
    g1, sh2, sc2, g2 = [late[:b, None, i * d:(i + 1) * d] for i in range(N_MOD - n_early)]
    y = _mix_prep(attn, rest, conv_w, group_g, s, conv_width, na_width)
    x1 = _outproj(y, w_out, x.reshape(m, d), g1, s, tm=512, tn=1024, name="out_proj")
    h2 = _norm_mod(x1.reshape(b, s, d), norm2_g, sh2, sc2).reshape(m, d)
    hid = _ffn_up(h2, w_gate_up, hidden)
    return _outproj(hid, w_down, x1, g2, s, tm=512, tn=512, name="ffn_down")


def kernel(x, c, ctx, c_ctx, ada_w, ada_b, norm1_g, w_in, conv_w, rpb, group_norm_g, w_out,
           norm2_g, w_gate_up, w_down, final_norm_g):
    b, s, d = x.shape
    depth = ada_w.shape[0]
    assert depth == 1, "context-stream update between layers is not implemented"
    assert conv_w.shape[2] == rpb.shape[1] * HEAD_DIM, "equal-width branches expected"
    cos, sin = _rope_tables(s)
    pad = (-(b + 1)) % 8
    cond = jnp.concatenate([c, c_ctx[None, :], jnp.zeros((pad, d), F32)], axis=0)
    x2 = _layer(x, ctx, cond, ada_w[0], ada_b[0], norm1_g[0], w_in[0], conv_w[0], rpb[0],
                group_norm_g[0], w_out[0], norm2_g[0], w_gate_up[0], w_down[0], cos, sin)
    return _final_norm(x2, final_norm_g).reshape(b, s, d)
```

```python
import functools

import jax
import jax.numpy as jnp
import numpy as np
from jax import lax
from jax.experimental import pallas as pl
from jax.experimental.pallas import tpu as pltpu

F32 = jnp.float32
BF16 = jnp.bfloat16

GRID_W = 64
HEAD_DIM = 128
NA_KH = 8
NA_KW = 16
ROPE_THETA = 10000.0
EPS = 1e-6
LOG2E = 1.4426950408889634
N_MOD = 6
RPB_ROWS = 2 * NA_KH - 1
RPB_COLS = 2 * NA_KW - 1

V7X_VMEM_BYTES = 64 * 1024 * 1024
V7X_LANES = 128
V7X_BF16_SUBLANES = 16
VMEM_REQUEST_CAP = 60000 * 1024

ATT_QROWS = 4
ATT_KROWS = ATT_QROWS + NA_KH
ATT_QB = ATT_QROWS * GRID_W
ATT_KB = ATT_KROWS * GRID_W


def _params(semantics, vmem_bytes, keep_operands_in_hbm=False):
    request = VMEM_REQUEST_CAP if keep_operands_in_hbm else min(vmem_bytes, VMEM_REQUEST_CAP)
    return pltpu.CompilerParams(dimension_semantics=semantics, vmem_limit_bytes=int(request))


def _nbytes(shape, dtype):
    return int(np.prod(shape)) * jnp.dtype(dtype).itemsize


def _vmem_estimate(blocks, scratch=(), temps=()):
    total = sum(2 * _nbytes(s, d) for s, d in blocks)
    total += sum(_nbytes(s, d) for s, d in scratch)
    total += sum(_nbytes(s, d) for s, d in temps)
    return total + 2 * 1024 * 1024


def _silu(x):
    return x * (1.0 / (1.0 + jnp.exp(-x)))


def _adaln_body(c_ref, w_ref, b_ref, o_ref):
    s = _silu(c_ref[...]).astype(BF16)
    w = w_ref[...].astype(BF16)
    o_ref[...] = jnp.dot(s, w, preferred_element_type=F32) + b_ref[...]


def _rms(x, g):
    return x * lax.rsqrt(jnp.mean(x * x, axis=-1, keepdims=True) + EPS) * g


def _lane_blocks(width):
    return [slice(c * V7X_LANES, (c + 1) * V7X_LANES) for c in range(width // V7X_LANES)]


NORM_ROWS = 128


def _rms_rows(load, store, n_rows, width):
    for r in range(n_rows // NORM_ROWS):
        rows = slice(r * NORM_ROWS, (r + 1) * NORM_ROWS)
        part = None
        for cols in _lane_blocks(width):
            xc = load(rows, cols)
            part = xc * xc if part is None else part + xc * xc
        mean = jnp.sum(part, axis=-1, keepdims=True) / width
        inv = jnp.broadcast_to(lax.rsqrt(mean + EPS), (NORM_ROWS, V7X_LANES))
        for cols in _lane_blocks(width):
            store(rows, cols, load(rows, cols) * inv)


def _norm_mod_body(x_ref, g_ref, sh_ref, sc_ref, o_ref):
    gain = g_ref[...] * (1.0 + sc_ref[0])
    shift = sh_ref[0]

    def store(rows, cols, xn):
        o_ref[0, rows, cols] = (xn * gain[:, cols] + shift[:, cols]).astype(o_ref.dtype)

    _rms_rows(lambda rows, cols: x_ref[0, rows, cols], store, x_ref.shape[1], x_ref.shape[2])


def _norm_mod(x3, g, shift, scale, ts=512):
    b, s, d = x3.shape
    ts = min(ts, s)
    per_batch = shift.shape[0] == b
    mod_map = (lambda bi, si: (bi, 0, 0)) if per_batch else (lambda bi, si: (0, 0, 0))
    est = _vmem_estimate(
        [((ts, d), F32), ((ts, d), BF16)], temps=[((ts, d), F32)] * 3)
    return pl.pallas_call(
        _norm_mod_body,
        grid=(b, s // ts),
        in_specs=[
            pl.BlockSpec((1, ts, d), lambda bi, si: (bi, si, 0)),
            pl.BlockSpec((1, d), lambda bi, si: (0, 0)),
            pl.BlockSpec((1, 1, d), mod_map),
            pl.BlockSpec((1, 1, d), mod_map),
        ],
        out_specs=pl.BlockSpec((1, ts, d), lambda bi, si: (bi, si, 0)),
        out_shape=jax.ShapeDtypeStruct((b, s, d), BF16),
        compiler_params=_params(("arbitrary", "arbitrary"), est, keep_operands_in_hbm=True),
        name="norm_mod",
    )(x3, g.reshape(1, d), shift, scale)


def _final_body(x_ref, g_ref, o_ref):
    gain = g_ref[...]

    def store(rows, cols, xn):
        o_ref[rows, cols] = xn * gain[:, cols]

    _rms_rows(lambda rows, cols: x_ref[rows, cols], store, x_ref.shape[0], x_ref.shape[1])


def _final_norm(x2, g, ts=512):
    m, d = x2.shape
    est = _vmem_estimate([((ts, d), F32)] * 2, temps=[((ts, d), F32)] * 3)
    return pl.pallas_call(
        _final_body,
        grid=(m // ts,),
        in_specs=[
            pl.BlockSpec((ts, d), lambda i: (i, 0)),
            pl.BlockSpec((1, d), lambda i: (0, 0)),
        ],
        out_specs=pl.BlockSpec((ts, d), lambda i: (i, 0)),
        out_shape=jax.ShapeDtypeStruct((m, d), F32),
        compiler_params=_params(("arbitrary",), est),
        name="final_norm",
    )(x2, g.reshape(1, d))


def _ws_body(*refs, n_extra, nj, ni, kc, segs, epilogue, row_splits):
    a_ref, w_hbm = refs[0], refs[1]
    extra = refs[2:2 + n_extra]
    o_ref = refs[2 + n_extra]
    wbf_even, wbf_odd, stage, sem = refs[3 + n_extra:]
    j = pl.program_id(0)
    i = pl.program_id(1)
    t = j * ni + i

    def aligned(v, multiple):
        return v if isinstance(v, int) else pl.multiple_of(v, multiple)

    def chunk_copies(tile, chunk, slot):
        row0 = aligned(chunk * kc, kc)
        return [
            pltpu.make_async_copy(
                w_hbm.at[pl.ds(row0, kc), pl.ds(aligned(col_fn(tile), V7X_LANES), width)],
                stage.at[slot, :, pl.ds(dst, width)],
                sem.at[slot])
            for col_fn, width, dst in segs
        ]

    def start(tile, chunk, slot):
        for cp in chunk_copies(tile, chunk, slot):
            cp.start()

    def wait(tile, chunk, slot):
        for cp in chunk_copies(tile, chunk, slot):
            cp.wait()

    def round_chunk(chunk, slot, dst_ref):
        row0 = aligned(chunk * kc, kc)
        dst_ref[pl.ds(row0, kc), :] = stage[slot].astype(BF16)

    @pl.when(t == 0)
    def _():
        start(0, 0, 0)
        for c in range(ni):
            if c + 1 < ni:
                start(0, c + 1, (c + 1) % 2)
            wait(0, c, c % 2)
            round_chunk(c, c % 2, wbf_even)
        start(min(1, nj - 1), 0, ni % 2)

    nxt = jnp.minimum(j + 1, nj - 1)
    slot = (t + ni) % 2
    wait(nxt, i, slot)

    @pl.when(t + 1 < nj * ni)
    def _():
        wrap = i + 1 == ni
        tile_next = jnp.where(wrap, jnp.minimum(j + 2, nj - 1), nxt)
        chunk_next = jnp.where(wrap, 0, i + 1)
        start(tile_next, chunk_next, 1 - slot)

    def compute(cur_ref, idle_ref):
        tm = a_ref.shape[0]
        for r in range(row_splits):
            rows = slice(r * tm // row_splits, (r + 1) * tm // row_splits)
            acc = jnp.dot(a_ref[rows, :], cur_ref[...], preferred_element_type=F32)
            epilogue(acc, rows, j, extra, o_ref)
        round_chunk(i, slot, idle_ref)

    @pl.when(j % 2 == 0)
    def _():
        compute(wbf_even, wbf_odd)

    @pl.when(j % 2 == 1)
    def _():
        compute(wbf_odd, wbf_even)


def _ws_matmul(a, w, segs, nj, tm, out_shape, out_dtype, out_tn, epilogue, extra=(),
               extra_specs=(), temps=(), row_splits=1, name="ws_matmul"):
    m, k = a.shape
    ni = m // tm
    assert ni * tm == m and k % ni == 0
    kc = k // ni
    assert kc % V7X_BF16_SUBLANES == 0 and tm % (row_splits * V7X_BF16_SUBLANES) == 0
    tn = sum(width for _, width, _ in segs)
    body = functools.partial(_ws_body, n_extra=len(extra), nj=nj, ni=ni, kc=kc, segs=segs,
                             epilogue=epilogue, row_splits=row_splits)
    blocks = [((tm, k), BF16), ((tm, out_tn), out_dtype)]
    blocks += [(spec.block_shape, arr.dtype) for spec, arr in zip(extra_specs, extra)]
    est = _vmem_estimate(
        blocks, scratch=[((k, tn), BF16), ((k, tn), BF16), ((2, kc, tn), F32)],
        temps=[((tm, tn), F32), ((tm, tn), F32), ((kc, tn), F32)] + list(temps))
    return pl.pallas_call(
        body,
        grid=(nj, ni),
        in_specs=[pl.BlockSpec((tm, k), lambda j, i: (i, 0)),
                  pl.BlockSpec(memory_space=pl.ANY)] + list(extra_specs),
        out_specs=pl.BlockSpec((tm, out_tn), lambda j, i: (i, j)),
        out_shape=jax.ShapeDtypeStruct(out_shape, out_dtype),
        scratch_shapes=[pltpu.VMEM((k, tn), BF16), pltpu.VMEM((k, tn), BF16),
                        pltpu.VMEM((2, kc, tn), F32), pltpu.SemaphoreType.DMA((2,))],
        compiler_params=_params(("arbitrary", "arbitrary"), est),
        name=name,
    )(a, w, *extra)


def _rope_tile(acc, cos, sin, factor):
    lane = lax.broadcasted_iota(jnp.int32, (acc.shape[0], HEAD_DIM), 1)
    low = (lane & 32) == 0
    outs = []
    for c in range(acc.shape[1] // HEAD_DIM):
        xc = acc[:, c * HEAD_DIM:(c + 1) * HEAD_DIM]
        partner = jnp.where(low, pltpu.roll(xc, HEAD_DIM - 32, 1), pltpu.roll(xc, 32, 1))
        outs.append((xc * cos + partner * sin) * factor)
    return jnp.concatenate(outs, axis=1)


def _qk_epilogue(acc, rows, j, extra, o_ref, *, n_q_tiles, q_scale):
    cos_ref, sin_ref = extra
    factor = jnp.where(j < n_q_tiles, jnp.float32(q_scale), jnp.float32(1.0))
    roped = _rope_tile(acc, cos_ref[rows, :], sin_ref[rows, :], factor)
    o_ref[rows, :] = roped.astype(o_ref.dtype)


def _cast_epilogue(acc, rows, j, extra, o_ref):
    o_ref[rows, :] = acc.astype(o_ref.dtype)


def _residual_epilogue(acc, rows, j, extra, o_ref):
    x_ref, g_ref = extra
    o_ref[rows, :] = x_ref[rows, :] + g_ref[0] * acc


def _swiglu_epilogue(acc, rows, j, extra, o_ref, *, tg):
    o_ref[rows, :] = (_silu(acc[:, :tg]) * acc[:, tg:]).astype(o_ref.dtype)


def _inproj_qk(a, w, cos, sin, na_width, tm=1024, tn=1024):
    m = a.shape[0]
    per = cos.shape[0] // tm
    table_spec = pl.BlockSpec((tm, HEAD_DIM), lambda j, i: (i % per, 0))
    return _ws_matmul(
        a, w, [(lambda tile: tile * tn, tn, 0)], 2 * na_width // tn, tm,
        (m, 2 * na_width), BF16, tn,
        functools.partial(_qk_epilogue, n_q_tiles=na_width // tn, q_scale=LOG2E * HEAD_DIM ** -0.5),
        extra=(cos, sin), extra_specs=(table_spec, table_spec),
        temps=[((tm, tn), F32)] * 2, row_splits=4, name="in_proj_qk")


def _inproj_rest(a, w, col0, tm=1024, tn=1024):
    m = a.shape[0]
    ncols = w.shape[1] - col0
    return _ws_matmul(
        a, w, [(lambda tile: col0 + tile * tn, tn, 0)], ncols // tn, tm,
        (m, ncols), BF16, tn, _cast_epilogue, row_splits=2, name="in_proj_rest")


def _outproj(a, w, x, gate, seq, tm, tn, name):
    m = a.shape[0]
    n = w.shape[1]
    per = seq // tm
    return _ws_matmul(
        a, w, [(lambda tile: tile * tn, tn, 0)], n // tn, tm, (m, n), F32, tn,
        _residual_epilogue, extra=(x, gate),
        extra_specs=(pl.BlockSpec((tm, tn), lambda j, i: (i, j)),
                     pl.BlockSpec((1, 1, tn), lambda j, i: (i // per, 0, j))),
        row_splits=2, name=name)


def _ffn_up(a, w, hidden, tm=2048, tg=256):
    m = a.shape[0]
    segs = [(lambda tile: tile * tg, tg, 0), (lambda tile: hidden + tile * tg, tg, tg)]
    return _ws_matmul(
        a, w, segs, hidden // tg, tm, (m, hidden), BF16, tg,
        functools.partial(_swiglu_epilogue, tg=tg), temps=[((tm, tg), F32)], row_splits=4,
        name="ffn_up")


def _plain_mm_body(a_ref, w_ref, o_ref, wbf_ref):
    @pl.when(pl.program_id(1) == 0)
    def _():
        wbf_ref[...] = w_ref[...].astype(BF16)

    o_ref[...] = jnp.dot(a_ref[...], wbf_ref[...], preferred_element_type=F32).astype(o_ref.dtype)


def _ctx_kv(a, w, col0, ncols, tm=1024, tn=512):
    m, k = a.shape
    off = col0 // tn
    est = _vmem_estimate(
        [((tm, k), BF16), ((k, tn), F32), ((tm, tn), BF16)],
        scratch=[((k, tn), BF16)], temps=[((tm, tn), F32)])
    return pl.pallas_call(
        _plain_mm_body,
        grid=(ncols // tn, m // tm),
        in_specs=[
            pl.BlockSpec((tm, k), lambda j, i: (i, 0)),
            pl.BlockSpec((k, tn), lambda j, i: (0, j + off)),
        ],
        out_specs=pl.BlockSpec((tm, tn), lambda j, i: (i, j)),
        out_shape=jax.ShapeDtypeStruct((m, ncols), BF16),
        scratch_shapes=[pltpu.VMEM((k, tn), BF16)],
        compiler_params=_params(("arbitrary", "arbitrary"), est, keep_operands_in_hbm=True),
        name="ctx_kv_proj",
    )(a, w)


def _block_classes(rows):
    last_ws = rows - ATT_KROWS
    first = (0, [0] * ATT_QROWS)
    interior = (NA_KH // 2, list(range(ATT_QROWS)))
    r0 = rows - ATT_QROWS
    last = (r0 - last_ws, [rows - NA_KH - last_ws] * ATT_QROWS)
    return [first, interior, last]


def _bias_body(rpb_ref, o_ref, *, classes):
    base = pl.program_id(0) * (RPB_ROWS * RPB_COLS)
    shape = (GRID_W, 2 * GRID_W)
    lane = lax.broadcasted_iota(jnp.int32, shape, 1)
    qc = lax.broadcasted_iota(jnp.int32, shape, 0)
    kc = lane & (GRID_W - 1)
    right = lane >= GRID_W
    dc_idx = kc - qc + (NA_KW - 1)
    cs = jnp.clip(qc - NA_KW // 2, 0, GRID_W - NA_KW)
    col_ok = (kc >= cs) & (kc < cs + NA_KW)
    neg = jnp.full(shape, -jnp.inf, F32)

    pair = []
    for d in range(-1, RPB_ROWS):
        dl = min(max(d, 0), RPB_ROWS - 1)
        dr = min(max(d + 1, 0), RPB_ROWS - 1)
        acc = jnp.zeros(shape, F32)
        for dc in range(RPB_COLS):
            val = jnp.where(right, rpb_ref[base + dr * RPB_COLS + dc] * LOG2E,
                            rpb_ref[base + dl * RPB_COLS + dc] * LOG2E)
            acc = jnp.where(dc_idx == dc, val, acc)
        pair.append(acc)

    for ci, (off, rs_rel) in enumerate(classes):
        for i in range(ATT_QROWS):
            for jp in range(ATT_KROWS // 2):
                jl, jr = 2 * jp, 2 * jp + 1
                ok_l = rs_rel[i] <= jl < rs_rel[i] + NA_KH
                ok_r = rs_rel[i] <= jr < rs_rel[i] + NA_KH
                d = jl - i - off + (NA_KH - 1)
                if ok_l and ok_r:
                    blk = jnp.where(col_ok, pair[d + 1], neg)
                elif ok_l:
                    blk = jnp.where(col_ok & jnp.logical_not(right), pair[d + 1], neg)
                elif ok_r:
                    blk = jnp.where(col_ok & right, pair[d + 1], neg)
                else:
                    blk = neg
                o_ref[0, ci, i * GRID_W:(i + 1) * GRID_W,
                      jp * 2 * GRID_W:(jp + 1) * 2 * GRID_W] = blk


def _adaln_bias_body(c_ref, w_ref, b_ref, rpb_ref, o_ref, tab_ref, *, classes):
    _adaln_body(c_ref, w_ref, b_ref, o_ref)
    _bias_body(rpb_ref, tab_ref, classes=classes)


def _adaln_and_bias(cond, w, b, n, rpb, rows):
    crows, d = cond.shape
    heads = rpb.shape[0]
    tn = n // heads
    assert tn * heads == n and tn % V7X_LANES == 0
    classes = _block_classes(rows)
    tab_block = (1, len(classes), ATT_QB, ATT_KB)
    est = _vmem_estimate(
        [((crows, d), F32), ((d, tn), F32), ((1, tn), F32), ((crows, tn), F32), (tab_block, F32)],
        temps=[((d, tn), BF16)] + [((GRID_W, 2 * GRID_W), F32)] * 32)
    return pl.pallas_call(
        functools.partial(_adaln_bias_body, classes=classes),
        grid=(heads,),
        in_specs=[
            pl.BlockSpec((crows, d), lambda j: (0, 0)),
            pl.BlockSpec((d, tn), lambda j: (0, j)),
            pl.BlockSpec((1, tn), lambda j: (0, j)),
            pl.BlockSpec(memory_space=pltpu.SMEM),
        ],
        out_specs=[
            pl.BlockSpec((crows, tn), lambda j: (0, j)),
            pl.BlockSpec(tab_block, lambda j: (j, 0, 0, 0)),
        ],
        out_shape=[
            jax.ShapeDtypeStruct((crows, n), F32),
            jax.ShapeDtypeStruct((heads,) + tab_block[1:], F32),
        ],
        compiler_params=_params(("arbitrary",), est, keep_operands_in_hbm=True),
        name="adaln_bias",
    )(cond, w, b.reshape(1, -1), rpb.reshape(-1))


def _attn_body(q_ref, k_ref, v_ref, kc_ref, vc_ref, tab_ref, cond_ref, adaw_hbm, adab_ref,
               o_ref, mod_ref, wstage, sem, *, rows, batch, n_steps, mod_col0, mod_tn):
    step = pl.program_id(0) * batch + pl.program_id(1)
    slot = step % 2

    def chunk_copy(s, sl):
        col = mod_col0 + s * mod_tn
        col = col if isinstance(col, int) else pl.multiple_of(col, mod_tn)
        return pltpu.make_async_copy(
            adaw_hbm.at[pl.ds(0, wstage.shape[1]), pl.ds(col, mod_tn)], wstage.at[sl], sem.at[sl])

    @pl.when(step == 0)
    def _():
        chunk_copy(0, 0).start()

    chunk_copy(step, slot).wait()

    @pl.when(step + 1 < n_steps)
    def _():
        chunk_copy(step + 1, 1 - slot).start()

    mod_ref[...] = jnp.dot(_silu(cond_ref[...]).astype(BF16), wstage[slot].astype(BF16),
                           preferred_element_type=F32) + adab_ref[...]

    nt = (((1,), (1,)), ((), ()))
    kc = kc_ref[...]
    vc = vc_ref[...]
    n_blocks = rows // ATT_QROWS

    def window(blk):
        r0 = blk * ATT_QROWS
        ws = min(max(r0 - NA_KH // 2, 0), rows - ATT_KROWS)
        return r0 * GRID_W, ws * GRID_W

    def scores(blk):
        q0, k0 = window(blk)
        cls = 0 if blk == 0 else (2 if blk == n_blocks - 1 else 1)
        q = q_ref[q0:q0 + ATT_QB, :]
        s_lat = lax.dot_general(q, k_ref[k0:k0 + ATT_KB, :], nt, preferred_element_type=F32)
        s_ctx = lax.dot_general(q, kc, nt, preferred_element_type=F32)
        return s_lat + tab_ref[0, cls], s_ctx

    def probs(s_lat, s_ctx):
        mx = jnp.maximum(jnp.max(s_lat, axis=-1, keepdims=True),
                         jnp.max(s_ctx, axis=-1, keepdims=True))
        e_lat = jnp.exp2(s_lat - mx)
        e_ctx = jnp.exp2(s_ctx - mx)
        denom = jnp.sum(e_lat, axis=-1, keepdims=True) + jnp.sum(e_ctx, axis=-1, keepdims=True)
        return e_lat.astype(BF16), e_ctx.astype(BF16), denom

    def finish(blk, e_lat, e_ctx, denom):
        q0, k0 = window(blk)
        o = (jnp.dot(e_lat, v_ref[k0:k0 + ATT_KB, :], preferred_element_type=F32)
             + jnp.dot(e_ctx, vc, preferred_element_type=F32))
        o_ref[q0:q0 + ATT_QB, :] = (o / denom).astype(o_ref.dtype)

    nxt = scores(0)
    pending = None
    for blk in range(n_blocks):
        cur = nxt
        if blk + 1 < n_blocks:
            nxt = scores(blk + 1)
        p = probs(*cur)
        if pending is not None:
            finish(blk - 1, *pending)
        pending = p
    finish(n_blocks - 1, *pending)


def _attention(qk, rest, kvc, tabs, cond, ada_w, ada_b, mod_col0, batch, seq, heads, ctx_len):
    rows = seq // GRID_W
    tab_block = (1,) + tabs.shape[1:]
    n_steps = heads * batch
    crows, d = cond.shape
    mod_cols = ada_w.shape[1] - mod_col0
    mod_tn = mod_cols // n_steps
    assert mod_tn * n_steps == mod_cols and mod_tn % V7X_LANES == 0 and mod_col0 % mod_tn == 0
    est = _vmem_estimate(
        [((seq, HEAD_DIM), BF16)] * 4 + [((ctx_len, HEAD_DIM), BF16)] * 2 + [(tab_block, F32)],
        scratch=[((2, d, mod_tn), F32)],
        temps=[((ATT_QB, ATT_KB + ctx_len), F32)] * 12 + [((d, mod_tn), F32)])
    step = lambda h, b: h * batch + b
    return pl.pallas_call(
        functools.partial(_attn_body, rows=rows, batch=batch, n_steps=n_steps,
                          mod_col0=mod_col0, mod_tn=mod_tn),
        grid=(heads, batch),
        in_specs=[
            pl.BlockSpec((seq, HEAD_DIM), lambda h, b: (b, h)),
            pl.BlockSpec((seq, HEAD_DIM), lambda h, b: (b, heads + h)),
            pl.BlockSpec((seq, HEAD_DIM), lambda h, b: (b, h)),
            pl.BlockSpec((ctx_len, HEAD_DIM), lambda h, b: (b, h)),
            pl.BlockSpec((ctx_len, HEAD_DIM), lambda h, b: (b, heads + h)),
            pl.BlockSpec(tab_block, lambda h, b: (h, 0, 0, 0)),
            pl.BlockSpec((crows, d), lambda h, b: (0, 0)),
            pl.BlockSpec(memory_space=pl.ANY),
            pl.BlockSpec((1, mod_tn), lambda h, b: (0, mod_col0 // mod_tn + step(h, b))),
        ],
        out_specs=[
            pl.BlockSpec((seq, HEAD_DIM), lambda h, b: (b, h)),
            pl.BlockSpec((crows, mod_tn), lambda h, b: (0, step(h, b))),
        ],
        out_shape=[
            jax.ShapeDtypeStruct((batch * seq, heads * HEAD_DIM), BF16),
            jax.ShapeDtypeStruct((crows, mod_cols), F32),
        ],
        scratch_shapes=[pltpu.VMEM((2, d, mod_tn), F32), pltpu.SemaphoreType.DMA((2,))],
        compiler_params=_params(("arbitrary", "arbitrary"), est),
        name="nbr_attention",
    )(qk, qk, rest, kvc, kvc, tabs, cond, ada_w, ada_b.reshape(1, -1))


HALO = V7X_BF16_SUBLANES


def _mix_body(at_ref, u_ref, b_ref, c_ref, up_ref, cp_ref, un_ref, cn_ref, cw_ref, gg_ref,
              o_ref, *, per_seq, width):
    i = pl.program_id(0)
    ts = u_ref.shape[0]
    has_prev = jnp.where(i % per_seq == 0, 0.0, 1.0).astype(F32)
    has_next = jnp.where(i % per_seq == per_seq - 1, 0.0, 1.0).astype(F32)
    g = c_ref[...].astype(F32) * u_ref[...].astype(F32)
    g_prev = (cp_ref[...].astype(F32) * up_ref[...].astype(F32))[HALO - 1:HALO, :] * has_prev
    g_next = (cn_ref[...].astype(F32) * un_ref[...].astype(F32))[0:1, :] * has_next
    row = lax.broadcasted_iota(jnp.int32, g.shape, 0)
    below = jnp.where(row == 0, g_prev, pltpu.roll(g, 1, 0))
    above = jnp.where(row == ts - 1, g_next, pltpu.roll(g, ts - 1, 0))
    conv = cw_ref[0:1, :] * below + cw_ref[1:2, :] * g + cw_ref[2:3, :] * above
    conv_o = b_ref[...].astype(F32) * conv
    o_ref[:, :width] = _rms(at_ref[...].astype(F32), gg_ref[0:1, :]).astype(o_ref.dtype)
    o_ref[:, width:] = _rms(conv_o, gg_ref[1:2, :]).astype(o_ref.dtype)


def _mix_prep(attn, p, conv_w, group_g, seq, width, col_u, ts=256):
    m = attn.shape[0]
    cu = col_u // width
    hb = ts // HALO
    n_halo = m // HALO
    est = _vmem_estimate(
        [((ts, width), BF16)] * 4 + [((HALO, width), BF16)] * 4 + [((ts, 2 * width), BF16)],
        temps=[((ts, width), F32)] * 8)
    prev_map = lambda c: (lambda i: (jnp.maximum(i * hb - 1, 0), c))
    next_map = lambda c: (lambda i: (jnp.minimum((i + 1) * hb, n_halo - 1), c))
    return pl.pallas_call(
        functools.partial(_mix_body, per_seq=seq // ts, width=width),
        grid=(m // ts,),
        in_specs=[
            pl.BlockSpec((ts, width), lambda i: (i, 0)),
            pl.BlockSpec((ts, width), lambda i: (i, cu)),
            pl.BlockSpec((ts, width), lambda i: (i, cu + 1)),
            pl.BlockSpec((ts, width), lambda i: (i, cu + 2)),
            pl.BlockSpec((HALO, width), prev_map(cu)),
            pl.BlockSpec((HALO, width), prev_map(cu + 2)),
            pl.BlockSpec((HALO, width), next_map(cu)),
            pl.BlockSpec((HALO, width), next_map(cu + 2)),
            pl.BlockSpec((3, width), lambda i: (0, 0)),
            pl.BlockSpec((2, width), lambda i: (0, 0)),
        ],
        out_specs=pl.BlockSpec((ts, 2 * width), lambda i: (i, 0)),
        out_shape=jax.ShapeDtypeStruct((m, 2 * width), BF16),
        compiler_params=_params(("arbitrary",), est),
        name="mix_prep",
    )(attn, p, p, p, p, p, p, p, conv_w, group_g.reshape(2, width))


def _rope_tables(seq):
    rows = seq // GRID_W
    quarter = HEAD_DIM // 4
    freqs = ROPE_THETA ** (-jnp.arange(quarter, dtype=F32) / quarter)
    ang_r = jnp.arange(rows, dtype=F32)[:, None] * freqs[None, :]
    ang_c = jnp.arange(GRID_W, dtype=F32)[:, None] * freqs[None, :]
    per_row = lambda a: jnp.repeat(a, GRID_W, axis=0)
    per_col = lambda a: jnp.tile(a, (rows, 1))
    cos_r, sin_r = per_row(jnp.cos(ang_r)), per_row(jnp.sin(ang_r))
    cos_c, sin_c = per_col(jnp.cos(ang_c)), per_col(jnp.sin(ang_c))
    cos = jnp.concatenate([cos_r, cos_r, cos_c, cos_c], axis=-1)
    sin = jnp.concatenate([-sin_r, sin_r, -sin_c, sin_c], axis=-1)
    return cos, sin


def _layer(x, ctx, cond, ada_w, ada_b, norm1_g, w_in, conv_w, rpb, group_g, w_out, norm2_g,
           w_gate_up, w_down, cos, sin):
    b, s, d = x.shape
    ctx_len = ctx.shape[1]
    heads = rpb.shape[0]
    na_width = heads * HEAD_DIM
    conv_width = conv_w.shape[1]
    hidden = w_down.shape[0]
    m = b * s
    n_early = 2
    early, tabs = _adaln_and_bias(cond, ada_w, ada_b, n_early * d, rpb, s // GRID_W)
    sh1, sc1 = [early[:b, None, i * d:(i + 1) * d] for i in range(n_early)]
    csh1, csc1 = [early[b:b + 1, None, i * d:(i + 1) * d] for i in range(n_early)]

    h = _norm_mod(x, norm1_g, sh1, sc1).reshape(m, d)
    hc = _norm_mod(ctx, norm1_g, csh1, csc1).reshape(b * ctx_len, d)
    qk = _inproj_qk(h, w_in, cos, sin, na_width)
    rest = _inproj_rest(h, w_in, 2 * na_width)
    kvc = _ctx_kv(hc, w_in, na_width, 2 * na_width)
    attn, late = _attention(qk, rest, kvc, tabs, cond, ada_w, ada_b, n_early * d, b, s, heads,
                            ctx_len)
    g1, sh2, sc2, g2 = [late[:b, None, i * d:(i + 1) * d] for i in range(N_MOD - n_early)]
    y = _mix_prep(attn, rest, conv_w, group_g, s, conv_width, na_width)
    x1 = _outproj(y, w_out, x.reshape(m, d), g1, s, tm=512, tn=1024, name="out_proj")
    h2 = _norm_mod(x1.reshape(b, s, d), norm2_g, sh2, sc2).reshape(m, d)
    hid = _ffn_up(h2, w_gate_up, hidden)
    return _outproj(hid, w_down, x1, g2, s, tm=512, tn=512, name="ffn_down")


def kernel(x, c, ctx, c_ctx, ada_w, ada_b, norm1_g, w_in, conv_w, rpb, group_norm_g, w_out,
           norm2_g, w_gate_up, w_down, final_norm_g):
    b, s, d = x.shape
    depth = ada_w.shape[0]
    assert depth == 1, "context-stream update between layers is not implemented"
    assert conv_w.shape[2] == rpb.shape[1] * HEAD_DIM, "equal-width branches expected"
    cos, sin = _rope_tables(s)
    pad = (-(b + 1)) % 8
    cond = jnp.concatenate([c, c_ctx[None, :], jnp.zeros((pad, d), F32)], axis=0)
    x2 = _layer(x, ctx, cond, ada_w[0], ada_b[0], norm1_g[0], w_in[0], conv_w[0], rpb[0],
                group_norm_g[0], w_out[0], norm2_g[0], w_gate_up[0], w_down[0], cos, sin)
    return _final_norm(x2, final_norm_g).reshape(b, s, d)
```

```python
import functools

import jax
import jax.numpy as jnp
import numpy as np
from jax import lax
from jax.experimental import pallas as pl
from jax.experimental.pallas import tpu as pltpu

F32 = jnp.float32
BF16 = jnp.bfloat16

GRID_W = 64
HEAD_DIM = 128
NA_KH = 8
NA_KW = 16
ROPE_THETA = 10000.0
EPS = 1e-6
LOG2E = 1.4426950408889634
N_MOD = 6
RPB_ROWS = 2 * NA_KH - 1
RPB_COLS = 2 * NA_KW - 1

V7X_VMEM_BYTES = 64 * 1024 * 1024
V7X_LANES = 128
V7X_BF16_SUBLANES = 16
VMEM_REQUEST_CAP = 60000 * 1024

ATT_QROWS = 4
ATT_KROWS = ATT_QROWS + NA_KH
ATT_QB = ATT_QROWS * GRID_W
ATT_KB = ATT_KROWS * GRID_W


def _params(semantics, vmem_bytes, keep_operands_in_hbm=False):
    request = VMEM_REQUEST_CAP if keep_operands_in_hbm else min(vmem_bytes, VMEM_REQUEST_CAP)
    return pltpu.CompilerParams(dimension_semantics=semantics, vmem_limit_bytes=int(request))


def _nbytes(shape, dtype):
    return int(np.prod(shape)) * jnp.dtype(dtype).itemsize


def _vmem_estimate(blocks, scratch=(), temps=()):
    total = sum(2 * _nbytes(s, d) for s, d in blocks)
    total += sum(_nbytes(s, d) for s, d in scratch)
    total += sum(_nbytes(s, d) for s, d in temps)
    return total + 2 * 1024 * 1024


def _silu(x):
    return x * (1.0 / (1.0 + jnp.exp(-x)))


def _adaln_body(c_ref, w_ref, b_ref, o_ref):
    s = _silu(c_ref[...]).astype(BF16)
    w = w_ref[...].astype(BF16)
    o_ref[...] = jnp.dot(s, w, preferred_element_type=F32) + b_ref[...]


def _rms(x, g):
    return x * lax.rsqrt(jnp.mean(x * x, axis=-1, keepdims=True) + EPS) * g


def _lane_blocks(width):
    return [slice(c * V7X_LANES, (c + 1) * V7X_LANES) for c in range(width // V7X_LANES)]


NORM_ROWS = 128


def _rms_rows(load, store, n_rows, width):
    for r in range(n_rows // NORM_ROWS):
        rows = slice(r * NORM_ROWS, (r + 1) * NORM_ROWS)
        part = None
        for cols in _lane_blocks(width):
            xc = load(rows, cols)
            part = xc * xc if part is None else part + xc * xc
        mean = jnp.sum(part, axis=-1, keepdims=True) / width
        inv = jnp.broadcast_to(lax.rsqrt(mean + EPS), (NORM_ROWS, V7X_LANES))
        for cols in _lane_blocks(width):
            store(rows, cols, load(rows, cols) * inv)


def _norm_mod_body(x_ref, g_ref, sh_ref, sc_ref, o_ref):
    gain = g_ref[...] * (1.0 + sc_ref[0])
    shift = sh_ref[0]

    def store(rows, cols, xn):
        o_ref[0, rows, cols] = (xn * gain[:, cols] + shift[:, cols]).astype(o_ref.dtype)

    _rms_rows(lambda rows, cols: x_ref[0, rows, cols], store, x_ref.shape[1], x_ref.shape[2])


def _norm_mod(x3, g, shift, scale, ts=512):
    b, s, d = x3.shape
    ts = min(ts, s)
    per_batch = shift.shape[0] == b
    mod_map = (lambda bi, si: (bi, 0, 0)) if per_batch else (lambda bi, si: (0, 0, 0))
    est = _vmem_estimate(
        [((ts, d), F32), ((ts, d), BF16)], temps=[((ts, d), F32)] * 3)
    return pl.pallas_call(
        _norm_mod_body,
        grid=(b, s // ts),
        in_specs=[
            pl.BlockSpec((1, ts, d), lambda bi, si: (bi, si, 0)),
            pl.BlockSpec((1, d), lambda bi, si: (0, 0)),
            pl.BlockSpec((1, 1, d), mod_map),
            pl.BlockSpec((1, 1, d), mod_map),
        ],
        out_specs=pl.BlockSpec((1, ts, d), lambda bi, si: (bi, si, 0)),
        out_shape=jax.ShapeDtypeStruct((b, s, d), BF16),
        compiler_params=_params(("arbitrary", "arbitrary"), est, keep_operands_in_hbm=True),
        name="norm_mod",
    )(x3, g.reshape(1, d), shift, scale)


def _final_body(x_ref, g_ref, o_ref):
    gain = g_ref[...]

    def store(rows, cols, xn):
        o_ref[rows, cols] = xn * gain[:, cols]

    _rms_rows(lambda rows, cols: x_ref[rows, cols], store, x_ref.shape[0], x_ref.shape[1])


def _final_norm(x2, g, ts=512):
    m, d = x2.shape
    est = _vmem_estimate([((ts, d), F32)] * 2, temps=[((ts, d), F32)] * 3)
    return pl.pallas_call(
        _final_body,
        grid=(m // ts,),
        in_specs=[
            pl.BlockSpec((ts, d), lambda i: (i, 0)),
            pl.BlockSpec((1, d), lambda i: (0, 0)),
        ],
        out_specs=pl.BlockSpec((ts, d), lambda i: (i, 0)),
        out_shape=jax.ShapeDtypeStruct((m, d), F32),
        compiler_params=_params(("arbitrary",), est),
        name="final_norm",
    )(x2, g.reshape(1, d))


def _ws_body(*refs, n_extra, nj, ni, kc, segs, epilogue, row_splits):
    a_ref, w_hbm = refs[0], refs[1]
    extra = refs[2:2 + n_extra]
    o_ref = refs[2 + n_extra]
    wbf_even, wbf_odd, stage, sem = refs[3 + n_extra:]
    j = pl.program_id(0)
    i = pl.program_id(1)
    t = j * ni + i

    def aligned(v, multiple):
        return v if isinstance(v, int) else pl.multiple_of(v, multiple)

    def chunk_copies(tile, chunk, slot):
        row0 = aligned(chunk * kc, kc)
        return [
            pltpu.make_async_copy(
                w_hbm.at[pl.ds(row0, kc), pl.ds(aligned(col_fn(tile), V7X_LANES), width)],
                stage.at[slot, :, pl.ds(dst, width)],
                sem.at[slot])
            for col_fn, width, dst in segs
        ]

    def start(tile, chunk, slot):
        for cp in chunk_copies(tile, chunk, slot):
            cp.start()

    def wait(tile, chunk, slot):
        for cp in chunk_copies(tile, chunk, slot):
            cp.wait()

    def round_chunk(chunk, slot, dst_ref):
        row0 = aligned(chunk * kc, kc)
        dst_ref[pl.ds(row0, kc), :] = stage[slot].astype(BF16)

    @pl.when(t == 0)
    def _():
        start(0, 0, 0)
        for c in range(ni):
            if c + 1 < ni:
                start(0, c + 1, (c + 1) % 2)
            wait(0, c, c % 2)
            round_chunk(c, c % 2, wbf_even)
        start(min(1, nj - 1), 0, ni % 2)

    nxt = jnp.minimum(j + 1, nj - 1)
    slot = (t + ni) % 2
    wait(nxt, i, slot)

    @pl.when(t + 1 < nj * ni)
    def _():
        wrap = i + 1 == ni
        tile_next = jnp.where(wrap, jnp.minimum(j + 2, nj - 1), nxt)
        chunk_next = jnp.where(wrap, 0, i + 1)
        start(tile_next, chunk_next, 1 - slot)

    def compute(cur_ref, idle_ref):
        tm = a_ref.shape[0]
        for r in range(row_splits):
            rows = slice(r * tm // row_splits, (r + 1) * tm // row_splits)
            acc = jnp.dot(a_ref[rows, :], cur_ref[...], preferred_element_type=F32)
            epilogue(acc, rows, j, i, extra, o_ref)
        round_chunk(i, slot, idle_ref)

    @pl.when(j % 2 == 0)
    def _():
        compute(wbf_even, wbf_odd)

    @pl.when(j % 2 == 1)
    def _():
        compute(wbf_odd, wbf_even)


def _ws_matmul(a, w, segs, nj, tm, out_shape, out_dtype, out_tn, epilogue, extra=(),
               extra_specs=(), temps=(), row_splits=1, name="ws_matmul"):
    m, k = a.shape
    ni = m // tm
    assert ni * tm == m and k % ni == 0
    kc = k // ni
    assert kc % V7X_BF16_SUBLANES == 0 and tm % (row_splits * V7X_BF16_SUBLANES) == 0
    tn = sum(width for _, width, _ in segs)
    body = functools.partial(_ws_body, n_extra=len(extra), nj=nj, ni=ni, kc=kc, segs=segs,
                             epilogue=epilogue, row_splits=row_splits)
    blocks = [((tm, k), BF16), ((tm, out_tn), out_dtype)]
    blocks += [(spec.block_shape, arr.dtype) for spec, arr in zip(extra_specs, extra)]
    est = _vmem_estimate(
        blocks, scratch=[((k, tn), BF16), ((k, tn), BF16), ((2, kc, tn), F32)],
        temps=[((tm, tn), F32), ((tm, tn), F32), ((kc, tn), F32)] + list(temps))
    return pl.pallas_call(
        body,
        grid=(nj, ni),
        in_specs=[pl.BlockSpec((tm, k), lambda j, i: (i, 0)),
                  pl.BlockSpec(memory_space=pl.ANY)] + list(extra_specs),
        out_specs=pl.BlockSpec((tm, out_tn), lambda j, i: (i, j)),
        out_shape=jax.ShapeDtypeStruct(out_shape, out_dtype),
        scratch_shapes=[pltpu.VMEM((k, tn), BF16), pltpu.VMEM((k, tn), BF16),
                        pltpu.VMEM((2, kc, tn), F32), pltpu.SemaphoreType.DMA((2,))],
        compiler_params=_params(("arbitrary", "arbitrary"), est),
        name=name,
    )(a, w, *extra)


def _rope_tile(acc, cos, sin, factor):
    lane = lax.broadcasted_iota(jnp.int32, (acc.shape[0], HEAD_DIM), 1)
    low = (lane & 32) == 0
    outs = []
    for c in range(acc.shape[1] // HEAD_DIM):
        xc = acc[:, c * HEAD_DIM:(c + 1) * HEAD_DIM]
        partner = jnp.where(low, pltpu.roll(xc, HEAD_DIM - 32, 1), pltpu.roll(xc, 32, 1))
        outs.append((xc * cos + partner * sin) * factor)
    return jnp.concatenate(outs, axis=1)


def _qk_epilogue(acc, rows, j, i, extra, o_ref, *, n_q_tiles, q_scale, tiles_per_seq):
    cos_ref, sin_ref = extra
    n = rows.stop - rows.start
    tm = o_ref.shape[0]
    pos = pl.ds(pl.multiple_of((i % tiles_per_seq) * tm + rows.start, n), n)
    factor = jnp.where(j < n_q_tiles, jnp.float32(q_scale), jnp.float32(1.0))
    roped = _rope_tile(acc, cos_ref[pos, :], sin_ref[pos, :], factor)
    o_ref[rows, :] = roped.astype(o_ref.dtype)


def _cast_epilogue(acc, rows, j, i, extra, o_ref):
    o_ref[rows, :] = acc.astype(o_ref.dtype)


def _residual_epilogue(acc, rows, j, i, extra, o_ref, *, tiles_per_seq):
    x_ref, g_ref = extra
    o_ref[rows, :] = x_ref[rows, :] + g_ref[i // tiles_per_seq, j] * acc


def _swiglu_epilogue(acc, rows, j, i, extra, o_ref, *, tg):
    o_ref[rows, :] = (_silu(acc[:, :tg]) * acc[:, tg:]).astype(o_ref.dtype)


def _inproj_qk(a, w, cos, sin, na_width, tm=1024, tn=1024):
    m = a.shape[0]
    seq = cos.shape[0]
    table_spec = pl.BlockSpec((seq, HEAD_DIM), lambda j, i: (0, 0))
    return _ws_matmul(
        a, w, [(lambda tile: tile * tn, tn, 0)], 2 * na_width // tn, tm,
        (m, 2 * na_width), BF16, tn,
        functools.partial(_qk_epilogue, n_q_tiles=na_width // tn,
                          q_scale=LOG2E * HEAD_DIM ** -0.5, tiles_per_seq=seq // tm),
        extra=(cos, sin), extra_specs=(table_spec, table_spec),
        temps=[((tm, tn), F32)] * 2, row_splits=4, name="in_proj_qk")


def _inproj_rest(a, w, col0, tm=1024, tn=1024):
    m = a.shape[0]
    ncols = w.shape[1] - col0
    return _ws_matmul(
        a, w, [(lambda tile: col0 + tile * tn, tn, 0)], ncols // tn, tm,
        (m, ncols), BF16, tn, _cast_epilogue, row_splits=2, name="in_proj_rest")


def _outproj(a, w, x, gate, seq, tm, tn, name):
    m = a.shape[0]
    n = w.shape[1]
    batch = gate.shape[0]
    gate_tiles = gate.reshape(batch, n // tn, 1, tn)
    return _ws_matmul(
        a, w, [(lambda tile: tile * tn, tn, 0)], n // tn, tm, (m, n), F32, tn,
        functools.partial(_residual_epilogue, tiles_per_seq=seq // tm), extra=(x, gate_tiles),
        extra_specs=(pl.BlockSpec((tm, tn), lambda j, i: (i, j)),
                     pl.BlockSpec(gate_tiles.shape, lambda j, i: (0, 0, 0, 0))),
        row_splits=2, name=name)


def _ffn_up(a, w, hidden, tm=2048, tg=256):
    m = a.shape[0]
    segs = [(lambda tile: tile * tg, tg, 0), (lambda tile: hidden + tile * tg, tg, tg)]
    return _ws_matmul(
        a, w, segs, hidden // tg, tm, (m, hidden), BF16, tg,
        functools.partial(_swiglu_epilogue, tg=tg), temps=[((tm, tg), F32)], row_splits=4,
        name="ffn_up")


def _plain_mm_body(a_ref, w_ref, o_ref, wbf_ref):
    @pl.when(pl.program_id(1) == 0)
    def _():
        wbf_ref[...] = w_ref[...].astype(BF16)

    o_ref[...] = jnp.dot(a_ref[...], wbf_ref[...], preferred_element_type=F32).astype(o_ref.dtype)


def _ctx_kv(a, w, col0, ncols, tm=1024, tn=512):
    m, k = a.shape
    off = col0 // tn
    est = _vmem_estimate(
        [((tm, k), BF16), ((k, tn), F32), ((tm, tn), BF16)],
        scratch=[((k, tn), BF16)], temps=[((tm, tn), F32)])
    return pl.pallas_call(
        _plain_mm_body,
        grid=(ncols // tn, m // tm),
        in_specs=[
            pl.BlockSpec((tm, k), lambda j, i: (i, 0)),
            pl.BlockSpec((k, tn), lambda j, i: (0, j + off)),
        ],
        out_specs=pl.BlockSpec((tm, tn), lambda j, i: (i, j)),
        out_shape=jax.ShapeDtypeStruct((m, ncols), BF16),
        scratch_shapes=[pltpu.VMEM((k, tn), BF16)],
        compiler_params=_params(("arbitrary", "arbitrary"), est, keep_operands_in_hbm=True),
        name="ctx_kv_proj",
    )(a, w)


def _block_classes(rows):
    last_ws = rows - ATT_KROWS
    first = (0, [0] * ATT_QROWS)
    interior = (NA_KH // 2, list(range(ATT_QROWS)))
    r0 = rows - ATT_QROWS
    last = (r0 - last_ws, [rows - NA_KH - last_ws] * ATT_QROWS)
    return [first, interior, last]


def _bias_body(rpb_ref, o_ref, *, classes):
    base = pl.program_id(0) * (RPB_ROWS * RPB_COLS)
    shape = (GRID_W, 2 * GRID_W)
    lane = lax.broadcasted_iota(jnp.int32, shape, 1)
    qc = lax.broadcasted_iota(jnp.int32, shape, 0)
    kc = lane & (GRID_W - 1)
    right = lane >= GRID_W
    dc_idx = kc - qc + (NA_KW - 1)
    cs = jnp.clip(qc - NA_KW // 2, 0, GRID_W - NA_KW)
    col_ok = (kc >= cs) & (kc < cs + NA_KW)
    neg = jnp.full(shape, -jnp.inf, F32)

    pair = []
    for d in range(-1, RPB_ROWS):
        dl = min(max(d, 0), RPB_ROWS - 1)
        dr = min(max(d + 1, 0), RPB_ROWS - 1)
        acc = jnp.zeros(shape, F32)
        for dc in range(RPB_COLS):
            val = jnp.where(right, rpb_ref[base + dr * RPB_COLS + dc] * LOG2E,
                            rpb_ref[base + dl * RPB_COLS + dc] * LOG2E)
            acc = jnp.where(dc_idx == dc, val, acc)
        pair.append(acc)

    for ci, (off, rs_rel) in enumerate(classes):
        for i in range(ATT_QROWS):
            for jp in range(ATT_KROWS // 2):
                jl, jr = 2 * jp, 2 * jp + 1
                ok_l = rs_rel[i] <= jl < rs_rel[i] + NA_KH
                ok_r = rs_rel[i] <= jr < rs_rel[i] + NA_KH
                d = jl - i - off + (NA_KH - 1)
                if ok_l and ok_r:
                    blk = jnp.where(col_ok, pair[d + 1], neg)
                elif ok_l:
                    blk = jnp.where(col_ok & jnp.logical_not(right), pair[d + 1], neg)
                elif ok_r:
                    blk = jnp.where(col_ok & right, pair[d + 1], neg)
                else:
                    blk = neg
                o_ref[0, ci, i * GRID_W:(i + 1) * GRID_W,
                      jp * 2 * GRID_W:(jp + 1) * 2 * GRID_W] = blk


def _adaln_bias_body(c_ref, w_ref, b_ref, rpb_ref, o_ref, tab_ref, *, classes):
    _adaln_body(c_ref, w_ref, b_ref, o_ref)
    _bias_body(rpb_ref, tab_ref, classes=classes)


def _adaln_and_bias(cond, w, b, n, rpb, rows):
    crows, d = cond.shape
    heads = rpb.shape[0]
    tn = n // heads
    assert tn * heads == n and tn % V7X_LANES == 0
    classes = _block_classes(rows)
    tab_block = (1, len(classes), ATT_QB, ATT_KB)
    est = _vmem_estimate(
        [((crows, d), F32), ((d, tn), F32), ((1, tn), F32), ((crows, tn), F32), (tab_block, F32)],
        temps=[((d, tn), BF16)] + [((GRID_W, 2 * GRID_W), F32)] * 32)
    return pl.pallas_call(
        functools.partial(_adaln_bias_body, classes=classes),
        grid=(heads,),
        in_specs=[
            pl.BlockSpec((crows, d), lambda j: (0, 0)),
            pl.BlockSpec((d, tn), lambda j: (0, j)),
            pl.BlockSpec((1, tn), lambda j: (0, j)),
            pl.BlockSpec(memory_space=pltpu.SMEM),
        ],
        out_specs=[
            pl.BlockSpec((crows, tn), lambda j: (0, j)),
            pl.BlockSpec(tab_block, lambda j: (j, 0, 0, 0)),
        ],
        out_shape=[
            jax.ShapeDtypeStruct((crows, n), F32),
            jax.ShapeDtypeStruct((heads,) + tab_block[1:], F32),
        ],
        compiler_params=_params(("arbitrary",), est, keep_operands_in_hbm=True),
        name="adaln_bias",
    )(cond, w, b.reshape(1, -1), rpb.reshape(-1))


def _attn_body(q_ref, k_ref, v_ref, kc_ref, vc_ref, tab_ref, cond_ref, adaw_hbm, adab_ref,
               o_ref, mod_ref, wstage, sem, *, rows, batch, n_steps, mod_col0, mod_tn):
    step = pl.program_id(0) * batch + pl.program_id(1)
    slot = step % 2

    def chunk_copy(s, sl):
        col = mod_col0 + s * mod_tn
        col = col if isinstance(col, int) else pl.multiple_of(col, mod_tn)
        return pltpu.make_async_copy(
            adaw_hbm.at[pl.ds(0, wstage.shape[1]), pl.ds(col, mod_tn)], wstage.at[sl], sem.at[sl])

    @pl.when(step == 0)
    def _():
        chunk_copy(0, 0).start()

    chunk_copy(step, slot).wait()

    @pl.when(step + 1 < n_steps)
    def _():
        chunk_copy(step + 1, 1 - slot).start()

    mod_ref[...] = jnp.dot(_silu(cond_ref[...]).astype(BF16), wstage[slot].astype(BF16),
                           preferred_element_type=F32) + adab_ref[...]

    nt = (((1,), (1,)), ((), ()))
    kc = kc_ref[...]
    vc = vc_ref[...]
    n_blocks = rows // ATT_QROWS

    def window(blk):
        r0 = blk * ATT_QROWS
        ws = min(max(r0 - NA_KH // 2, 0), rows - ATT_KROWS)
        return r0 * GRID_W, ws * GRID_W

    def scores(blk):
        q0, k0 = window(blk)
        cls = 0 if blk == 0 else (2 if blk == n_blocks - 1 else 1)
        q = q_ref[q0:q0 + ATT_QB, :]
        s_lat = lax.dot_general(q, k_ref[k0:k0 + ATT_KB, :], nt, preferred_element_type=F32)
        s_ctx = lax.dot_general(q, kc, nt, preferred_element_type=F32)
        return s_lat + tab_ref[0, cls], s_ctx

    def probs(s_lat, s_ctx):
        mx = jnp.maximum(jnp.max(s_lat, axis=-1, keepdims=True),
                         jnp.max(s_ctx, axis=-1, keepdims=True))
        e_lat = jnp.exp2(s_lat - mx)
        e_ctx = jnp.exp2(s_ctx - mx)
        denom = jnp.sum(e_lat, axis=-1, keepdims=True) + jnp.sum(e_ctx, axis=-1, keepdims=True)
        return e_lat.astype(BF16), e_ctx.astype(BF16), denom

    def finish(blk, e_lat, e_ctx, denom):
        q0, k0 = window(blk)
        o = (jnp.dot(e_lat, v_ref[k0:k0 + ATT_KB, :], preferred_element_type=F32)
             + jnp.dot(e_ctx, vc, preferred_element_type=F32))
        o_ref[q0:q0 + ATT_QB, :] = (o / denom).astype(o_ref.dtype)

    nxt = scores(0)
    pending = None
    for blk in range(n_blocks):
        cur = nxt
        if blk + 1 < n_blocks:
            nxt = scores(blk + 1)
        p = probs(*cur)
        if pending is not None:
            finish(blk - 1, *pending)
        pending = p
    finish(n_blocks - 1, *pending)


def _attention(qk, rest, kvc, tabs, cond, ada_w, ada_b, mod_col0, batch, seq, heads, ctx_len):
    rows = seq // GRID_W
    tab_block = (1,) + tabs.shape[1:]
    n_steps = heads * batch
    crows, d = cond.shape
    mod_cols = ada_w.shape[1] - mod_col0
    mod_tn = mod_cols // n_steps
    assert mod_tn * n_steps == mod_cols and mod_tn % V7X_LANES == 0 and mod_col0 % mod_tn == 0
    est = _vmem_estimate(
        [((seq, HEAD_DIM), BF16)] * 4 + [((ctx_len, HEAD_DIM), BF16)] * 2 + [(tab_block, F32)],
        scratch=[((2, d, mod_tn), F32)],
        temps=[((ATT_QB, ATT_KB + ctx_len), F32)] * 12 + [((d, mod_tn), F32)])
    step = lambda h, b: h * batch + b
    return pl.pallas_call(
        functools.partial(_attn_body, rows=rows, batch=batch, n_steps=n_steps,
                          mod_col0=mod_col0, mod_tn=mod_tn),
        grid=(heads, batch),
        in_specs=[
            pl.BlockSpec((seq, HEAD_DIM), lambda h, b: (b, h)),
            pl.BlockSpec((seq, HEAD_DIM), lambda h, b: (b, heads + h)),
            pl.BlockSpec((seq, HEAD_DIM), lambda h, b: (b, h)),
            pl.BlockSpec((ctx_len, HEAD_DIM), lambda h, b: (b, h)),
            pl.BlockSpec((ctx_len, HEAD_DIM), lambda h, b: (b, heads + h)),
            pl.BlockSpec(tab_block, lambda h, b: (h, 0, 0, 0)),
            pl.BlockSpec((crows, d), lambda h, b: (0, 0)),
            pl.BlockSpec(memory_space=pl.ANY),
            pl.BlockSpec((1, mod_tn), lambda h, b: (0, mod_col0 // mod_tn + step(h, b))),
        ],
        out_specs=[
            pl.BlockSpec((seq, HEAD_DIM), lambda h, b: (b, h)),
            pl.BlockSpec((crows, mod_tn), lambda h, b: (0, step(h, b))),
        ],
        out_shape=[
            jax.ShapeDtypeStruct((batch * seq, heads * HEAD_DIM), BF16),
            jax.ShapeDtypeStruct((crows, mod_cols), F32),
        ],
        scratch_shapes=[pltpu.VMEM((2, d, mod_tn), F32), pltpu.SemaphoreType.DMA((2,))],
        compiler_params=_params(("arbitrary", "arbitrary"), est),
        name="nbr_attention",
    )(qk, qk, rest, kvc, kvc, tabs, cond, ada_w, ada_b.reshape(1, -1))


HALO = V7X_BF16_SUBLANES


def _mix_body(at_ref, u_ref, b_ref, c_ref, up_ref, cp_ref, un_ref, cn_ref, cw_ref, gg_ref,
              o_ref, *, per_seq, width):
    i = pl.program_id(0)
    ts = u_ref.shape[0]
    has_prev = jnp.where(i % per_seq == 0, 0.0, 1.0).astype(F32)
    has_next = jnp.where(i % per_seq == per_seq - 1, 0.0, 1.0).astype(F32)
    g = c_ref[...].astype(F32) * u_ref[...].astype(F32)
    g_prev = (cp_ref[...].astype(F32) * up_ref[...].astype(F32))[HALO - 1:HALO, :] * has_prev
    g_next = (cn_ref[...].astype(F32) * un_ref[...].astype(F32))[0:1, :] * has_next
    row = lax.broadcasted_iota(jnp.int32, g.shape, 0)
    below = jnp.where(row == 0, g_prev, pltpu.roll(g, 1, 0))
    above = jnp.where(row == ts - 1, g_next, pltpu.roll(g, ts - 1, 0))
    conv = cw_ref[0:1, :] * below + cw_ref[1:2, :] * g + cw_ref[2:3, :] * above
    conv_o = b_ref[...].astype(F32) * conv
    o_ref[:, :width] = _rms(at_ref[...].astype(F32), gg_ref[0:1, :]).astype(o_ref.dtype)
    o_ref[:, width:] = _rms(conv_o, gg_ref[1:2, :]).astype(o_ref.dtype)


def _mix_prep(attn, p, conv_w, group_g, seq, width, col_u, ts=256):
    m = attn.shape[0]
    cu = col_u // width
    hb = ts // HALO
    n_halo = m // HALO
    est = _vmem_estimate(
        [((ts, width), BF16)] * 4 + [((HALO, width), BF16)] * 4 + [((ts, 2 * width), BF16)],
        temps=[((ts, width), F32)] * 8)
    prev_map = lambda c: (lambda i: (jnp.maximum(i * hb - 1, 0), c))
    next_map = lambda c: (lambda i: (jnp.minimum((i + 1) * hb, n_halo - 1), c))
    return pl.pallas_call(
        functools.partial(_mix_body, per_seq=seq // ts, width=width),
        grid=(m // ts,),
        in_specs=[
            pl.BlockSpec((ts, width), lambda i: (i, 0)),
            pl.BlockSpec((ts, width), lambda i: (i, cu)),
            pl.BlockSpec((ts, width), lambda i: (i, cu + 1)),
            pl.BlockSpec((ts, width), lambda i: (i, cu + 2)),
            pl.BlockSpec((HALO, width), prev_map(cu)),
            pl.BlockSpec((HALO, width), prev_map(cu + 2)),
            pl.BlockSpec((HALO, width), next_map(cu)),
            pl.BlockSpec((HALO, width), next_map(cu + 2)),
            pl.BlockSpec((3, width), lambda i: (0, 0)),
            pl.BlockSpec((2, width), lambda i: (0, 0)),
        ],
        out_specs=pl.BlockSpec((ts, 2 * width), lambda i: (i, 0)),
        out_shape=jax.ShapeDtypeStruct((m, 2 * width), BF16),
        compiler_params=_params(("arbitrary",), est),
        name="mix_prep",
    )(attn, p, p, p, p, p, p, p, conv_w, group_g.reshape(2, width))


def _rope_tables(seq):
    rows = seq // GRID_W
    quarter = HEAD_DIM // 4
    freqs = ROPE_THETA ** (-jnp.arange(quarter, dtype=F32) / quarter)
    ang_r = jnp.arange(rows, dtype=F32)[:, None] * freqs[None, :]
    ang_c = jnp.arange(GRID_W, dtype=F32)[:, None] * freqs[None, :]
    per_row = lambda a: jnp.repeat(a, GRID_W, axis=0)
    per_col = lambda a: jnp.tile(a, (rows, 1))
    cos_r, sin_r = per_row(jnp.cos(ang_r)), per_row(jnp.sin(ang_r))
    cos_c, sin_c = per_col(jnp.cos(ang_c)), per_col(jnp.sin(ang_c))
    cos = jnp.concatenate([cos_r, cos_r, cos_c, cos_c], axis=-1)
    sin = jnp.concatenate([-sin_r, sin_r, -sin_c, sin_c], axis=-1)
    return cos, sin


def _layer(x, ctx, cond, ada_w, ada_b, norm1_g, w_in, conv_w, rpb, group_g, w_out, norm2_g,
           w_gate_up, w_down, cos, sin):
    b, s, d = x.shape
    ctx_len = ctx.shape[1]
    heads = rpb.shape[0]
    na_width = heads * HEAD_DIM
    conv_width = conv_w.shape[1]
    hidden = w_down.shape[0]
    m = b * s
    n_early = 2
    early, tabs = _adaln_and_bias(cond, ada_w, ada_b, n_early * d, rpb, s // GRID_W)
    sh1, sc1 = [early[:b, None, i * d:(i + 1) * d] for i in range(n_early)]
    csh1, csc1 = [early[b:b + 1, None, i * d:(i + 1) * d] for i in range(n_early)]

    h = _norm_mod(x, norm1_g, sh1, sc1).reshape(m, d)
    hc = _norm_mod(ctx, norm1_g, csh1, csc1).reshape(b * ctx_len, d)
    qk = _inproj_qk(h, w_in, cos, sin, na_width)
    rest = _inproj_rest(h, w_in, 2 * na_width)
    kvc = _ctx_kv(hc, w_in, na_width, 2 * na_width)
    attn, late = _attention(qk, rest, kvc, tabs, cond, ada_w, ada_b, n_early * d, b, s, heads,
                            ctx_len)
    g1, sh2, sc2, g2 = [late[:b, None, i * d:(i + 1) * d] for i in range(N_MOD - n_early)]
    y = _mix_prep(attn, rest, conv_w, group_g, s, conv_width, na_width)
    x1 = _outproj(y, w_out, x.reshape(m, d), g1, s, tm=512, tn=1024, name="out_proj")
    h2 = _norm_mod(x1.reshape(b, s, d), norm2_g, sh2, sc2).reshape(m, d)
    hid = _ffn_up(h2, w_gate_up, hidden)
    return _outproj(hid, w_down, x1, g2, s, tm=512, tn=512, name="ffn_down")


def kernel(x, c, ctx, c_ctx, ada_w, ada_b, norm1_g, w_in, conv_w, rpb, group_norm_g, w_out,
           norm2_g, w_gate_up, w_down, final_norm_g):
    b, s, d = x.shape
    depth = ada_w.shape[0]
    assert depth == 1, "context-stream update between layers is not implemented"
    assert conv_w.shape[2] == rpb.shape[1] * HEAD_DIM, "equal-width branches expected"
    cos, sin = _rope_tables(s)
    pad = (-(b + 1)) % 8
    cond = jnp.concatenate([c, c_ctx[None, :], jnp.zeros((pad, d), F32)], axis=0)
    x2 = _layer(x, ctx, cond, ada_w[0], ada_b[0], norm1_g[0], w_in[0], conv_w[0], rpb[0],
                group_norm_g[0], w_out[0], norm2_g[0], w_gate_up[0], w_down[0], cos, sin)
    return _final_norm(x2, final_norm_g).reshape(b, s, d)
```

```python
import functools

import jax
import jax.numpy as jnp
import numpy as np
from jax import lax
from jax.experimental import pallas as pl
from jax.experimental.pallas import tpu as pltpu

F32 = jnp.float32
BF16 = jnp.bfloat16

GRID_W = 64
HEAD_DIM = 128
NA_KH = 8
NA_KW = 16
ROPE_THETA = 10000.0
EPS = 1e-6
LOG2E = 1.4426950408889634
N_MOD = 6
RPB_ROWS = 2 * NA_KH - 1
RPB_COLS = 2 * NA_KW - 1

V7X_VMEM_BYTES = 64 * 1024 * 1024
V7X_LANES = 128
V7X_BF16_SUBLANES = 16
VMEM_REQUEST_CAP = 60000 * 1024

ATT_QROWS = 4
ATT_KROWS = ATT_QROWS + NA_KH
ATT_QB = ATT_QROWS * GRID_W
ATT_KB = ATT_KROWS * GRID_W


def _params(semantics, vmem_bytes, keep_operands_in_hbm=False):
    request = VMEM_REQUEST_CAP if keep_operands_in_hbm else min(vmem_bytes, VMEM_REQUEST_CAP)
    return pltpu.CompilerParams(dimension_semantics=semantics, vmem_limit_bytes=int(request))


def _nbytes(shape, dtype):
    return int(np.prod(shape)) * jnp.dtype(dtype).itemsize


def _vmem_estimate(blocks, scratch=(), temps=()):
    total = sum(2 * _nbytes(s, d) for s, d in blocks)
    total += sum(_nbytes(s, d) for s, d in scratch)
    total += sum(_nbytes(s, d) for s, d in temps)
    return total + 2 * 1024 * 1024


def _silu(x):
    return x * (1.0 / (1.0 + jnp.exp(-x)))


def _adaln_body(c_ref, w_ref, b_ref, o_ref):
    s = _silu(c_ref[...]).astype(BF16)
    w = w_ref[...].astype(BF16)
    o_ref[...] = jnp.dot(s, w, preferred_element_type=F32) + b_ref[...]


def _rms(x, g):
    return x * lax.rsqrt(jnp.mean(x * x, axis=-1, keepdims=True) + EPS) * g


def _lane_blocks(width):
    return [slice(c * V7X_LANES, (c + 1) * V7X_LANES) for c in range(width // V7X_LANES)]


NORM_ROWS = 128


def _rms_rows(load, store, n_rows, width):
    for r in range(n_rows // NORM_ROWS):
        rows = slice(r * NORM_ROWS, (r + 1) * NORM_ROWS)
        part = None
        for cols in _lane_blocks(width):
            xc = load(rows, cols)
            part = xc * xc if part is None else part + xc * xc
        mean = jnp.sum(part, axis=-1, keepdims=True) / width
        inv = jnp.broadcast_to(lax.rsqrt(mean + EPS), (NORM_ROWS, V7X_LANES))
        for cols in _lane_blocks(width):
            store(rows, cols, load(rows, cols) * inv)


def _norm_mod_body(x_ref, g_ref, sh_ref, sc_ref, o_ref):
    gain = g_ref[...] * (1.0 + sc_ref[0])
    shift = sh_ref[0]

    def store(rows, cols, xn):
        o_ref[0, rows, cols] = (xn * gain[:, cols] + shift[:, cols]).astype(o_ref.dtype)

    _rms_rows(lambda rows, cols: x_ref[0, rows, cols], store, x_ref.shape[1], x_ref.shape[2])


def _norm_mod(x3, g, shift, scale, ts=512):
    b, s, d = x3.shape
    ts = min(ts, s)
    per_batch = shift.shape[0] == b
    mod_map = (lambda bi, si: (bi, 0, 0)) if per_batch else (lambda bi, si: (0, 0, 0))
    est = _vmem_estimate(
        [((ts, d), F32), ((ts, d), BF16)], temps=[((ts, d), F32)] * 3)
    return pl.pallas_call(
        _norm_mod_body,
        grid=(b, s // ts),
        in_specs=[
            pl.BlockSpec((1, ts, d), lambda bi, si: (bi, si, 0)),
            pl.BlockSpec((1, d), lambda bi, si: (0, 0)),
            pl.BlockSpec((1, 1, d), mod_map),
            pl.BlockSpec((1, 1, d), mod_map),
        ],
        out_specs=pl.BlockSpec((1, ts, d), lambda bi, si: (bi, si, 0)),
        out_shape=jax.ShapeDtypeStruct((b, s, d), BF16),
        compiler_params=_params(("arbitrary", "arbitrary"), est, keep_operands_in_hbm=True),
        name="norm_mod",
    )(x3, g.reshape(1, d), shift, scale)


def _final_body(x_ref, g_ref, o_ref):
    gain = g_ref[...]

    def store(rows, cols, xn):
        o_ref[rows, cols] = xn * gain[:, cols]

    _rms_rows(lambda rows, cols: x_ref[rows, cols], store, x_ref.shape[0], x_ref.shape[1])


def _final_norm(x2, g, ts=512):
    m, d = x2.shape
    est = _vmem_estimate([((ts, d), F32)] * 2, temps=[((ts, d), F32)] * 3)
    return pl.pallas_call(
        _final_body,
        grid=(m // ts,),
        in_specs=[
            pl.BlockSpec((ts, d), lambda i: (i, 0)),
            pl.BlockSpec((1, d), lambda i: (0, 0)),
        ],
        out_specs=pl.BlockSpec((ts, d), lambda i: (i, 0)),
        out_shape=jax.ShapeDtypeStruct((m, d), F32),
        compiler_params=_params(("arbitrary",), est),
        name="final_norm",
    )(x2, g.reshape(1, d))


def _ws_body(*refs, n_extra, nj, ni, kc, segs, epilogue, row_splits):
    a_ref, w_hbm = refs[0], refs[1]
    extra = refs[2:2 + n_extra]
    o_ref = refs[2 + n_extra]
    wbf_even, wbf_odd, stage, sem = refs[3 + n_extra:]
    j = pl.program_id(0)
    i = pl.program_id(1)
    t = j * ni + i

    def aligned(v, multiple):
        return v if isinstance(v, int) else pl.multiple_of(v, multiple)

    def chunk_copies(tile, chunk, slot):
        row0 = aligned(chunk * kc, kc)
        return [
            pltpu.make_async_copy(
                w_hbm.at[pl.ds(row0, kc), pl.ds(aligned(col_fn(tile), V7X_LANES), width)],
                stage.at[slot, :, pl.ds(dst, width)],
                sem.at[slot])
            for col_fn, width, dst in segs
        ]

    def start(tile, chunk, slot):
        for cp in chunk_copies(tile, chunk, slot):
            cp.start()

    def wait(tile, chunk, slot):
        for cp in chunk_copies(tile, chunk, slot):
            cp.wait()

    def round_chunk(chunk, slot, dst_ref):
        row0 = aligned(chunk * kc, kc)
        dst_ref[pl.ds(row0, kc), :] = stage[slot].astype(BF16)

    @pl.when(t == 0)
    def _():
        start(0, 0, 0)
        for c in range(ni):
            if c + 1 < ni:
                start(0, c + 1, (c + 1) % 2)
            wait(0, c, c % 2)
            round_chunk(c, c % 2, wbf_even)
        start(min(1, nj - 1), 0, ni % 2)

    nxt = jnp.minimum(j + 1, nj - 1)
    slot = (t + ni) % 2
    wait(nxt, i, slot)

    @pl.when(t + 1 < nj * ni)
    def _():
        wrap = i + 1 == ni
        tile_next = jnp.where(wrap, jnp.minimum(j + 2, nj - 1), nxt)
        chunk_next = jnp.where(wrap, 0, i + 1)
        start(tile_next, chunk_next, 1 - slot)

    def compute(cur_ref, idle_ref):
        tm = a_ref.shape[0]
        for r in range(row_splits):
            rows = slice(r * tm // row_splits, (r + 1) * tm // row_splits)
            acc = jnp.dot(a_ref[rows, :], cur_ref[...], preferred_element_type=F32)
            epilogue(acc, rows, j, i, extra, o_ref)
        round_chunk(i, slot, idle_ref)

    @pl.when(j % 2 == 0)
    def _():
        compute(wbf_even, wbf_odd)

    @pl.when(j % 2 == 1)
    def _():
        compute(wbf_odd, wbf_even)


def _ws_matmul(a, w, segs, nj, tm, out_shape, out_dtype, out_tn, epilogue, extra=(),
               extra_specs=(), temps=(), row_splits=1, name="ws_matmul"):
    m, k = a.shape
    ni = m // tm
    assert ni * tm == m and k % ni == 0
    kc = k // ni
    assert kc % V7X_BF16_SUBLANES == 0 and tm % (row_splits * V7X_BF16_SUBLANES) == 0
    tn = sum(width for _, width, _ in segs)
    body = functools.partial(_ws_body, n_extra=len(extra), nj=nj, ni=ni, kc=kc, segs=segs,
                             epilogue=epilogue, row_splits=row_splits)
    blocks = [((tm, k), BF16), ((tm, out_tn), out_dtype)]
    blocks += [(spec.block_shape, arr.dtype) for spec, arr in zip(extra_specs, extra)]
    est = _vmem_estimate(
        blocks, scratch=[((k, tn), BF16), ((k, tn), BF16), ((2, kc, tn), F32)],
        temps=[((tm, tn), F32), ((tm, tn), F32), ((kc, tn), F32)] + list(temps))
    return pl.pallas_call(
        body,
        grid=(nj, ni),
        in_specs=[pl.BlockSpec((tm, k), lambda j, i: (i, 0)),
                  pl.BlockSpec(memory_space=pl.ANY)] + list(extra_specs),
        out_specs=pl.BlockSpec((tm, out_tn), lambda j, i: (i, j)),
        out_shape=jax.ShapeDtypeStruct(out_shape, out_dtype),
        scratch_shapes=[pltpu.VMEM((k, tn), BF16), pltpu.VMEM((k, tn), BF16),
                        pltpu.VMEM((2, kc, tn), F32), pltpu.SemaphoreType.DMA((2,))],
        compiler_params=_params(("arbitrary", "arbitrary"), est),
        name=name,
    )(a, w, *extra)


def _rope_tile(acc, cos, sin, factor):
    lane = lax.broadcasted_iota(jnp.int32, (acc.shape[0], HEAD_DIM), 1)
    low = (lane & 32) == 0
    outs = []
    for c in range(acc.shape[1] // HEAD_DIM):
        xc = acc[:, c * HEAD_DIM:(c + 1) * HEAD_DIM]
        partner = jnp.where(low, pltpu.roll(xc, HEAD_DIM - 32, 1), pltpu.roll(xc, 32, 1))
        outs.append((xc * cos + partner * sin) * factor)
    return jnp.concatenate(outs, axis=1)


def _qk_epilogue(acc, rows, j, i, extra, o_ref, *, n_q_tiles, q_scale, tiles_per_seq):
    cos_ref, sin_ref = extra
    n = rows.stop - rows.start
    tm = o_ref.shape[0]
    pos = pl.ds(pl.multiple_of((i % tiles_per_seq) * tm + rows.start, n), n)
    factor = jnp.where(j < n_q_tiles, jnp.float32(q_scale), jnp.float32(1.0))
    roped = _rope_tile(acc, cos_ref[pos, :], sin_ref[pos, :], factor)
    o_ref[rows, :] = roped.astype(o_ref.dtype)


def _cast_epilogue(acc, rows, j, i, extra, o_ref):
    o_ref[rows, :] = acc.astype(o_ref.dtype)


def _residual_epilogue(acc, rows, j, i, extra, o_ref, *, tiles_per_seq):
    x_ref, g_ref = extra
    o_ref[rows, :] = x_ref[rows, :] + g_ref[i // tiles_per_seq, j] * acc


def _swiglu_epilogue(acc, rows, j, i, extra, o_ref, *, tg):
    o_ref[rows, :] = (_silu(acc[:, :tg]) * acc[:, tg:]).astype(o_ref.dtype)


def _inproj_qk(a, w, cos, sin, na_width, tm=1024, tn=1024):
    m = a.shape[0]
    seq = cos.shape[0]
    table_spec = pl.BlockSpec((seq, HEAD_DIM), lambda j, i: (0, 0))
    return _ws_matmul(
        a, w, [(lambda tile: tile * tn, tn, 0)], 2 * na_width // tn, tm,
        (m, 2 * na_width), BF16, tn,
        functools.partial(_qk_epilogue, n_q_tiles=na_width // tn,
                          q_scale=LOG2E * HEAD_DIM ** -0.5, tiles_per_seq=seq // tm),
        extra=(cos, sin), extra_specs=(table_spec, table_spec),
        temps=[((tm, tn), F32)] * 2, row_splits=4, name="in_proj_qk")


def _inproj_rest(a, w, col0, tm=2048, tn=512):
    m = a.shape[0]
    ncols = w.shape[1] - col0
    return _ws_matmul(
        a, w, [(lambda tile: col0 + tile * tn, tn, 0)], ncols // tn, tm,
        (m, ncols), BF16, tn, _cast_epilogue, row_splits=4, name="in_proj_rest")


def _outproj(a, w, x, gate, seq, tm, tn, name):
    m = a.shape[0]
    n = w.shape[1]
    batch = gate.shape[0]
    gate_tiles = gate.reshape(batch, n // tn, 1, tn)
    return _ws_matmul(
        a, w, [(lambda tile: tile * tn, tn, 0)], n // tn, tm, (m, n), F32, tn,
        functools.partial(_residual_epilogue, tiles_per_seq=seq // tm), extra=(x, gate_tiles),
        extra_specs=(pl.BlockSpec((tm, tn), lambda j, i: (i, j)),
                     pl.BlockSpec(gate_tiles.shape, lambda j, i: (0, 0, 0, 0))),
        row_splits=2, name=name)


def _ffn_up(a, w, hidden, tm=2048, tg=256):
    m = a.shape[0]
    segs = [(lambda tile: tile * tg, tg, 0), (lambda tile: hidden + tile * tg, tg, tg)]
    return _ws_matmul(
        a, w, segs, hidden // tg, tm, (m, hidden), BF16, tg,
        functools.partial(_swiglu_epilogue, tg=tg), temps=[((tm, tg), F32)], row_splits=4,
        name="ffn_up")


def _plain_mm_body(a_ref, w_ref, o_ref, wbf_ref):
    @pl.when(pl.program_id(1) == 0)
    def _():
        wbf_ref[...] = w_ref[...].astype(BF16)

    o_ref[...] = jnp.dot(a_ref[...], wbf_ref[...], preferred_element_type=F32).astype(o_ref.dtype)


def _ctx_kv(a, w, col0, ncols, tm=1024, tn=512):
    m, k = a.shape
    off = col0 // tn
    est = _vmem_estimate(
        [((tm, k), BF16), ((k, tn), F32), ((tm, tn), BF16)],
        scratch=[((k, tn), BF16)], temps=[((tm, tn), F32)])
    return pl.pallas_call(
        _plain_mm_body,
        grid=(ncols // tn, m // tm),
        in_specs=[
            pl.BlockSpec((tm, k), lambda j, i: (i, 0)),
            pl.BlockSpec((k, tn), lambda j, i: (0, j + off)),
        ],
        out_specs=pl.BlockSpec((tm, tn), lambda j, i: (i, j)),
        out_shape=jax.ShapeDtypeStruct((m, ncols), BF16),
        scratch_shapes=[pltpu.VMEM((k, tn), BF16)],
        compiler_params=_params(("arbitrary", "arbitrary"), est, keep_operands_in_hbm=True),
        name="ctx_kv_proj",
    )(a, w)


def _block_classes(rows):
    last_ws = rows - ATT_KROWS
    first = (0, [0] * ATT_QROWS)
    interior = (NA_KH // 2, list(range(ATT_QROWS)))
    r0 = rows - ATT_QROWS
    last = (r0 - last_ws, [rows - NA_KH - last_ws] * ATT_QROWS)
    return [first, interior, last]


def _bias_body(rpb_ref, o_ref, *, classes):
    base = pl.program_id(0) * (RPB_ROWS * RPB_COLS)
    shape = (GRID_W, 2 * GRID_W)
    lane = lax.broadcasted_iota(jnp.int32, shape, 1)
    qc = lax.broadcasted_iota(jnp.int32, shape, 0)
    kc = lane & (GRID_W - 1)
    right = lane >= GRID_W
    dc_idx = kc - qc + (NA_KW - 1)
    cs = jnp.clip(qc - NA_KW // 2, 0, GRID_W - NA_KW)
    col_ok = (kc >= cs) & (kc < cs + NA_KW)
    neg = jnp.full(shape, -jnp.inf, F32)

    pair = []
    for d in range(-1, RPB_ROWS):
        dl = min(max(d, 0), RPB_ROWS - 1)
        dr = min(max(d + 1, 0), RPB_ROWS - 1)
        acc = jnp.zeros(shape, F32)
        for dc in range(RPB_COLS):
            val = jnp.where(right, rpb_ref[base + dr * RPB_COLS + dc] * LOG2E,
                            rpb_ref[base + dl * RPB_COLS + dc] * LOG2E)
            acc = jnp.where(dc_idx == dc, val, acc)
        pair.append(acc)

    for ci, (off, rs_rel) in enumerate(classes):
        for i in range(ATT_QROWS):
            for jp in range(ATT_KROWS // 2):
                jl, jr = 2 * jp, 2 * jp + 1
                ok_l = rs_rel[i] <= jl < rs_rel[i] + NA_KH
                ok_r = rs_rel[i] <= jr < rs_rel[i] + NA_KH
                d = jl - i - off + (NA_KH - 1)
                if ok_l and ok_r:
                    blk = jnp.where(col_ok, pair[d + 1], neg)
                elif ok_l:
                    blk = jnp.where(col_ok & jnp.logical_not(right), pair[d + 1], neg)
                elif ok_r:
                    blk = jnp.where(col_ok & right, pair[d + 1], neg)
                else:
                    blk = neg
                o_ref[0, ci, i * GRID_W:(i + 1) * GRID_W,
                      jp * 2 * GRID_W:(jp + 1) * 2 * GRID_W] = blk


def _adaln_bias_body(c_ref, w_ref, b_ref, rpb_ref, o_ref, tab_ref, *, classes):
    _adaln_body(c_ref, w_ref, b_ref, o_ref)
    _bias_body(rpb_ref, tab_ref, classes=classes)


def _adaln_and_bias(cond, w, b, n, rpb, rows):
    crows, d = cond.shape
    heads = rpb.shape[0]
    tn = n // heads
    assert tn * heads == n and tn % V7X_LANES == 0
    classes = _block_classes(rows)
    tab_block = (1, len(classes), ATT_QB, ATT_KB)
    est = _vmem_estimate(
        [((crows, d), F32), ((d, tn), F32), ((1, tn), F32), ((crows, tn), F32), (tab_block, F32)],
        temps=[((d, tn), BF16)] + [((GRID_W, 2 * GRID_W), F32)] * 32)
    return pl.pallas_call(
        functools.partial(_adaln_bias_body, classes=classes),
        grid=(heads,),
        in_specs=[
            pl.BlockSpec((crows, d), lambda j: (0, 0)),
            pl.BlockSpec((d, tn), lambda j: (0, j)),
            pl.BlockSpec((1, tn), lambda j: (0, j)),
            pl.BlockSpec(memory_space=pltpu.SMEM),
        ],
        out_specs=[
            pl.BlockSpec((crows, tn), lambda j: (0, j)),
            pl.BlockSpec(tab_block, lambda j: (j, 0, 0, 0)),
        ],
        out_shape=[
            jax.ShapeDtypeStruct((crows, n), F32),
            jax.ShapeDtypeStruct((heads,) + tab_block[1:], F32),
        ],
        compiler_params=_params(("arbitrary",), est, keep_operands_in_hbm=True),
        name="adaln_bias",
    )(cond, w, b.reshape(1, -1), rpb.reshape(-1))


def _attn_body(q_ref, k_ref, v_ref, kc_ref, vc_ref, tab_ref, cond_ref, adaw_hbm, adab_ref,
               o_ref, mod_ref, wstage, sem, *, rows, batch, n_steps, mod_col0, mod_tn):
    step = pl.program_id(0) * batch + pl.program_id(1)
    slot = step % 2

    def chunk_copy(s, sl):
        col = mod_col0 + s * mod_tn
        col = col if isinstance(col, int) else pl.multiple_of(col, mod_tn)
        return pltpu.make_async_copy(
            adaw_hbm.at[pl.ds(0, wstage.shape[1]), pl.ds(col, mod_tn)], wstage.at[sl], sem.at[sl])

    @pl.when(step == 0)
    def _():
        chunk_copy(0, 0).start()

    chunk_copy(step, slot).wait()

    @pl.when(step + 1 < n_steps)
    def _():
        chunk_copy(step + 1, 1 - slot).start()

    mod_ref[...] = jnp.dot(_silu(cond_ref[...]).astype(BF16), wstage[slot].astype(BF16),
                           preferred_element_type=F32) + adab_ref[...]

    nt = (((1,), (1,)), ((), ()))
    kc = kc_ref[...]
    vc = vc_ref[...]
    n_blocks = rows // ATT_QROWS

    def window(blk):
        r0 = blk * ATT_QROWS
        ws = min(max(r0 - NA_KH // 2, 0), rows - ATT_KROWS)
        return r0 * GRID_W, ws * GRID_W

    def scores(blk):
        q0, k0 = window(blk)
        cls = 0 if blk == 0 else (2 if blk == n_blocks - 1 else 1)
        q = q_ref[q0:q0 + ATT_QB, :]
        s_lat = lax.dot_general(q, k_ref[k0:k0 + ATT_KB, :], nt, preferred_element_type=F32)
        s_ctx = lax.dot_general(q, kc, nt, preferred_element_type=F32)
        return s_lat + tab_ref[0, cls], s_ctx

    def probs(s_lat, s_ctx):
        mx = jnp.maximum(jnp.max(s_lat, axis=-1, keepdims=True),
                         jnp.max(s_ctx, axis=-1, keepdims=True))
        e_lat = jnp.exp2(s_lat - mx)
        e_ctx = jnp.exp2(s_ctx - mx)
        denom = jnp.sum(e_lat, axis=-1, keepdims=True) + jnp.sum(e_ctx, axis=-1, keepdims=True)
        return e_lat.astype(BF16), e_ctx.astype(BF16), denom

    def finish(blk, e_lat, e_ctx, denom):
        q0, k0 = window(blk)
        o = (jnp.dot(e_lat, v_ref[k0:k0 + ATT_KB, :], preferred_element_type=F32)
             + jnp.dot(e_ctx, vc, preferred_element_type=F32))
        o_ref[q0:q0 + ATT_QB, :] = (o / denom).astype(o_ref.dtype)

    nxt = scores(0)
    pending = None
    for blk in range(n_blocks):
        cur = nxt
        if blk + 1 < n_blocks:
            nxt = scores(blk + 1)
        if pending is not None:
            finish(blk - 1, *pending)
        pending = probs(*cur)
    finish(n_blocks - 1, *pending)


def _attention(qk, rest, kvc, tabs, cond, ada_w, ada_b, mod_col0, batch, seq, heads, ctx_len):
    rows = seq // GRID_W
    tab_block = (1,) + tabs.shape[1:]
    n_steps = heads * batch
    crows, d = cond.shape
    mod_cols = ada_w.shape[1] - mod_col0
    mod_tn = mod_cols // n_steps
    assert mod_tn * n_steps == mod_cols and mod_tn % V7X_LANES == 0 and mod_col0 % mod_tn == 0
    est = _vmem_estimate(
        [((seq, HEAD_DIM), BF16)] * 4 + [((ctx_len, HEAD_DIM), BF16)] * 2 + [(tab_block, F32)],
        scratch=[((2, d, mod_tn), F32)],
        temps=[((ATT_QB, ATT_KB + ctx_len), F32)] * 12 + [((d, mod_tn), F32)])
    step = lambda h, b: h * batch + b
    return pl.pallas_call(
        functools.partial(_attn_body, rows=rows, batch=batch, n_steps=n_steps,
                          mod_col0=mod_col0, mod_tn=mod_tn),
        grid=(heads, batch),
        in_specs=[
            pl.BlockSpec((seq, HEAD_DIM), lambda h, b: (b, h)),
            pl.BlockSpec((seq, HEAD_DIM), lambda h, b: (b, heads + h)),
            pl.BlockSpec((seq, HEAD_DIM), lambda h, b: (b, h)),
            pl.BlockSpec((ctx_len, HEAD_DIM), lambda h, b: (b, h)),
            pl.BlockSpec((ctx_len, HEAD_DIM), lambda h, b: (b, heads + h)),
            pl.BlockSpec(tab_block, lambda h, b: (h, 0, 0, 0)),
            pl.BlockSpec((crows, d), lambda h, b: (0, 0)),
            pl.BlockSpec(memory_space=pl.ANY),
            pl.BlockSpec((1, mod_tn), lambda h, b: (0, mod_col0 // mod_tn + step(h, b))),
        ],
        out_specs=[
            pl.BlockSpec((seq, HEAD_DIM), lambda h, b: (b, h)),
            pl.BlockSpec((crows, mod_tn), lambda h, b: (0, step(h, b))),
        ],
        out_shape=[
            jax.ShapeDtypeStruct((batch * seq, heads * HEAD_DIM), BF16),
            jax.ShapeDtypeStruct((crows, mod_cols), F32),
        ],
        scratch_shapes=[pltpu.VMEM((2, d, mod_tn), F32), pltpu.SemaphoreType.DMA((2,))],
        compiler_params=_params(("arbitrary", "arbitrary"), est),
        name="nbr_attention",
    )(qk, qk, rest, kvc, kvc, tabs, cond, ada_w, ada_b.reshape(1, -1))


HALO = V7X_BF16_SUBLANES


def _mix_body(at_ref, u_ref, b_ref, c_ref, up_ref, cp_ref, un_ref, cn_ref, cw_ref, gg_ref,
              o_ref, *, per_seq, width):
    i = pl.program_id(0)
    ts = u_ref.shape[0]
    has_prev = jnp.where(i % per_seq == 0, 0.0, 1.0).astype(F32)
    has_next = jnp.where(i % per_seq == per_seq - 1, 0.0, 1.0).astype(F32)
    g = c_ref[...].astype(F32) * u_ref[...].astype(F32)
    g_prev = (cp_ref[...].astype(F32) * up_ref[...].astype(F32))[HALO - 1:HALO, :] * has_prev
    g_next = (cn_ref[...].astype(F32) * un_ref[...].astype(F32))[0:1, :] * has_next
    row = lax.broadcasted_iota(jnp.int32, g.shape, 0)
    below = jnp.where(row == 0, g_prev, pltpu.roll(g, 1, 0))
    above = jnp.where(row == ts - 1, g_next, pltpu.roll(g, ts - 1, 0))
    conv = cw_ref[0:1, :] * below + cw_ref[1:2, :] * g + cw_ref[2:3, :] * above
    conv_o = b_ref[...].astype(F32) * conv
    o_ref[:, :width] = _rms(at_ref[...].astype(F32), gg_ref[0:1, :]).astype(o_ref.dtype)
    o_ref[:, width:] = _rms(conv_o, gg_ref[1:2, :]).astype(o_ref.dtype)


def _mix_prep(attn, p, conv_w, group_g, seq, width, col_u, ts=256):
    m = attn.shape[0]
    cu = col_u // width
    hb = ts // HALO
    n_halo = m // HALO
    est = _vmem_estimate(
        [((ts, width), BF16)] * 4 + [((HALO, width), BF16)] * 4 + [((ts, 2 * width), BF16)],
        temps=[((ts, width), F32)] * 8)
    prev_map = lambda c: (lambda i: (jnp.maximum(i * hb - 1, 0), c))
    next_map = lambda c: (lambda i: (jnp.minimum((i + 1) * hb, n_halo - 1), c))
    return pl.pallas_call(
        functools.partial(_mix_body, per_seq=seq // ts, width=width),
        grid=(m // ts,),
        in_specs=[
            pl.BlockSpec((ts, width), lambda i: (i, 0)),
            pl.BlockSpec((ts, width), lambda i: (i, cu)),
            pl.BlockSpec((ts, width), lambda i: (i, cu + 1)),
            pl.BlockSpec((ts, width), lambda i: (i, cu + 2)),
            pl.BlockSpec((HALO, width), prev_map(cu)),
            pl.BlockSpec((HALO, width), prev_map(cu + 2)),
            pl.BlockSpec((HALO, width), next_map(cu)),
            pl.BlockSpec((HALO, width), next_map(cu + 2)),
            pl.BlockSpec((3, width), lambda i: (0, 0)),
            pl.BlockSpec((2, width), lambda i: (0, 0)),
        ],
        out_specs=pl.BlockSpec((ts, 2 * width), lambda i: (i, 0)),
        out_shape=jax.ShapeDtypeStruct((m, 2 * width), BF16),
        compiler_params=_params(("arbitrary",), est),
        name="mix_prep",
    )(attn, p, p, p, p, p, p, p, conv_w, group_g.reshape(2, width))


def _rope_tables(seq):
    rows = seq // GRID_W
    quarter = HEAD_DIM // 4
    freqs = ROPE_THETA ** (-jnp.arange(quarter, dtype=F32) / quarter)
    ang_r = jnp.arange(rows, dtype=F32)[:, None] * freqs[None, :]
    ang_c = jnp.arange(GRID_W, dtype=F32)[:, None] * freqs[None, :]
    per_row = lambda a: jnp.repeat(a, GRID_W, axis=0)
    per_col = lambda a: jnp.tile(a, (rows, 1))
    cos_r, sin_r = per_row(jnp.cos(ang_r)), per_row(jnp.sin(ang_r))
    cos_c, sin_c = per_col(jnp.cos(ang_c)), per_col(jnp.sin(ang_c))
    cos = jnp.concatenate([cos_r, cos_r, cos_c, cos_c], axis=-1)
    sin = jnp.concatenate([-sin_r, sin_r, -sin_c, sin_c], axis=-1)
    return cos, sin


def _layer(x, ctx, cond, ada_w, ada_b, norm1_g, w_in, conv_w, rpb, group_g, w_out, norm2_g,
           w_gate_up, w_down, cos, sin):
    b, s, d = x.shape
    ctx_len = ctx.shape[1]
    heads = rpb.shape[0]
    na_width = heads * HEAD_DIM
    conv_width = conv_w.shape[1]
    hidden = w_down.shape[0]
    m = b * s
    n_early = 2
    early, tabs = _adaln_and_bias(cond, ada_w, ada_b, n_early * d, rpb, s // GRID_W)
    sh1, sc1 = [early[:b, None, i * d:(i + 1) * d] for i in range(n_early)]
    csh1, csc1 = [early[b:b + 1, None, i * d:(i + 1) * d] for i in range(n_early)]

    h = _norm_mod(x, norm1_g, sh1, sc1).reshape(m, d)
    hc = _norm_mod(ctx, norm1_g, csh1, csc1).reshape(b * ctx_len, d)
    qk = _inproj_qk(h, w_in, cos, sin, na_width)
    rest = _inproj_rest(h, w_in, 2 * na_width)
    kvc = _ctx_kv(hc, w_in, na_width, 2 * na_width)
    attn, late = _attention(qk, rest, kvc, tabs, cond, ada_w, ada_b, n_early * d, b, s, heads,
                            ctx_len)
    g1, sh2, sc2, g2 = [late[:b, None, i * d:(i + 1) * d] for i in range(N_MOD - n_early)]
    y = _mix_prep(attn, rest, conv_w, group_g, s, conv_width, na_width)
    x1 = _outproj(y, w_out, x.reshape(m, d), g1, s, tm=512, tn=1024, name="out_proj")
    h2 = _norm_mod(x1.reshape(b, s, d), norm2_g, sh2, sc2).reshape(m, d)
    hid = _ffn_up(h2, w_gate_up, hidden)
    return _outproj(hid, w_down, x1, g2, s, tm=512, tn=512, name="ffn_down")


def kernel(x, c, ctx, c_ctx, ada_w, ada_b, norm1_g, w_in, conv_w, rpb, group_norm_g, w_out,
           norm2_g, w_gate_up, w_down, final_norm_g):
    b, s, d = x.shape
    depth = ada_w.shape[0]
    assert depth == 1, "context-stream update between layers is not implemented"
    assert conv_w.shape[2] == rpb.shape[1] * HEAD_DIM, "equal-width branches expected"
    cos, sin = _rope_tables(s)
    pad = (-(b + 1)) % 8
    cond = jnp.concatenate([c, c_ctx[None, :], jnp.zeros((pad, d), F32)], axis=0)
    x2 = _layer(x, ctx, cond, ada_w[0], ada_b[0], norm1_g[0], w_in[0], conv_w[0], rpb[0],
                group_norm_g[0], w_out[0], norm2_g[0], w_gate_up[0], w_down[0], cos, sin)
    return _final_norm(x2, final_norm_g).reshape(b, s, d)
```

```python
import functools

import jax
import jax.numpy as jnp
import numpy as np
from jax import lax
from jax.experimental import pallas as pl
from jax.experimental.pallas import tpu as pltpu

F32 = jnp.float32
BF16 = jnp.bfloat16

GRID_W = 64
HEAD_DIM = 128
NA_KH = 8
NA_KW = 16
ROPE_THETA = 10000.0
EPS = 1e-6
LOG2E = 1.4426950408889634
N_MOD = 6
RPB_ROWS = 2 * NA_KH - 1
RPB_COLS = 2 * NA_KW - 1

V7X_VMEM_BYTES = 64 * 1024 * 1024
V7X_LANES = 128
V7X_BF16_SUBLANES = 16
VMEM_REQUEST_CAP = 60000 * 1024

ATT_QROWS = 4
ATT_KROWS = ATT_QROWS + NA_KH
ATT_QB = ATT_QROWS * GRID_W
ATT_KB = ATT_KROWS * GRID_W


def _params(semantics, vmem_bytes, keep_operands_in_hbm=False):
    request = VMEM_REQUEST_CAP if keep_operands_in_hbm else min(vmem_bytes, VMEM_REQUEST_CAP)
    return pltpu.CompilerParams(dimension_semantics=semantics, vmem_limit_bytes=int(request))


def _nbytes(shape, dtype):
    return int(np.prod(shape)) * jnp.dtype(dtype).itemsize


def _vmem_estimate(blocks, scratch=(), temps=()):
    total = sum(2 * _nbytes(s, d) for s, d in blocks)
    total += sum(_nbytes(s, d) for s, d in scratch)
    total += sum(_nbytes(s, d) for s, d in temps)
    return total + 2 * 1024 * 1024


def _silu(x):
    return x * (1.0 / (1.0 + jnp.exp(-x)))


def _adaln_body(c_ref, w_ref, b_ref, o_ref):
    s = _silu(c_ref[...]).astype(BF16)
    w = w_ref[...].astype(BF16)
    o_ref[...] = jnp.dot(s, w, preferred_element_type=F32) + b_ref[...]


def _rms(x, g):
    return x * lax.rsqrt(jnp.mean(x * x, axis=-1, keepdims=True) + EPS) * g


def _lane_blocks(width):
    return [slice(c * V7X_LANES, (c + 1) * V7X_LANES) for c in range(width // V7X_LANES)]


NORM_ROWS = 128


def _rms_rows(load, store, n_rows, width):
    for r in range(n_rows // NORM_ROWS):
        rows = slice(r * NORM_ROWS, (r + 1) * NORM_ROWS)
        part = None
        for cols in _lane_blocks(width):
            xc = load(rows, cols)
            part = xc * xc if part is None else part + xc * xc
        mean = jnp.sum(part, axis=-1, keepdims=True) / width
        inv = jnp.broadcast_to(lax.rsqrt(mean + EPS), (NORM_ROWS, V7X_LANES))
        for cols in _lane_blocks(width):
            store(rows, cols, load(rows, cols) * inv)


def _norm_mod_body(x_ref, g_ref, sh_ref, sc_ref, o_ref):
    gain = g_ref[...] * (1.0 + sc_ref[0])
    shift = sh_ref[0]

    def store(rows, cols, xn):
        o_ref[0, rows, cols] = (xn * gain[:, cols] + shift[:, cols]).astype(o_ref.dtype)

    _rms_rows(lambda rows, cols: x_ref[0, rows, cols], store, x_ref.shape[1], x_ref.shape[2])


def _norm_mod(x3, g, shift, scale, ts=1024):
    b, s, d = x3.shape
    ts = min(ts, s)
    per_batch = shift.shape[0] == b
    mod_map = (lambda bi, si: (bi, 0, 0)) if per_batch else (lambda bi, si: (0, 0, 0))
    est = _vmem_estimate(
        [((ts, d), F32), ((ts, d), BF16)], temps=[((NORM_ROWS, d), F32)] * 4)
    return pl.pallas_call(
        _norm_mod_body,
        grid=(b, s // ts),
        in_specs=[
            pl.BlockSpec((1, ts, d), lambda bi, si: (bi, si, 0)),
            pl.BlockSpec((1, d), lambda bi, si: (0, 0)),
            pl.BlockSpec((1, 1, d), mod_map),
            pl.BlockSpec((1, 1, d), mod_map),
        ],
        out_specs=pl.BlockSpec((1, ts, d), lambda bi, si: (bi, si, 0)),
        out_shape=jax.ShapeDtypeStruct((b, s, d), BF16),
        compiler_params=_params(("arbitrary", "arbitrary"), est, keep_operands_in_hbm=True),
        name="norm_mod",
    )(x3, g.reshape(1, d), shift, scale)


def _final_body(x_ref, g_ref, o_ref):
    gain = g_ref[...]

    def store(rows, cols, xn):
        o_ref[rows, cols] = xn * gain[:, cols]

    _rms_rows(lambda rows, cols: x_ref[rows, cols], store, x_ref.shape[0], x_ref.shape[1])


def _final_norm(x2, g, ts=512):
    m, d = x2.shape
    est = _vmem_estimate([((ts, d), F32)] * 2, temps=[((ts, d), F32)] * 3)
    return pl.pallas_call(
        _final_body,
        grid=(m // ts,),
        in_specs=[
            pl.BlockSpec((ts, d), lambda i: (i, 0)),
            pl.BlockSpec((1, d), lambda i: (0, 0)),
        ],
        out_specs=pl.BlockSpec((ts, d), lambda i: (i, 0)),
        out_shape=jax.ShapeDtypeStruct((m, d), F32),
        compiler_params=_params(("arbitrary",), est),
        name="final_norm",
    )(x2, g.reshape(1, d))


def _ws_body(*refs, n_extra, nj, ni, kc, segs, epilogue, row_splits):
    a_ref, w_hbm = refs[0], refs[1]
    extra = refs[2:2 + n_extra]
    o_ref = refs[2 + n_extra]
    wbf_even, wbf_odd, stage, sem = refs[3 + n_extra:]
    j = pl.program_id(0)
    i = pl.program_id(1)
    t = j * ni + i

    def aligned(v, multiple):
        return v if isinstance(v, int) else pl.multiple_of(v, multiple)

    def chunk_copies(tile, chunk, slot):
        row0 = aligned(chunk * kc, kc)
        return [
            pltpu.make_async_copy(
                w_hbm.at[pl.ds(row0, kc), pl.ds(aligned(col_fn(tile), V7X_LANES), width)],
                stage.at[slot, :, pl.ds(dst, width)],
                sem.at[slot])
            for col_fn, width, dst in segs
        ]

    def start(tile, chunk, slot):
        for cp in chunk_copies(tile, chunk, slot):
            cp.start()

    def wait(tile, chunk, slot):
        for cp in chunk_copies(tile, chunk, slot):
            cp.wait()

    def round_chunk(chunk, slot, dst_ref):
        row0 = aligned(chunk * kc, kc)
        dst_ref[pl.ds(row0, kc), :] = stage[slot].astype(BF16)

    @pl.when(t == 0)
    def _():
        start(0, 0, 0)
        for c in range(ni):
            if c + 1 < ni:
                start(0, c + 1, (c + 1) % 2)
            wait(0, c, c % 2)
            round_chunk(c, c % 2, wbf_even)
        start(min(1, nj - 1), 0, ni % 2)

    nxt = jnp.minimum(j + 1, nj - 1)
    slot = (t + ni) % 2
    wait(nxt, i, slot)

    @pl.when(t + 1 < nj * ni)
    def _():
        wrap = i + 1 == ni
        tile_next = jnp.where(wrap, jnp.minimum(j + 2, nj - 1), nxt)
        chunk_next = jnp.where(wrap, 0, i + 1)
        start(tile_next, chunk_next, 1 - slot)

    def compute(cur_ref, idle_ref):
        tm = a_ref.shape[0]
        for r in range(row_splits):
            rows = slice(r * tm // row_splits, (r + 1) * tm // row_splits)
            acc = jnp.dot(a_ref[rows, :], cur_ref[...], preferred_element_type=F32)
            epilogue(acc, rows, j, i, extra, o_ref)
        round_chunk(i, slot, idle_ref)

    @pl.when(j % 2 == 0)
    def _():
        compute(wbf_even, wbf_odd)

    @pl.when(j % 2 == 1)
    def _():
        compute(wbf_odd, wbf_even)


def _ws_matmul(a, w, segs, nj, tm, out_shape, out_dtype, out_tn, epilogue, extra=(),
               extra_specs=(), temps=(), row_splits=1, name="ws_matmul"):
    m, k = a.shape
    ni = m // tm
    assert ni * tm == m and k % ni == 0
    kc = k // ni
    assert kc % V7X_BF16_SUBLANES == 0 and tm % (row_splits * V7X_BF16_SUBLANES) == 0
    tn = sum(width for _, width, _ in segs)
    body = functools.partial(_ws_body, n_extra=len(extra), nj=nj, ni=ni, kc=kc, segs=segs,
                             epilogue=epilogue, row_splits=row_splits)
    blocks = [((tm, k), BF16), ((tm, out_tn), out_dtype)]
    blocks += [(spec.block_shape, arr.dtype) for spec, arr in zip(extra_specs, extra)]
    est = _vmem_estimate(
        blocks, scratch=[((k, tn), BF16), ((k, tn), BF16), ((2, kc, tn), F32)],
        temps=[((tm, tn), F32), ((tm, tn), F32), ((kc, tn), F32)] + list(temps))
    return pl.pallas_call(
        body,
        grid=(nj, ni),
        in_specs=[pl.BlockSpec((tm, k), lambda j, i: (i, 0)),
                  pl.BlockSpec(memory_space=pl.ANY)] + list(extra_specs),
        out_specs=pl.BlockSpec((tm, out_tn), lambda j, i: (i, j)),
        out_shape=jax.ShapeDtypeStruct(out_shape, out_dtype),
        scratch_shapes=[pltpu.VMEM((k, tn), BF16), pltpu.VMEM((k, tn), BF16),
                        pltpu.VMEM((2, kc, tn), F32), pltpu.SemaphoreType.DMA((2,))],
        compiler_params=_params(("arbitrary", "arbitrary"), est),
        name=name,
    )(a, w, *extra)


def _rope_tile(acc, cos, sin, factor):
    lane = lax.broadcasted_iota(jnp.int32, (acc.shape[0], HEAD_DIM), 1)
    low = (lane & 32) == 0
    outs = []
    for c in range(acc.shape[1] // HEAD_DIM):
        xc = acc[:, c * HEAD_DIM:(c + 1) * HEAD_DIM]
        partner = jnp.where(low, pltpu.roll(xc, HEAD_DIM - 32, 1), pltpu.roll(xc, 32, 1))
        outs.append((xc * cos + partner * sin) * factor)
    return jnp.concatenate(outs, axis=1)


def _qk_epilogue(acc, rows, j, i, extra, o_ref, *, n_q_tiles, q_scale, tiles_per_seq):
    cos_ref, sin_ref = extra
    n = rows.stop - rows.start
    tm = o_ref.shape[0]
    pos = pl.ds(pl.multiple_of((i % tiles_per_seq) * tm + rows.start, n), n)
    factor = jnp.where(j < n_q_tiles, jnp.float32(q_scale), jnp.float32(1.0))
    roped = _rope_tile(acc, cos_ref[pos, :], sin_ref[pos, :], factor)
    o_ref[rows, :] = roped.astype(o_ref.dtype)


def _cast_epilogue(acc, rows, j, i, extra, o_ref):
    o_ref[rows, :] = acc.astype(o_ref.dtype)


def _residual_epilogue(acc, rows, j, i, extra, o_ref, *, tiles_per_seq):
    x_ref, g_ref = extra
    o_ref[rows, :] = x_ref[rows, :] + g_ref[i // tiles_per_seq, j] * acc


def _swiglu_epilogue(acc, rows, j, i, extra, o_ref, *, tg):
    o_ref[rows, :] = (_silu(acc[:, :tg]) * acc[:, tg:]).astype(o_ref.dtype)


def _inproj_qk(a, w, cos, sin, na_width, tm=1024, tn=1024):
    m = a.shape[0]
    seq = cos.shape[0]
    table_spec = pl.BlockSpec((seq, HEAD_DIM), lambda j, i: (0, 0))
    return _ws_matmul(
        a, w, [(lambda tile: tile * tn, tn, 0)], 2 * na_width // tn, tm,
        (m, 2 * na_width), BF16, tn,
        functools.partial(_qk_epilogue, n_q_tiles=na_width // tn,
                          q_scale=LOG2E * HEAD_DIM ** -0.5, tiles_per_seq=seq // tm),
        extra=(cos, sin), extra_specs=(table_spec, table_spec),
        temps=[((tm, tn), F32)] * 2, row_splits=4, name="in_proj_qk")


def _inproj_rest(a, w, col0, tm=1024, tn=1024):
    m = a.shape[0]
    ncols = w.shape[1] - col0
    return _ws_matmul(
        a, w, [(lambda tile: col0 + tile * tn, tn, 0)], ncols // tn, tm,
        (m, ncols), BF16, tn, _cast_epilogue, row_splits=2, name="in_proj_rest")


def _outproj(a, w, x, gate, seq, tm, tn, name):
    m = a.shape[0]
    n = w.shape[1]
    batch = gate.shape[0]
    gate_tiles = gate.reshape(batch, n // tn, 1, tn)
    return _ws_matmul(
        a, w, [(lambda tile: tile * tn, tn, 0)], n // tn, tm, (m, n), F32, tn,
        functools.partial(_residual_epilogue, tiles_per_seq=seq // tm), extra=(x, gate_tiles),
        extra_specs=(pl.BlockSpec((tm, tn), lambda j, i: (i, j)),
                     pl.BlockSpec(gate_tiles.shape, lambda j, i: (0, 0, 0, 0))),
        row_splits=2, name=name)


def _ffn_up(a, w, hidden, tm=2048, tg=256):
    m = a.shape[0]
    segs = [(lambda tile: tile * tg, tg, 0), (lambda tile: hidden + tile * tg, tg, tg)]
    return _ws_matmul(
        a, w, segs, hidden // tg, tm, (m, hidden), BF16, tg,
        functools.partial(_swiglu_epilogue, tg=tg), temps=[((tm, tg), F32)], row_splits=4,
        name="ffn_up")


def _plain_mm_body(a_ref, w_ref, o_ref, wbf_ref):
    @pl.when(pl.program_id(1) == 0)
    def _():
        wbf_ref[...] = w_ref[...].astype(BF16)

    o_ref[...] = jnp.dot(a_ref[...], wbf_ref[...], preferred_element_type=F32).astype(o_ref.dtype)


def _ctx_kv(a, w, col0, ncols, tm=1024, tn=512):
    m, k = a.shape
    off = col0 // tn
    est = _vmem_estimate(
        [((tm, k), BF16), ((k, tn), F32), ((tm, tn), BF16)],
        scratch=[((k, tn), BF16)], temps=[((tm, tn), F32)])
    return pl.pallas_call(
        _plain_mm_body,
        grid=(ncols // tn, m // tm),
        in_specs=[
            pl.BlockSpec((tm, k), lambda j, i: (i, 0)),
            pl.BlockSpec((k, tn), lambda j, i: (0, j + off)),
        ],
        out_specs=pl.BlockSpec((tm, tn), lambda j, i: (i, j)),
        out_shape=jax.ShapeDtypeStruct((m, ncols), BF16),
        scratch_shapes=[pltpu.VMEM((k, tn), BF16)],
        compiler_params=_params(("arbitrary", "arbitrary"), est, keep_operands_in_hbm=True),
        name="ctx_kv_proj",
    )(a, w)


def _block_classes(rows):
    last_ws = rows - ATT_KROWS
    first = (0, [0] * ATT_QROWS)
    interior = (NA_KH // 2, list(range(ATT_QROWS)))
    r0 = rows - ATT_QROWS
    last = (r0 - last_ws, [rows - NA_KH - last_ws] * ATT_QROWS)
    return [first, interior, last]


def _bias_body(rpb_ref, o_ref, *, classes):
    base = pl.program_id(0) * (RPB_ROWS * RPB_COLS)
    shape = (GRID_W, 2 * GRID_W)
    lane = lax.broadcasted_iota(jnp.int32, shape, 1)
    qc = lax.broadcasted_iota(jnp.int32, shape, 0)
    kc = lane & (GRID_W - 1)
    right = lane >= GRID_W
    dc_idx = kc - qc + (NA_KW - 1)
    cs = jnp.clip(qc - NA_KW // 2, 0, GRID_W - NA_KW)
    col_ok = (kc >= cs) & (kc < cs + NA_KW)
    neg = jnp.full(shape, -jnp.inf, F32)

    pair = []
    for d in range(-1, RPB_ROWS):
        dl = min(max(d, 0), RPB_ROWS - 1)
        dr = min(max(d + 1, 0), RPB_ROWS - 1)
        acc = jnp.zeros(shape, F32)
        for dc in range(RPB_COLS):
            val = jnp.where(right, rpb_ref[base + dr * RPB_COLS + dc] * LOG2E,
                            rpb_ref[base + dl * RPB_COLS + dc] * LOG2E)
            acc = jnp.where(dc_idx == dc, val, acc)
        pair.append(acc)

    for ci, (off, rs_rel) in enumerate(classes):
        for i in range(ATT_QROWS):
            for jp in range(ATT_KROWS // 2):
                jl, jr = 2 * jp, 2 * jp + 1
                ok_l = rs_rel[i] <= jl < rs_rel[i] + NA_KH
                ok_r = rs_rel[i] <= jr < rs_rel[i] + NA_KH
                d = jl - i - off + (NA_KH - 1)
                if ok_l and ok_r:
                    blk = jnp.where(col_ok, pair[d + 1], neg)
                elif ok_l:
                    blk = jnp.where(col_ok & jnp.logical_not(right), pair[d + 1], neg)
                elif ok_r:
                    blk = jnp.where(col_ok & right, pair[d + 1], neg)
                else:
                    blk = neg
                o_ref[0, ci, i * GRID_W:(i + 1) * GRID_W,
                      jp * 2 * GRID_W:(jp + 1) * 2 * GRID_W] = blk


def _adaln_bias_body(c_ref, w_ref, b_ref, rpb_ref, o_ref, tab_ref, *, classes):
    _adaln_body(c_ref, w_ref, b_ref, o_ref)
    _bias_body(rpb_ref, tab_ref, classes=classes)


def _adaln_and_bias(cond, w, b, n, rpb, rows):
    crows, d = cond.shape
    heads = rpb.shape[0]
    tn = n // heads
    assert tn * heads == n and tn % V7X_LANES == 0
    classes = _block_classes(rows)
    tab_block = (1, len(classes), ATT_QB, ATT_KB)
    est = _vmem_estimate(
        [((crows, d), F32), ((d, tn), F32), ((1, tn), F32), ((crows, tn), F32), (tab_block, F32)],
        temps=[((d, tn), BF16)] + [((GRID_W, 2 * GRID_W), F32)] * 32)
    return pl.pallas_call(
        functools.partial(_adaln_bias_body, classes=classes),
        grid=(heads,),
        in_specs=[
            pl.BlockSpec((crows, d), lambda j: (0, 0)),
            pl.BlockSpec((d, tn), lambda j: (0, j)),
            pl.BlockSpec((1, tn), lambda j: (0, j)),
            pl.BlockSpec(memory_space=pltpu.SMEM),
        ],
        out_specs=[
            pl.BlockSpec((crows, tn), lambda j: (0, j)),
            pl.BlockSpec(tab_block, lambda j: (j, 0, 0, 0)),
        ],
        out_shape=[
            jax.ShapeDtypeStruct((crows, n), F32),
            jax.ShapeDtypeStruct((heads,) + tab_block[1:], F32),
        ],
        compiler_params=_params(("arbitrary",), est, keep_operands_in_hbm=True),
        name="adaln_bias",
    )(cond, w, b.reshape(1, -1), rpb.reshape(-1))


def _attn_body(q_ref, k_ref, v_ref, kc_ref, vc_ref, tab_ref, cond_ref, adaw_hbm, adab_ref,
               o_ref, mod_ref, wstage, sem, *, rows, batch, n_steps, mod_col0, mod_tn):
    step = pl.program_id(0) * batch + pl.program_id(1)
    slot = step % 2

    def chunk_copy(s, sl):
        col = mod_col0 + s * mod_tn
        col = col if isinstance(col, int) else pl.multiple_of(col, mod_tn)
        return pltpu.make_async_copy(
            adaw_hbm.at[pl.ds(0, wstage.shape[1]), pl.ds(col, mod_tn)], wstage.at[sl], sem.at[sl])

    @pl.when(step == 0)
    def _():
        chunk_copy(0, 0).start()

    chunk_copy(step, slot).wait()

    @pl.when(step + 1 < n_steps)
    def _():
        chunk_copy(step + 1, 1 - slot).start()

    mod_ref[...] = jnp.dot(_silu(cond_ref[...]).astype(BF16), wstage[slot].astype(BF16),
                           preferred_element_type=F32) + adab_ref[...]

    nt = (((1,), (1,)), ((), ()))
    kc = kc_ref[...]
    vc = vc_ref[...]
    n_blocks = rows // ATT_QROWS

    def window(blk):
        r0 = blk * ATT_QROWS
        ws = min(max(r0 - NA_KH // 2, 0), rows - ATT_KROWS)
        return r0 * GRID_W, ws * GRID_W

    def scores(blk):
        q0, k0 = window(blk)
        cls = 0 if blk == 0 else (2 if blk == n_blocks - 1 else 1)
        q = q_ref[q0:q0 + ATT_QB, :]
        s_lat = lax.dot_general(q, k_ref[k0:k0 + ATT_KB, :], nt, preferred_element_type=F32)
        s_ctx = lax.dot_general(q, kc, nt, preferred_element_type=F32)
        return s_lat + tab_ref[0, cls], s_ctx

    def probs(s_lat, s_ctx):
        mx = jnp.maximum(jnp.max(s_lat, axis=-1, keepdims=True),
                         jnp.max(s_ctx, axis=-1, keepdims=True))
        e_lat = jnp.exp2(s_lat - mx)
        e_ctx = jnp.exp2(s_ctx - mx)
        denom = jnp.sum(e_lat, axis=-1, keepdims=True) + jnp.sum(e_ctx, axis=-1, keepdims=True)
        return e_lat.astype(BF16), e_ctx.astype(BF16), denom

    def finish(blk, e_lat, e_ctx, denom):
        q0, k0 = window(blk)
        o = (jnp.dot(e_lat, v_ref[k0:k0 + ATT_KB, :], preferred_element_type=F32)
             + jnp.dot(e_ctx, vc, preferred_element_type=F32))
        o_ref[q0:q0 + ATT_QB, :] = (o / denom).astype(o_ref.dtype)

    nxt = scores(0)
    pending = None
    for blk in range(n_blocks):
        cur = nxt
        if blk + 1 < n_blocks:
            nxt = scores(blk + 1)
        p = probs(*cur)
        if pending is not None:
            finish(blk - 1, *pending)
        pending = p
    finish(n_blocks - 1, *pending)


def _attention(qk, rest, kvc, tabs, cond, ada_w, ada_b, mod_col0, batch, seq, heads, ctx_len):
    rows = seq // GRID_W
    tab_block = (1,) + tabs.shape[1:]
    n_steps = heads * batch
    crows, d = cond.shape
    mod_cols = ada_w.shape[1] - mod_col0
    mod_tn = mod_cols // n_steps
    assert mod_tn * n_steps == mod_cols and mod_tn % V7X_LANES == 0 and mod_col0 % mod_tn == 0
    est = _vmem_estimate(
        [((seq, HEAD_DIM), BF16)] * 4 + [((ctx_len, HEAD_DIM), BF16)] * 2 + [(tab_block, F32)],
        scratch=[((2, d, mod_tn), F32)],
        temps=[((ATT_QB, ATT_KB + ctx_len), F32)] * 12 + [((d, mod_tn), F32)])
    step = lambda h, b: h * batch + b
    return pl.pallas_call(
        functools.partial(_attn_body, rows=rows, batch=batch, n_steps=n_steps,
                          mod_col0=mod_col0, mod_tn=mod_tn),
        grid=(heads, batch),
        in_specs=[
            pl.BlockSpec((seq, HEAD_DIM), lambda h, b: (b, h)),
            pl.BlockSpec((seq, HEAD_DIM), lambda h, b: (b, heads + h)),
            pl.BlockSpec((seq, HEAD_DIM), lambda h, b: (b, h)),
            pl.BlockSpec((ctx_len, HEAD_DIM), lambda h, b: (b, h)),
            pl.BlockSpec((ctx_len, HEAD_DIM), lambda h, b: (b, heads + h)),
            pl.BlockSpec(tab_block, lambda h, b: (h, 0, 0, 0)),
            pl.BlockSpec((crows, d), lambda h, b: (0, 0)),
            pl.BlockSpec(memory_space=pl.ANY),
            pl.BlockSpec((1, mod_tn), lambda h, b: (0, mod_col0 // mod_tn + step(h, b))),
        ],
        out_specs=[
            pl.BlockSpec((seq, HEAD_DIM), lambda h, b: (b, h)),
            pl.BlockSpec((crows, mod_tn), lambda h, b: (0, step(h, b))),
        ],
        out_shape=[
            jax.ShapeDtypeStruct((batch * seq, heads * HEAD_DIM), BF16),
            jax.ShapeDtypeStruct((crows, mod_cols), F32),
        ],
        scratch_shapes=[pltpu.VMEM((2, d, mod_tn), F32), pltpu.SemaphoreType.DMA((2,))],
        compiler_params=_params(("arbitrary", "arbitrary"), est),
        name="nbr_attention",
    )(qk, qk, rest, kvc, kvc, tabs, cond, ada_w, ada_b.reshape(1, -1))


HALO = V7X_BF16_SUBLANES


def _mix_body(at_ref, u_ref, b_ref, c_ref, up_ref, cp_ref, un_ref, cn_ref, cw_ref, gg_ref,
              o_ref, *, per_seq, width):
    i = pl.program_id(0)
    ts = u_ref.shape[0]
    has_prev = jnp.where(i % per_seq == 0, 0.0, 1.0).astype(F32)
    has_next = jnp.where(i % per_seq == per_seq - 1, 0.0, 1.0).astype(F32)
    g = c_ref[...].astype(F32) * u_ref[...].astype(F32)
    g_prev = (cp_ref[...].astype(F32) * up_ref[...].astype(F32))[HALO - 1:HALO, :] * has_prev
    g_next = (cn_ref[...].astype(F32) * un_ref[...].astype(F32))[0:1, :] * has_next
    row = lax.broadcasted_iota(jnp.int32, g.shape, 0)
    below = jnp.where(row == 0, g_prev, pltpu.roll(g, 1, 0))
    above = jnp.where(row == ts - 1, g_next, pltpu.roll(g, ts - 1, 0))
    conv = cw_ref[0:1, :] * below + cw_ref[1:2, :] * g + cw_ref[2:3, :] * above
    conv_o = b_ref[...].astype(F32) * conv
    o_ref[:, :width] = _rms(at_ref[...].astype(F32), gg_ref[0:1, :]).astype(o_ref.dtype)
    o_ref[:, width:] = _rms(conv_o, gg_ref[1:2, :]).astype(o_ref.dtype)


def _mix_prep(attn, p, conv_w, group_g, seq, width, col_u, ts=512):
    m = attn.shape[0]
    cu = col_u // width
    hb = ts // HALO
    n_halo = m // HALO
    est = _vmem_estimate(
        [((ts, width), BF16)] * 4 + [((HALO, width), BF16)] * 4 + [((ts, 2 * width), BF16)],
        temps=[((ts, width), F32)] * 8)
    prev_map = lambda c: (lambda i: (jnp.maximum(i * hb - 1, 0), c))
    next_map = lambda c: (lambda i: (jnp.minimum((i + 1) * hb, n_halo - 1), c))
    return pl.pallas_call(
        functools.partial(_mix_body, per_seq=seq // ts, width=width),
        grid=(m // ts,),
        in_specs=[
            pl.BlockSpec((ts, width), lambda i: (i, 0)),
            pl.BlockSpec((ts, width), lambda i: (i, cu)),
            pl.BlockSpec((ts, width), lambda i: (i, cu + 1)),
            pl.BlockSpec((ts, width), lambda i: (i, cu + 2)),
            pl.BlockSpec((HALO, width), prev_map(cu)),
            pl.BlockSpec((HALO, width), prev_map(cu + 2)),
            pl.BlockSpec((HALO, width), next_map(cu)),
            pl.BlockSpec((HALO, width), next_map(cu + 2)),
            pl.BlockSpec((3, width), lambda i: (0, 0)),
            pl.BlockSpec((2, width), lambda i: (0, 0)),
        ],
        out_specs=pl.BlockSpec((ts, 2 * width), lambda i: (i, 0)),
        out_shape=jax.ShapeDtypeStruct((m, 2 * width), BF16),
        compiler_params=_params(("arbitrary",), est),
        name="mix_prep",
    )(attn, p, p, p, p, p, p, p, conv_w, group_g.reshape(2, width))


def _rope_tables(seq):
    rows = seq // GRID_W
    quarter = HEAD_DIM // 4
    freqs = ROPE_THETA ** (-jnp.arange(quarter, dtype=F32) / quarter)
    ang_r = jnp.arange(rows, dtype=F32)[:, None] * freqs[None, :]
    ang_c = jnp.arange(GRID_W, dtype=F32)[:, None] * freqs[None, :]
    per_row = lambda a: jnp.repeat(a, GRID_W, axis=0)
    per_col = lambda a: jnp.tile(a, (rows, 1))
    cos_r, sin_r = per_row(jnp.cos(ang_r)), per_row(jnp.sin(ang_r))
    cos_c, sin_c = per_col(jnp.cos(ang_c)), per_col(jnp.sin(ang_c))
    cos = jnp.concatenate([cos_r, cos_r, cos_c, cos_c], axis=-1)
    sin = jnp.concatenate([-sin_r, sin_r, -sin_c, sin_c], axis=-1)
    return cos, sin


def _layer(x, ctx, cond, ada_w, ada_b, norm1_g, w_in, conv_w, rpb, group_g, w_out, norm2_g,
           w_gate_up, w_down, cos, sin):
    b, s, d = x.shape
    ctx_len = ctx.shape[1]
    heads = rpb.shape[0]
    na_width = heads * HEAD_DIM
    conv_width = conv_w.shape[1]
    hidden = w_down.shape[0]
    m = b * s
    n_early = 2
    early, tabs = _adaln_and_bias(cond, ada_w, ada_b, n_early * d, rpb, s // GRID_W)
    sh1, sc1 = [early[:b, None, i * d:(i + 1) * d] for i in range(n_early)]
    csh1, csc1 = [early[b:b + 1, None, i * d:(i + 1) * d] for i in range(n_early)]

    h = _norm_mod(x, norm1_g, sh1, sc1).reshape(m, d)
    hc = _norm_mod(ctx, norm1_g, csh1, csc1).reshape(b * ctx_len, d)
    qk = _inproj_qk(h, w_in, cos, sin, na_width)
    rest = _inproj_rest(h, w_in, 2 * na_width)
    kvc = _ctx_kv(hc, w_in, na_width, 2 * na_width)
    attn, late = _attention(qk, rest, kvc, tabs, cond, ada_w, ada_b, n_early * d, b, s, heads,
                            ctx_len)
    g1, sh2, sc2, g2 = [late[:b, None, i * d:(i + 1) * d] for i in range(N_MOD - n_early)]
    y = _mix_prep(attn, rest, conv_w, group_g, s, conv_width, na_width)
    x1 = _outproj(y, w_out, x.reshape(m, d), g1, s, tm=512, tn=1024, name="out_proj")
    h2 = _norm_mod(x1.reshape(b, s, d), norm2_g, sh2, sc2).reshape(m, d)
    hid = _ffn_up(h2, w_gate_up, hidden)
    return _outproj(hid, w_down, x1, g2, s, tm=512, tn=512, name="ffn_down")


def kernel(x, c, ctx, c_ctx, ada_w, ada_b, norm1_g, w_in, conv_w, rpb, group_norm_g, w_out,
           norm2_g, w_gate_up, w_down, final_norm_g):
    b, s, d = x.shape
    depth = ada_w.shape[0]
    assert depth == 1, "context-stream update between layers is not implemented"
    assert conv_w.shape[2] == rpb.shape[1] * HEAD_DIM, "equal-width branches expected"
    cos, sin = _rope_tables(s)
    pad = (-(b + 1)) % 8
    cond = jnp.concatenate([c, c_ctx[None, :], jnp.zeros((pad, d), F32)], axis=0)
    x2 = _layer(x, ctx, cond, ada_w[0], ada_b[0], norm1_g[0], w_in[0], conv_w[0], rpb[0],
                group_norm_g[0], w_out[0], norm2_g[0], w_gate_up[0], w_down[0], cos, sin)
    return _final_norm(x2, final_norm_g).reshape(b, s, d)
```

```python
import functools

import jax
import jax.numpy as jnp
import numpy as np
from jax import lax
from jax.experimental import pallas as pl
from jax.experimental.pallas import tpu as pltpu

F32 = jnp.float32
BF16 = jnp.bfloat16

GRID_W = 64
HEAD_DIM = 128
NA_KH = 8
NA_KW = 16
ROPE_THETA = 10000.0
EPS = 1e-6
LOG2E = 1.4426950408889634
N_MOD = 6
RPB_ROWS = 2 * NA_KH - 1
RPB_COLS = 2 * NA_KW - 1

V7X_VMEM_BYTES = 64 * 1024 * 1024
V7X_LANES = 128
V7X_BF16_SUBLANES = 16
VMEM_REQUEST_CAP = 60000 * 1024

ATT_QROWS = 4
ATT_KROWS = ATT_QROWS + NA_KH
ATT_QB = ATT_QROWS * GRID_W
ATT_KB = ATT_KROWS * GRID_W
ATT_HEADS_PER_STEP = 2


def _params(semantics, vmem_bytes, keep_operands_in_hbm=False):
    request = VMEM_REQUEST_CAP if keep_operands_in_hbm else min(vmem_bytes, VMEM_REQUEST_CAP)
    return pltpu.CompilerParams(dimension_semantics=semantics, vmem_limit_bytes=int(request))


def _nbytes(shape, dtype):
    return int(np.prod(shape)) * jnp.dtype(dtype).itemsize


def _vmem_estimate(blocks, scratch=(), temps=()):
    total = sum(2 * _nbytes(s, d) for s, d in blocks)
    total += sum(_nbytes(s, d) for s, d in scratch)
    total += sum(_nbytes(s, d) for s, d in temps)
    return total + 2 * 1024 * 1024


def _silu(x):
    return x * (1.0 / (1.0 + jnp.exp(-x)))


def _adaln_body(c_ref, w_ref, b_ref, o_ref):
    s = _silu(c_ref[...]).astype(BF16)
    w = w_ref[...].astype(BF16)
    o_ref[...] = jnp.dot(s, w, preferred_element_type=F32) + b_ref[...]


def _rms(x, g):
    return x * lax.rsqrt(jnp.mean(x * x, axis=-1, keepdims=True) + EPS) * g


def _lane_blocks(width):
    return [slice(c * V7X_LANES, (c + 1) * V7X_LANES) for c in range(width // V7X_LANES)]


NORM_ROWS = 128


def _rms_rows(load, store, n_rows, width):
    for r in range(n_rows // NORM_ROWS):
        rows = slice(r * NORM_ROWS, (r + 1) * NORM_ROWS)
        part = None
        for cols in _lane_blocks(width):
            xc = load(rows, cols)
            part = xc * xc if part is None else part + xc * xc
        mean = jnp.sum(part, axis=-1, keepdims=True) / width
        inv = jnp.broadcast_to(lax.rsqrt(mean + EPS), (NORM_ROWS, V7X_LANES))
        for cols in _lane_blocks(width):
            store(rows, cols, load(rows, cols) * inv)


def _norm_mod_body(x_ref, g_ref, sh_ref, sc_ref, o_ref):
    gain = g_ref[...] * (1.0 + sc_ref[0])
    shift = sh_ref[0]

    def store(rows, cols, xn):
        o_ref[0, rows, cols] = (xn * gain[:, cols] + shift[:, cols]).astype(o_ref.dtype)

    _rms_rows(lambda rows, cols: x_ref[0, rows, cols], store, x_ref.shape[1], x_ref.shape[2])


def _norm_mod(x3, g, shift, scale, ts=1024):
    b, s, d = x3.shape
    ts = min(ts, s)
    per_batch = shift.shape[0] == b
    mod_map = (lambda bi, si: (bi, 0, 0)) if per_batch else (lambda bi, si: (0, 0, 0))
    est = _vmem_estimate(
        [((ts, d), F32), ((ts, d), BF16)], temps=[((NORM_ROWS, d), F32)] * 4)
    return pl.pallas_call(
        _norm_mod_body,
        grid=(b, s // ts),
        in_specs=[
            pl.BlockSpec((1, ts, d), lambda bi, si: (bi, si, 0)),
            pl.BlockSpec((1, d), lambda bi, si: (0, 0)),
            pl.BlockSpec((1, 1, d), mod_map),
            pl.BlockSpec((1, 1, d), mod_map),
        ],
        out_specs=pl.BlockSpec((1, ts, d), lambda bi, si: (bi, si, 0)),
        out_shape=jax.ShapeDtypeStruct((b, s, d), BF16),
        compiler_params=_params(("arbitrary", "arbitrary"), est, keep_operands_in_hbm=True),
        name="norm_mod",
    )(x3, g.reshape(1, d), shift, scale)


def _final_body(x_ref, g_ref, o_ref):
    gain = g_ref[...]

    def store(rows, cols, xn):
        o_ref[rows, cols] = xn * gain[:, cols]

    _rms_rows(lambda rows, cols: x_ref[rows, cols], store, x_ref.shape[0], x_ref.shape[1])


def _final_norm(x2, g, ts=512):
    m, d = x2.shape
    est = _vmem_estimate([((ts, d), F32)] * 2, temps=[((ts, d), F32)] * 3)
    return pl.pallas_call(
        _final_body,
        grid=(m // ts,),
        in_specs=[
            pl.BlockSpec((ts, d), lambda i: (i, 0)),
            pl.BlockSpec((1, d), lambda i: (0, 0)),
        ],
        out_specs=pl.BlockSpec((ts, d), lambda i: (i, 0)),
        out_shape=jax.ShapeDtypeStruct((m, d), F32),
        compiler_params=_params(("arbitrary",), est),
        name="final_norm",
    )(x2, g.reshape(1, d))


def _ws_body(*refs, n_extra, nj, ni, kc, segs, epilogue, row_splits):
    a_ref, w_hbm = refs[0], refs[1]
    extra = refs[2:2 + n_extra]
    o_ref = refs[2 + n_extra]
    wbf_even, wbf_odd, stage, sem = refs[3 + n_extra:]
    j = pl.program_id(0)
    i = pl.program_id(1)
    t = j * ni + i

    def aligned(v, multiple):
        return v if isinstance(v, int) else pl.multiple_of(v, multiple)

    def chunk_copies(tile, chunk, slot):
        row0 = aligned(chunk * kc, kc)
        return [
            pltpu.make_async_copy(
                w_hbm.at[pl.ds(row0, kc), pl.ds(aligned(col_fn(tile), V7X_LANES), width)],
                stage.at[slot, :, pl.ds(dst, width)],
                sem.at[slot])
            for col_fn, width, dst in segs
        ]

    def start(tile, chunk, slot):
        for cp in chunk_copies(tile, chunk, slot):
            cp.start()

    def wait(tile, chunk, slot):
        for cp in chunk_copies(tile, chunk, slot):
            cp.wait()

    def round_chunk(chunk, slot, dst_ref):
        row0 = aligned(chunk * kc, kc)
        dst_ref[pl.ds(row0, kc), :] = stage[slot].astype(BF16)

    @pl.when(t == 0)
    def _():
        start(0, 0, 0)
        for c in range(ni):
            if c + 1 < ni:
                start(0, c + 1, (c + 1) % 2)
            wait(0, c, c % 2)
            round_chunk(c, c % 2, wbf_even)
        start(min(1, nj - 1), 0, ni % 2)

    nxt = jnp.minimum(j + 1, nj - 1)
    slot = (t + ni) % 2
    wait(nxt, i, slot)

    @pl.when(t + 1 < nj * ni)
    def _():
        wrap = i + 1 == ni
        tile_next = jnp.where(wrap, jnp.minimum(j + 2, nj - 1), nxt)
        chunk_next = jnp.where(wrap, 0, i + 1)
        start(tile_next, chunk_next, 1 - slot)

    def compute(cur_ref, idle_ref):
        tm = a_ref.shape[0]
        for r in range(row_splits):
            rows = slice(r * tm // row_splits, (r + 1) * tm // row_splits)
            acc = jnp.dot(a_ref[rows, :], cur_ref[...], preferred_element_type=F32)
            epilogue(acc, rows, j, i, extra, o_ref)
        round_chunk(i, slot, idle_ref)

    @pl.when(j % 2 == 0)
    def _():
        compute(wbf_even, wbf_odd)

    @pl.when(j % 2 == 1)
    def _():
        compute(wbf_odd, wbf_even)


def _ws_matmul(a, w, segs, nj, tm, out_shape, out_dtype, out_tn, epilogue, extra=(),
               extra_specs=(), temps=(), row_splits=1, name="ws_matmul"):
    m, k = a.shape
    ni = m // tm
    assert ni * tm == m and k % ni == 0
    kc = k // ni
    assert kc % V7X_BF16_SUBLANES == 0 and tm % (row_splits * V7X_BF16_SUBLANES) == 0
    tn = sum(width for _, width, _ in segs)
    body = functools.partial(_ws_body, n_extra=len(extra), nj=nj, ni=ni, kc=kc, segs=segs,
                             epilogue=epilogue, row_splits=row_splits)
    blocks = [((tm, k), BF16), ((tm, out_tn), out_dtype)]
    blocks += [(spec.block_shape, arr.dtype) for spec, arr in zip(extra_specs, extra)]
    est = _vmem_estimate(
        blocks, scratch=[((k, tn), BF16), ((k, tn), BF16), ((2, kc, tn), F32)],
        temps=[((tm, tn), F32), ((tm, tn), F32), ((kc, tn), F32)] + list(temps))
    return pl.pallas_call(
        body,
        grid=(nj, ni),
        in_specs=[pl.BlockSpec((tm, k), lambda j, i: (i, 0)),
                  pl.BlockSpec(memory_space=pl.ANY)] + list(extra_specs),
        out_specs=pl.BlockSpec((tm, out_tn), lambda j, i: (i, j)),
        out_shape=jax.ShapeDtypeStruct(out_shape, out_dtype),
        scratch_shapes=[pltpu.VMEM((k, tn), BF16), pltpu.VMEM((k, tn), BF16),
                        pltpu.VMEM((2, kc, tn), F32), pltpu.SemaphoreType.DMA((2,))],
        compiler_params=_params(("arbitrary", "arbitrary"), est),
        name=name,
    )(a, w, *extra)


def _rope_tile(acc, cos, sin, factor):
    lane = lax.broadcasted_iota(jnp.int32, (acc.shape[0], HEAD_DIM), 1)
    low = (lane & 32) == 0
    outs = []
    for c in range(acc.shape[1] // HEAD_DIM):
        xc = acc[:, c * HEAD_DIM:(c + 1) * HEAD_DIM]
        partner = jnp.where(low, pltpu.roll(xc, HEAD_DIM - 32, 1), pltpu.roll(xc, 32, 1))
        outs.append((xc * cos + partner * sin) * factor)
    return jnp.concatenate(outs, axis=1)


def _qk_epilogue(acc, rows, j, i, extra, o_ref, *, n_q_tiles, q_scale, tiles_per_seq):
    cos_ref, sin_ref = extra
    n = rows.stop - rows.start
    tm = o_ref.shape[0]
    pos = pl.ds(pl.multiple_of((i % tiles_per_seq) * tm + rows.start, n), n)
    factor = jnp.where(j < n_q_tiles, jnp.float32(q_scale), jnp.float32(1.0))
    roped = _rope_tile(acc, cos_ref[pos, :], sin_ref[pos, :], factor)
    o_ref[rows, :] = roped.astype(o_ref.dtype)


def _cast_epilogue(acc, rows, j, i, extra, o_ref):
    o_ref[rows, :] = acc.astype(o_ref.dtype)


def _residual_epilogue(acc, rows, j, i, extra, o_ref, *, tiles_per_seq):
    x_ref, g_ref = extra
    o_ref[rows, :] = x_ref[rows, :] + g_ref[i // tiles_per_seq, j] * acc


def _swiglu_epilogue(acc, rows, j, i, extra, o_ref, *, tg):
    o_ref[rows, :] = (_silu(acc[:, :tg]) * acc[:, tg:]).astype(o_ref.dtype)


def _inproj_qk(a, w, cos, sin, na_width, tm=1024, tn=1024):
    m = a.shape[0]
    seq = cos.shape[0]
    table_spec = pl.BlockSpec((seq, HEAD_DIM), lambda j, i: (0, 0))
    return _ws_matmul(
        a, w, [(lambda tile: tile * tn, tn, 0)], 2 * na_width // tn, tm,
        (m, 2 * na_width), BF16, tn,
        functools.partial(_qk_epilogue, n_q_tiles=na_width // tn,
                          q_scale=LOG2E * HEAD_DIM ** -0.5, tiles_per_seq=seq // tm),
        extra=(cos, sin), extra_specs=(table_spec, table_spec),
        temps=[((tm, tn), F32)] * 2, row_splits=4, name="in_proj_qk")


def _inproj_rest(a, w, col0, tm=1024, tn=1024):
    m = a.shape[0]
    ncols = w.shape[1] - col0
    return _ws_matmul(
        a, w, [(lambda tile: col0 + tile * tn, tn, 0)], ncols // tn, tm,
        (m, ncols), BF16, tn, _cast_epilogue, row_splits=2, name="in_proj_rest")


def _outproj(a, w, x, gate, seq, tm, tn, name):
    m = a.shape[0]
    n = w.shape[1]
    batch = gate.shape[0]
    gate_tiles = gate.reshape(batch, n // tn, 1, tn)
    return _ws_matmul(
        a, w, [(lambda tile: tile * tn, tn, 0)], n // tn, tm, (m, n), F32, tn,
        functools.partial(_residual_epilogue, tiles_per_seq=seq // tm), extra=(x, gate_tiles),
        extra_specs=(pl.BlockSpec((tm, tn), lambda j, i: (i, j)),
                     pl.BlockSpec(gate_tiles.shape, lambda j, i: (0, 0, 0, 0))),
        row_splits=2, name=name)


def _ffn_up(a, w, hidden, tm=2048, tg=256):
    m = a.shape[0]
    segs = [(lambda tile: tile * tg, tg, 0), (lambda tile: hidden + tile * tg, tg, tg)]
    return _ws_matmul(
        a, w, segs, hidden // tg, tm, (m, hidden), BF16, tg,
        functools.partial(_swiglu_epilogue, tg=tg), temps=[((tm, tg), F32)], row_splits=4,
        name="ffn_up")


def _plain_mm_body(a_ref, w_ref, o_ref, wbf_ref):
    @pl.when(pl.program_id(1) == 0)
    def _():
        wbf_ref[...] = w_ref[...].astype(BF16)

    o_ref[...] = jnp.dot(a_ref[...], wbf_ref[...], preferred_element_type=F32).astype(o_ref.dtype)


def _ctx_kv(a, w, col0, ncols, tm=1024, tn=512):
    m, k = a.shape
    off = col0 // tn
    est = _vmem_estimate(
        [((tm, k), BF16), ((k, tn), F32), ((tm, tn), BF16)],
        scratch=[((k, tn), BF16)], temps=[((tm, tn), F32)])
    return pl.pallas_call(
        _plain_mm_body,
        grid=(ncols // tn, m // tm),
        in_specs=[
            pl.BlockSpec((tm, k), lambda j, i: (i, 0)),
            pl.BlockSpec((k, tn), lambda j, i: (0, j + off)),
        ],
        out_specs=pl.BlockSpec((tm, tn), lambda j, i: (i, j)),
        out_shape=jax.ShapeDtypeStruct((m, ncols), BF16),
        scratch_shapes=[pltpu.VMEM((k, tn), BF16)],
        compiler_params=_params(("arbitrary", "arbitrary"), est, keep_operands_in_hbm=True),
        name="ctx_kv_proj",
    )(a, w)


def _block_classes(rows):
    last_ws = rows - ATT_KROWS
    first = (0, [0] * ATT_QROWS)
    interior = (NA_KH // 2, list(range(ATT_QROWS)))
    r0 = rows - ATT_QROWS
    last = (r0 - last_ws, [rows - NA_KH - last_ws] * ATT_QROWS)
    return [first, interior, last]


def _bias_body(rpb_ref, o_ref, *, classes):
    base = pl.program_id(0) * (RPB_ROWS * RPB_COLS)
    shape = (GRID_W, 2 * GRID_W)
    lane = lax.broadcasted_iota(jnp.int32, shape, 1)
    qc = lax.broadcasted_iota(jnp.int32, shape, 0)
    kc = lane & (GRID_W - 1)
    right = lane >= GRID_W
    dc_idx = kc - qc + (NA_KW - 1)
    cs = jnp.clip(qc - NA_KW // 2, 0, GRID_W - NA_KW)
    col_ok = (kc >= cs) & (kc < cs + NA_KW)
    neg = jnp.full(shape, -jnp.inf, F32)

    pair = []
    for d in range(-1, RPB_ROWS):
        dl = min(max(d, 0), RPB_ROWS - 1)
        dr = min(max(d + 1, 0), RPB_ROWS - 1)
        acc = jnp.zeros(shape, F32)
        for dc in range(RPB_COLS):
            val = jnp.where(right, rpb_ref[base + dr * RPB_COLS + dc] * LOG2E,
                            rpb_ref[base + dl * RPB_COLS + dc] * LOG2E)
            acc = jnp.where(dc_idx == dc, val, acc)
        pair.append(acc)

    for ci, (off, rs_rel) in enumerate(classes):
        for i in range(ATT_QROWS):
            for jp in range(ATT_KROWS // 2):
                jl, jr = 2 * jp, 2 * jp + 1
                ok_l = rs_rel[i] <= jl < rs_rel[i] + NA_KH
                ok_r = rs_rel[i] <= jr < rs_rel[i] + NA_KH
                d = jl - i - off + (NA_KH - 1)
                if ok_l and ok_r:
                    blk = jnp.where(col_ok, pair[d + 1], neg)
                elif ok_l:
                    blk = jnp.where(col_ok & jnp.logical_not(right), pair[d + 1], neg)
                elif ok_r:
                    blk = jnp.where(col_ok & right, pair[d + 1], neg)
                else:
                    blk = neg
                o_ref[0, ci, i * GRID_W:(i + 1) * GRID_W,
                      jp * 2 * GRID_W:(jp + 1) * 2 * GRID_W] = blk


def _adaln_bias_body(c_ref, w_ref, b_ref, rpb_ref, o_ref, tab_ref, *, classes):
    _adaln_body(c_ref, w_ref, b_ref, o_ref)
    _bias_body(rpb_ref, tab_ref, classes=classes)


def _adaln_and_bias(cond, w, b, n, rpb, rows):
    crows, d = cond.shape
    heads = rpb.shape[0]
    tn = n // heads
    assert tn * heads == n and tn % V7X_LANES == 0
    classes = _block_classes(rows)
    tab_block = (1, len(classes), ATT_QB, ATT_KB)
    est = _vmem_estimate(
        [((crows, d), F32), ((d, tn), F32), ((1, tn), F32), ((crows, tn), F32), (tab_block, F32)],
        temps=[((d, tn), BF16)] + [((GRID_W, 2 * GRID_W), F32)] * 32)
    return pl.pallas_call(
        functools.partial(_adaln_bias_body, classes=classes),
        grid=(heads,),
        in_specs=[
            pl.BlockSpec((crows, d), lambda j: (0, 0)),
            pl.BlockSpec((d, tn), lambda j: (0, j)),
            pl.BlockSpec((1, tn), lambda j: (0, j)),
            pl.BlockSpec(memory_space=pltpu.SMEM),
        ],
        out_specs=[
            pl.BlockSpec((crows, tn), lambda j: (0, j)),
            pl.BlockSpec(tab_block, lambda j: (j, 0, 0, 0)),
        ],
        out_shape=[
            jax.ShapeDtypeStruct((crows, n), F32),
            jax.ShapeDtypeStruct((heads,) + tab_block[1:], F32),
        ],
        compiler_params=_params(("arbitrary",), est, keep_operands_in_hbm=True),
        name="adaln_bias",
    )(cond, w, b.reshape(1, -1), rpb.reshape(-1))


def _attn_body(q_ref, k_ref, v_ref, kc_ref, vc_ref, tab_ref, cond_ref, adaw_hbm, adab_ref,
               o_ref, mod_ref, wstage, sem, *, rows, batch, n_steps, mod_col0, mod_tn):
    step = pl.program_id(0) * batch + pl.program_id(1)
    slot = step % 2

    def chunk_copy(s, sl):
        col = mod_col0 + s * mod_tn
        col = col if isinstance(col, int) else pl.multiple_of(col, mod_tn)
        return pltpu.make_async_copy(
            adaw_hbm.at[pl.ds(0, wstage.shape[1]), pl.ds(col, mod_tn)], wstage.at[sl], sem.at[sl])

    @pl.when(step == 0)
    def _():
        chunk_copy(0, 0).start()

    chunk_copy(step, slot).wait()

    @pl.when(step + 1 < n_steps)
    def _():
        chunk_copy(step + 1, 1 - slot).start()

    mod_ref[...] = jnp.dot(_silu(cond_ref[...]).astype(BF16), wstage[slot].astype(BF16),
                           preferred_element_type=F32) + adab_ref[...]

    nt = (((1,), (1,)), ((), ()))
    n_blocks = rows // ATT_QROWS
    n_heads = q_ref.shape[1] // HEAD_DIM

    def window(item):
        head, blk = divmod(item, n_blocks)
        r0 = blk * ATT_QROWS
        ws = min(max(r0 - NA_KH // 2, 0), rows - ATT_KROWS)
        return r0 * GRID_W, ws * GRID_W, slice(head * HEAD_DIM, (head + 1) * HEAD_DIM)

    def scores(item):
        head, blk = divmod(item, n_blocks)
        q0, k0, lanes = window(item)
        cls = 0 if blk == 0 else (2 if blk == n_blocks - 1 else 1)
        q = q_ref[q0:q0 + ATT_QB, lanes]
        s_lat = lax.dot_general(q, k_ref[k0:k0 + ATT_KB, lanes], nt, preferred_element_type=F32)
        s_ctx = lax.dot_general(q, kc_ref[:, lanes], nt, preferred_element_type=F32)
        return s_lat + tab_ref[head, cls], s_ctx

    def probs(s_lat, s_ctx):
        mx = jnp.maximum(jnp.max(s_lat, axis=-1, keepdims=True),
                         jnp.max(s_ctx, axis=-1, keepdims=True))
        e_lat = jnp.exp2(s_lat - mx)
        e_ctx = jnp.exp2(s_ctx - mx)
        denom = jnp.sum(e_lat, axis=-1, keepdims=True) + jnp.sum(e_ctx, axis=-1, keepdims=True)
        return e_lat.astype(BF16), e_ctx.astype(BF16), denom

    def finish(item, e_lat, e_ctx, denom):
        q0, k0, lanes = window(item)
        o = (jnp.dot(e_lat, v_ref[k0:k0 + ATT_KB, lanes], preferred_element_type=F32)
             + jnp.dot(e_ctx, vc_ref[:, lanes], preferred_element_type=F32))
        o_ref[q0:q0 + ATT_QB, lanes] = (o / denom).astype(o_ref.dtype)

    n_items = n_heads * n_blocks
    nxt = scores(0)
    pending = None
    for item in range(n_items):
        cur = nxt
        if item + 1 < n_items:
            nxt = scores(item + 1)
        p = probs(*cur)
        if pending is not None:
            finish(item - 1, *pending)
        pending = p
    finish(n_items - 1, *pending)


def _attention(qk, rest, kvc, tabs, cond, ada_w, ada_b, mod_col0, batch, seq, heads, ctx_len):
    rows = seq // GRID_W
    hp = ATT_HEADS_PER_STEP
    groups = heads // hp
    assert groups * hp == heads
    width = hp * HEAD_DIM
    tab_block = (hp,) + tabs.shape[1:]
    n_steps = groups * batch
    crows, d = cond.shape
    mod_cols = ada_w.shape[1] - mod_col0
    mod_tn = mod_cols // n_steps
    assert mod_tn * n_steps == mod_cols and mod_tn % V7X_LANES == 0 and mod_col0 % mod_tn == 0
    est = _vmem_estimate(
        [((seq, width), BF16)] * 4 + [((ctx_len, width), BF16)] * 2 + [(tab_block, F32)],
        scratch=[((2, d, mod_tn), F32)],
        temps=[((ATT_QB, ATT_KB + ctx_len), F32)] * 12 + [((d, mod_tn), F32)])
    step = lambda h, b: h * batch + b
    return pl.pallas_call(
        functools.partial(_attn_body, rows=rows, batch=batch, n_steps=n_steps,
                          mod_col0=mod_col0, mod_tn=mod_tn),
        grid=(groups, batch),
        in_specs=[
            pl.BlockSpec((seq, width), lambda h, b: (b, h)),
            pl.BlockSpec((seq, width), lambda h, b: (b, groups + h)),
            pl.BlockSpec((seq, width), lambda h, b: (b, h)),
            pl.BlockSpec((ctx_len, width), lambda h, b: (b, h)),
            pl.BlockSpec((ctx_len, width), lambda h, b: (b, groups + h)),
            pl.BlockSpec(tab_block, lambda h, b: (h, 0, 0, 0)),
            pl.BlockSpec((crows, d), lambda h, b: (0, 0)),
            pl.BlockSpec(memory_space=pl.ANY),
            pl.BlockSpec((1, mod_tn), lambda h, b: (0, mod_col0 // mod_tn + step(h, b))),
        ],
        out_specs=[
            pl.BlockSpec((seq, width), lambda h, b: (b, h)),
            pl.BlockSpec((crows, mod_tn), lambda h, b: (0, step(h, b))),
        ],
        out_shape=[
            jax.ShapeDtypeStruct((batch * seq, heads * HEAD_DIM), BF16),
            jax.ShapeDtypeStruct((crows, mod_cols), F32),
        ],
        scratch_shapes=[pltpu.VMEM((2, d, mod_tn), F32), pltpu.SemaphoreType.DMA((2,))],
        compiler_params=_params(("arbitrary", "arbitrary"), est),
        name="nbr_attention",
    )(qk, qk, rest, kvc, kvc, tabs, cond, ada_w, ada_b.reshape(1, -1))


HALO = V7X_BF16_SUBLANES


def _mix_body(at_ref, u_ref, b_ref, c_ref, up_ref, cp_ref, un_ref, cn_ref, cw_ref, gg_ref,
              o_ref, *, per_seq, width):
    i = pl.program_id(0)
    ts = u_ref.shape[0]
    has_prev = jnp.where(i % per_seq == 0, 0.0, 1.0).astype(F32)
    has_next = jnp.where(i % per_seq == per_seq - 1, 0.0, 1.0).astype(F32)
    g = c_ref[...].astype(F32) * u_ref[...].astype(F32)
    g_prev = (cp_ref[...].astype(F32) * up_ref[...].astype(F32))[HALO - 1:HALO, :] * has_prev
    g_next = (cn_ref[...].astype(F32) * un_ref[...].astype(F32))[0:1, :] * has_next
    row = lax.broadcasted_iota(jnp.int32, g.shape, 0)
    below = jnp.where(row == 0, g_prev, pltpu.roll(g, 1, 0))
    above = jnp.where(row == ts - 1, g_next, pltpu.roll(g, ts - 1, 0))
    conv = cw_ref[0:1, :] * below + cw_ref[1:2, :] * g + cw_ref[2:3, :] * above
    conv_o = b_ref[...].astype(F32) * conv
    o_ref[:, :width] = _rms(at_ref[...].astype(F32), gg_ref[0:1, :]).astype(o_ref.dtype)
    o_ref[:, width:] = _rms(conv_o, gg_ref[1:2, :]).astype(o_ref.dtype)


def _mix_prep(attn, p, conv_w, group_g, seq, width, col_u, ts=512):
    m = attn.shape[0]
    cu = col_u // width
    hb = ts // HALO
    n_halo = m // HALO
    est = _vmem_estimate(
        [((ts, width), BF16)] * 4 + [((HALO, width), BF16)] * 4 + [((ts, 2 * width), BF16)],
        temps=[((ts, width), F32)] * 8)
    prev_map = lambda c: (lambda i: (jnp.maximum(i * hb - 1, 0), c))
    next_map = lambda c: (lambda i: (jnp.minimum((i + 1) * hb, n_halo - 1), c))
    return pl.pallas_call(
        functools.partial(_mix_body, per_seq=seq // ts, width=width),
        grid=(m // ts,),
        in_specs=[
            pl.BlockSpec((ts, width), lambda i: (i, 0)),
            pl.BlockSpec((ts, width), lambda i: (i, cu)),
            pl.BlockSpec((ts, width), lambda i: (i, cu + 1)),
            pl.BlockSpec((ts, width), lambda i: (i, cu + 2)),
            pl.BlockSpec((HALO, width), prev_map(cu)),
            pl.BlockSpec((HALO, width), prev_map(cu + 2)),
            pl.BlockSpec((HALO, width), next_map(cu)),
            pl.BlockSpec((HALO, width), next_map(cu + 2)),
            pl.BlockSpec((3, width), lambda i: (0, 0)),
            pl.BlockSpec((2, width), lambda i: (0, 0)),
        ],
        out_specs=pl.BlockSpec((ts, 2 * width), lambda i: (i, 0)),
        out_shape=jax.ShapeDtypeStruct((m, 2 * width), BF16),
        compiler_params=_params(("arbitrary",), est),
        name="mix_prep",
    )(attn, p, p, p, p, p, p, p, conv_w, group_g.reshape(2, width))


def _rope_tables(seq):
    rows = seq // GRID_W
    quarter = HEAD_DIM // 4
    freqs = ROPE_THETA ** (-jnp.arange(quarter, dtype=F32) / quarter)
    ang_r = jnp.arange(rows, dtype=F32)[:, None] * freqs[None, :]
    ang_c = jnp.arange(GRID_W, dtype=F32)[:, None] * freqs[None, :]
    per_row = lambda a: jnp.repeat(a, GRID_W, axis=0)
    per_col = lambda a: jnp.tile(a, (rows, 1))
    cos_r, sin_r = per_row(jnp.cos(ang_r)), per_row(jnp.sin(ang_r))
    cos_c, sin_c = per_col(jnp.cos(ang_c)), per_col(jnp.sin(ang_c))
    cos = jnp.concatenate([cos_r, cos_r, cos_c, cos_c], axis=-1)
    sin = jnp.concatenate([-sin_r, sin_r, -sin_c, sin_c], axis=-1)
    return cos, sin


def _layer(x, ctx, cond, ada_w, ada_b, norm1_g, w_in, conv_w, rpb, group_g, w_out, norm2_g,
           w_gate_up, w_down, cos, sin):
    b, s, d = x.shape
    ctx_len = ctx.shape[1]
    heads = rpb.shape[0]
    na_width = heads * HEAD_DIM
    conv_width = conv_w.shape[1]
    hidden = w_down.shape[0]
    m = b * s
    n_early = 2
    early, tabs = _adaln_and_bias(cond, ada_w, ada_b, n_early * d, rpb, s // GRID_W)
    sh1, sc1 = [early[:b, None, i * d:(i + 1) * d] for i in range(n_early)]
    csh1, csc1 = [early[b:b + 1, None, i * d:(i + 1) * d] for i in range(n_early)]

    h = _norm_mod(x, norm1_g, sh1, sc1).reshape(m, d)
    hc = _norm_mod(ctx, norm1_g, csh1, csc1).reshape(b * ctx_len, d)
    qk = _inproj_qk(h, w_in, cos, sin, na_width)
    rest = _inproj_rest(h, w_in, 2 * na_width)
    kvc = _ctx_kv(hc, w_in, na_width, 2 * na_width)
    attn, late = _attention(qk, rest, kvc, tabs, cond, ada_w, ada_b, n_early * d, b, s, heads,
                            ctx_len)
    g1, sh2, sc2, g2 = [late[:b, None, i * d:(i + 1) * d] for i in range(N_MOD - n_early)]
    y = _mix_prep(attn, rest, conv_w, group_g, s, conv_width, na_width)
    x1 = _outproj(y, w_out, x.reshape(m, d), g1, s, tm=512, tn=1024, name="out_proj")
    h2 = _norm_mod(x1.reshape(b, s, d), norm2_g, sh2, sc2).reshape(m, d)
    hid = _ffn_up(h2, w_gate_up, hidden)
    return _outproj(hid, w_down, x1, g2, s, tm=512, tn=512, name="ffn_down")


def kernel(x, c, ctx, c_ctx, ada_w, ada_b, norm1_g, w_in, conv_w, rpb, group_norm_g, w_out,
           norm2_g, w_gate_up, w_down, final_norm_g):
    b, s, d = x.shape
    depth = ada_w.shape[0]
    assert depth == 1, "context-stream update between layers is not implemented"
    assert conv_w.shape[2] == rpb.shape[1] * HEAD_DIM, "equal-width branches expected"
    cos, sin = _rope_tables(s)
    pad = (-(b + 1)) % 8
    cond = jnp.concatenate([c, c_ctx[None, :], jnp.zeros((pad, d), F32)], axis=0)
    x2 = _layer(x, ctx, cond, ada_w[0], ada_b[0], norm1_g[0], w_in[0], conv_w[0], rpb[0],
                group_norm_g[0], w_out[0], norm2_g[0], w_gate_up[0], w_down[0], cos, sin)
    return _final_norm(x2, final_norm_g).reshape(b, s, d)
```

```python
import functools

import jax
import jax.numpy as jnp
import numpy as np
from jax import lax
from jax.experimental import pallas as pl
from jax.experimental.pallas import tpu as pltpu

F32 = jnp.float32
BF16 = jnp.bfloat16

GRID_W = 64
HEAD_DIM = 128
NA_KH = 8
NA_KW = 16
ROPE_THETA = 10000.0
EPS = 1e-6
LOG2E = 1.4426950408889634
N_MOD = 6
RPB_ROWS = 2 * NA_KH - 1
RPB_COLS = 2 * NA_KW - 1

V7X_VMEM_BYTES = 64 * 1024 * 1024
V7X_LANES = 128
V7X_BF16_SUBLANES = 16
VMEM_REQUEST_CAP = 60000 * 1024

ATT_QROWS = 4
ATT_KROWS = ATT_QROWS + NA_KH
ATT_QB = ATT_QROWS * GRID_W
ATT_KB = ATT_KROWS * GRID_W
ATT_HEADS_PER_STEP = 2


def _params(semantics, vmem_bytes, keep_operands_in_hbm=False):
    request = VMEM_REQUEST_CAP if keep_operands_in_hbm else min(vmem_bytes, VMEM_REQUEST_CAP)
    return pltpu.CompilerParams(dimension_semantics=semantics, vmem_limit_bytes=int(request))


def _nbytes(shape, dtype):
    return int(np.prod(shape)) * jnp.dtype(dtype).itemsize


def _vmem_estimate(blocks, scratch=(), temps=()):
    total = sum(2 * _nbytes(s, d) for s, d in blocks)
    total += sum(_nbytes(s, d) for s, d in scratch)
    total += sum(_nbytes(s, d) for s, d in temps)
    return total + 2 * 1024 * 1024


def _silu(x):
    return x * (1.0 / (1.0 + jnp.exp(-x)))


def _adaln_body(c_ref, w_ref, b_ref, o_ref):
    s = _silu(c_ref[...]).astype(BF16)
    w = w_ref[...].astype(BF16)
    o_ref[...] = jnp.dot(s, w, preferred_element_type=F32) + b_ref[...]


def _rms(x, g):
    return x * lax.rsqrt(jnp.mean(x * x, axis=-1, keepdims=True) + EPS) * g


def _lane_blocks(width):
    return [slice(c * V7X_LANES, (c + 1) * V7X_LANES) for c in range(width // V7X_LANES)]


NORM_ROWS = 128


def _rms_rows(load, store, n_rows, width):
    for r in range(n_rows // NORM_ROWS):
        rows = slice(r * NORM_ROWS, (r + 1) * NORM_ROWS)
        part = None
        for cols in _lane_blocks(width):
            xc = load(rows, cols)
            part = xc * xc if part is None else part + xc * xc
        mean = jnp.sum(part, axis=-1, keepdims=True) / width
        inv = jnp.broadcast_to(lax.rsqrt(mean + EPS), (NORM_ROWS, V7X_LANES))
        for cols in _lane_blocks(width):
            store(rows, cols, load(rows, cols) * inv)


def _norm_mod_body(x_ref, g_ref, sh_ref, sc_ref, o_ref):
    gain = g_ref[...] * (1.0 + sc_ref[0])
    shift = sh_ref[0]

    def store(rows, cols, xn):
        o_ref[0, rows, cols] = (xn * gain[:, cols] + shift[:, cols]).astype(o_ref.dtype)

    _rms_rows(lambda rows, cols: x_ref[0, rows, cols], store, x_ref.shape[1], x_ref.shape[2])


def _norm_mod(x3, g, shift, scale, ts=1024):
    b, s, d = x3.shape
    ts = min(ts, s)
    per_batch = shift.shape[0] == b
    mod_map = (lambda bi, si: (bi, 0, 0)) if per_batch else (lambda bi, si: (0, 0, 0))
    est = _vmem_estimate(
        [((ts, d), F32), ((ts, d), BF16)], temps=[((NORM_ROWS, d), F32)] * 4)
    return pl.pallas_call(
        _norm_mod_body,
        grid=(b, s // ts),
        in_specs=[
            pl.BlockSpec((1, ts, d), lambda bi, si: (bi, si, 0)),
            pl.BlockSpec((1, d), lambda bi, si: (0, 0)),
            pl.BlockSpec((1, 1, d), mod_map),
            pl.BlockSpec((1, 1, d), mod_map),
        ],
        out_specs=pl.BlockSpec((1, ts, d), lambda bi, si: (bi, si, 0)),
        out_shape=jax.ShapeDtypeStruct((b, s, d), BF16),
        compiler_params=_params(("arbitrary", "arbitrary"), est, keep_operands_in_hbm=True),
        name="norm_mod",
    )(x3, g.reshape(1, d), shift, scale)


def _final_body(x_ref, g_ref, o_ref):
    gain = g_ref[...]

    def store(rows, cols, xn):
        o_ref[rows, cols] = xn * gain[:, cols]

    _rms_rows(lambda rows, cols: x_ref[rows, cols], store, x_ref.shape[0], x_ref.shape[1])


def _final_norm(x2, g, ts=512):
    m, d = x2.shape
    est = _vmem_estimate([((ts, d), F32)] * 2, temps=[((ts, d), F32)] * 3)
    return pl.pallas_call(
        _final_body,
        grid=(m // ts,),
        in_specs=[
            pl.BlockSpec((ts, d), lambda i: (i, 0)),
            pl.BlockSpec((1, d), lambda i: (0, 0)),
        ],
        out_specs=pl.BlockSpec((ts, d), lambda i: (i, 0)),
        out_shape=jax.ShapeDtypeStruct((m, d), F32),
        compiler_params=_params(("arbitrary",), est),
        name="final_norm",
    )(x2, g.reshape(1, d))


def _ws_body(*refs, n_extra, nj, ni, kc, segs, epilogue, row_splits):
    a_ref, w_hbm = refs[0], refs[1]
    extra = refs[2:2 + n_extra]
    o_ref = refs[2 + n_extra]
    wbf_even, wbf_odd, stage, sem = refs[3 + n_extra:]
    j = pl.program_id(0)
    i = pl.program_id(1)
    t = j * ni + i

    def aligned(v, multiple):
        return v if isinstance(v, int) else pl.multiple_of(v, multiple)

    def chunk_copies(tile, chunk, slot):
        row0 = aligned(chunk * kc, kc)
        return [
            pltpu.make_async_copy(
                w_hbm.at[pl.ds(row0, kc), pl.ds(aligned(col_fn(tile), V7X_LANES), width)],
                stage.at[slot, :, pl.ds(dst, width)],
                sem.at[slot])
            for col_fn, width, dst in segs
        ]

    def start(tile, chunk, slot):
        for cp in chunk_copies(tile, chunk, slot):
            cp.start()

    def wait(tile, chunk, slot):
        for cp in chunk_copies(tile, chunk, slot):
            cp.wait()

    def round_chunk(chunk, slot, dst_ref):
        row0 = aligned(chunk * kc, kc)
        dst_ref[pl.ds(row0, kc), :] = stage[slot].astype(BF16)

    @pl.when(t == 0)
    def _():
        start(0, 0, 0)
        for c in range(ni):
            if c + 1 < ni:
                start(0, c + 1, (c + 1) % 2)
            wait(0, c, c % 2)
            round_chunk(c, c % 2, wbf_even)
        start(min(1, nj - 1), 0, ni % 2)

    nxt = jnp.minimum(j + 1, nj - 1)
    slot = (t + ni) % 2
    wait(nxt, i, slot)

    @pl.when(t + 1 < nj * ni)
    def _():
        wrap = i + 1 == ni
        tile_next = jnp.where(wrap, jnp.minimum(j + 2, nj - 1), nxt)
        chunk_next = jnp.where(wrap, 0, i + 1)
        start(tile_next, chunk_next, 1 - slot)

    def compute(cur_ref, idle_ref):
        tm = a_ref.shape[0]
        for r in range(row_splits):
            rows = slice(r * tm // row_splits, (r + 1) * tm // row_splits)
            acc = jnp.dot(a_ref[rows, :], cur_ref[...], preferred_element_type=F32)
            epilogue(acc, rows, j, i, extra, o_ref)
        round_chunk(i, slot, idle_ref)

    @pl.when(j % 2 == 0)
    def _():
        compute(wbf_even, wbf_odd)

    @pl.when(j % 2 == 1)
    def _():
        compute(wbf_odd, wbf_even)


def _ws_matmul(a, w, segs, nj, tm, out_shape, out_dtype, out_tn, epilogue, extra=(),
               extra_specs=(), temps=(), row_splits=1, name="ws_matmul"):
    m, k = a.shape
    ni = m // tm
    assert ni * tm == m and k % ni == 0
    kc = k // ni
    assert kc % V7X_BF16_SUBLANES == 0 and tm % (row_splits * V7X_BF16_SUBLANES) == 0
    tn = sum(width for _, width, _ in segs)
    body = functools.partial(_ws_body, n_extra=len(extra), nj=nj, ni=ni, kc=kc, segs=segs,
                             epilogue=epilogue, row_splits=row_splits)
    blocks = [((tm, k), BF16), ((tm, out_tn), out_dtype)]
    blocks += [(spec.block_shape, arr.dtype) for spec, arr in zip(extra_specs, extra)]
    est = _vmem_estimate(
        blocks, scratch=[((k, tn), BF16), ((k, tn), BF16), ((2, kc, tn), F32)],
        temps=[((tm, tn), F32), ((tm, tn), F32), ((kc, tn), F32)] + list(temps))
    return pl.pallas_call(
        body,
        grid=(nj, ni),
        in_specs=[pl.BlockSpec((tm, k), lambda j, i: (i, 0)),
                  pl.BlockSpec(memory_space=pl.ANY)] + list(extra_specs),
        out_specs=pl.BlockSpec((tm, out_tn), lambda j, i: (i, j)),
        out_shape=jax.ShapeDtypeStruct(out_shape, out_dtype),
        scratch_shapes=[pltpu.VMEM((k, tn), BF16), pltpu.VMEM((k, tn), BF16),
                        pltpu.VMEM((2, kc, tn), F32), pltpu.SemaphoreType.DMA((2,))],
        compiler_params=_params(("arbitrary", "arbitrary"), est),
        name=name,
    )(a, w, *extra)


def _rope_tile(acc, cos, sin, factor):
    lane = lax.broadcasted_iota(jnp.int32, (acc.shape[0], HEAD_DIM), 1)
    low = (lane & 32) == 0
    outs = []
    for c in range(acc.shape[1] // HEAD_DIM):
        xc = acc[:, c * HEAD_DIM:(c + 1) * HEAD_DIM]
        partner = jnp.where(low, pltpu.roll(xc, HEAD_DIM - 32, 1), pltpu.roll(xc, 32, 1))
        outs.append((xc * cos + partner * sin) * factor)
    return jnp.concatenate(outs, axis=1)


def _qk_epilogue(acc, rows, j, i, extra, o_ref, *, n_q_tiles, q_scale, tiles_per_seq):
    cos_ref, sin_ref = extra
    n = rows.stop - rows.start
    tm = o_ref.shape[0]
    pos = pl.ds(pl.multiple_of((i % tiles_per_seq) * tm + rows.start, n), n)
    factor = jnp.where(j < n_q_tiles, jnp.float32(q_scale), jnp.float32(1.0))
    roped = _rope_tile(acc, cos_ref[pos, :], sin_ref[pos, :], factor)
    o_ref[rows, :] = roped.astype(o_ref.dtype)


def _cast_epilogue(acc, rows, j, i, extra, o_ref):
    o_ref[rows, :] = acc.astype(o_ref.dtype)


def _residual_epilogue(acc, rows, j, i, extra, o_ref, *, tiles_per_seq):
    x_ref, g_ref = extra
    o_ref[rows, :] = x_ref[rows, :] + g_ref[i // tiles_per_seq, j] * acc


def _swiglu_epilogue(acc, rows, j, i, extra, o_ref, *, tg):
    o_ref[rows, :] = (_silu(acc[:, :tg]) * acc[:, tg:]).astype(o_ref.dtype)


def _inproj_qk(a, w, cos, sin, na_width, tm=1024, tn=1024):
    m = a.shape[0]
    seq = cos.shape[0]
    table_spec = pl.BlockSpec((seq, HEAD_DIM), lambda j, i: (0, 0))
    return _ws_matmul(
        a, w, [(lambda tile: tile * tn, tn, 0)], 2 * na_width // tn, tm,
        (m, 2 * na_width), BF16, tn,
        functools.partial(_qk_epilogue, n_q_tiles=na_width // tn,
                          q_scale=LOG2E * HEAD_DIM ** -0.5, tiles_per_seq=seq // tm),
        extra=(cos, sin), extra_specs=(table_spec, table_spec),
        temps=[((tm, tn), F32)] * 2, row_splits=2, name="in_proj_qk")


def _inproj_rest(a, w, col0, tm=1024, tn=1024):
    m = a.shape[0]
    ncols = w.shape[1] - col0
    return _ws_matmul(
        a, w, [(lambda tile: col0 + tile * tn, tn, 0)], ncols // tn, tm,
        (m, ncols), BF16, tn, _cast_epilogue, row_splits=2, name="in_proj_rest")


def _outproj(a, w, x, gate, seq, tm, tn, row_splits, name):
    m = a.shape[0]
    n = w.shape[1]
    batch = gate.shape[0]
    gate_tiles = gate.reshape(batch, n // tn, 1, tn)
    return _ws_matmul(
        a, w, [(lambda tile: tile * tn, tn, 0)], n // tn, tm, (m, n), F32, tn,
        functools.partial(_residual_epilogue, tiles_per_seq=seq // tm), extra=(x, gate_tiles),
        extra_specs=(pl.BlockSpec((tm, tn), lambda j, i: (i, j)),
                     pl.BlockSpec(gate_tiles.shape, lambda j, i: (0, 0, 0, 0))),
        row_splits=row_splits, name=name)


def _ffn_up(a, w, hidden, tm=2048, tg=256):
    m = a.shape[0]
    segs = [(lambda tile: tile * tg, tg, 0), (lambda tile: hidden + tile * tg, tg, tg)]
    return _ws_matmul(
        a, w, segs, hidden // tg, tm, (m, hidden), BF16, tg,
        functools.partial(_swiglu_epilogue, tg=tg), temps=[((tm, tg), F32)], row_splits=4,
        name="ffn_up")


def _plain_mm_body(a_ref, w_ref, o_ref, wbf_ref):
    @pl.when(pl.program_id(1) == 0)
    def _():
        wbf_ref[...] = w_ref[...].astype(BF16)

    o_ref[...] = jnp.dot(a_ref[...], wbf_ref[...], preferred_element_type=F32).astype(o_ref.dtype)


def _ctx_kv(a, w, col0, ncols, tm=1024, tn=512):
    m, k = a.shape
    off = col0 // tn
    est = _vmem_estimate(
        [((tm, k), BF16), ((k, tn), F32), ((tm, tn), BF16)],
        scratch=[((k, tn), BF16)], temps=[((tm, tn), F32)])
    return pl.pallas_call(
        _plain_mm_body,
        grid=(ncols // tn, m // tm),
        in_specs=[
            pl.BlockSpec((tm, k), lambda j, i: (i, 0)),
            pl.BlockSpec((k, tn), lambda j, i: (0, j + off)),
        ],
        out_specs=pl.BlockSpec((tm, tn), lambda j, i: (i, j)),
        out_shape=jax.ShapeDtypeStruct((m, ncols), BF16),
        scratch_shapes=[pltpu.VMEM((k, tn), BF16)],
        compiler_params=_params(("arbitrary", "arbitrary"), est, keep_operands_in_hbm=True),
        name="ctx_kv_proj",
    )(a, w)


def _block_classes(rows):
    last_ws = rows - ATT_KROWS
    first = (0, [0] * ATT_QROWS)
    interior = (NA_KH // 2, list(range(ATT_QROWS)))
    r0 = rows - ATT_QROWS
    last = (r0 - last_ws, [rows - NA_KH - last_ws] * ATT_QROWS)
    return [first, interior, last]


def _bias_body(rpb_ref, o_ref, *, classes):
    base = pl.program_id(0) * (RPB_ROWS * RPB_COLS)
    shape = (GRID_W, 2 * GRID_W)
    lane = lax.broadcasted_iota(jnp.int32, shape, 1)
    qc = lax.broadcasted_iota(jnp.int32, shape, 0)
    kc = lane & (GRID_W - 1)
    right = lane >= GRID_W
    dc_idx = kc - qc + (NA_KW - 1)
    cs = jnp.clip(qc - NA_KW // 2, 0, GRID_W - NA_KW)
    col_ok = (kc >= cs) & (kc < cs + NA_KW)
    neg = jnp.full(shape, -jnp.inf, F32)

    pair = []
    for d in range(-1, RPB_ROWS):
        dl = min(max(d, 0), RPB_ROWS - 1)
        dr = min(max(d + 1, 0), RPB_ROWS - 1)
        acc = jnp.zeros(shape, F32)
        for dc in range(RPB_COLS):
            val = jnp.where(right, rpb_ref[base + dr * RPB_COLS + dc] * LOG2E,
                            rpb_ref[base + dl * RPB_COLS + dc] * LOG2E)
            acc = jnp.where(dc_idx == dc, val, acc)
        pair.append(acc)

    for ci, (off, rs_rel) in enumerate(classes):
        for i in range(ATT_QROWS):
            for jp in range(ATT_KROWS // 2):
                jl, jr = 2 * jp, 2 * jp + 1
                ok_l = rs_rel[i] <= jl < rs_rel[i] + NA_KH
                ok_r = rs_rel[i] <= jr < rs_rel[i] + NA_KH
                d = jl - i - off + (NA_KH - 1)
                if ok_l and ok_r:
                    blk = jnp.where(col_ok, pair[d + 1], neg)
                elif ok_l:
                    blk = jnp.where(col_ok & jnp.logical_not(right), pair[d + 1], neg)
                elif ok_r:
                    blk = jnp.where(col_ok & right, pair[d + 1], neg)
                else:
                    blk = neg
                o_ref[0, ci, i * GRID_W:(i + 1) * GRID_W,
                      jp * 2 * GRID_W:(jp + 1) * 2 * GRID_W] = blk


def _adaln_bias_body(c_ref, w_ref, b_ref, rpb_ref, o_ref, tab_ref, *, classes):
    _adaln_body(c_ref, w_ref, b_ref, o_ref)
    _bias_body(rpb_ref, tab_ref, classes=classes)


def _adaln_and_bias(cond, w, b, n, rpb, rows):
    crows, d = cond.shape
    heads = rpb.shape[0]
    tn = n // heads
    assert tn * heads == n and tn % V7X_LANES == 0
    classes = _block_classes(rows)
    tab_block = (1, len(classes), ATT_QB, ATT_KB)
    est = _vmem_estimate(
        [((crows, d), F32), ((d, tn), F32), ((1, tn), F32), ((crows, tn), F32), (tab_block, F32)],
        temps=[((d, tn), BF16)] + [((GRID_W, 2 * GRID_W), F32)] * 32)
    return pl.pallas_call(
        functools.partial(_adaln_bias_body, classes=classes),
        grid=(heads,),
        in_specs=[
            pl.BlockSpec((crows, d), lambda j: (0, 0)),
            pl.BlockSpec((d, tn), lambda j: (0, j)),
            pl.BlockSpec((1, tn), lambda j: (0, j)),
            pl.BlockSpec(memory_space=pltpu.SMEM),
        ],
        out_specs=[
            pl.BlockSpec((crows, tn), lambda j: (0, j)),
            pl.BlockSpec(tab_block, lambda j: (j, 0, 0, 0)),
        ],
        out_shape=[
            jax.ShapeDtypeStruct((crows, n), F32),
            jax.ShapeDtypeStruct((heads,) + tab_block[1:], F32),
        ],
        compiler_params=_params(("arbitrary",), est, keep_operands_in_hbm=True),
        name="adaln_bias",
    )(cond, w, b.reshape(1, -1), rpb.reshape(-1))


def _attn_body(q_ref, k_ref, v_ref, kc_ref, vc_ref, tab_ref, cond_ref, adaw_hbm, adab_ref,
               o_ref, mod_ref, wstage, sem, *, rows, batch, n_steps, mod_col0, mod_tn):
    step = pl.program_id(0) * batch + pl.program_id(1)
    slot = step % 2

    def chunk_copy(s, sl):
        col = mod_col0 + s * mod_tn
        col = col if isinstance(col, int) else pl.multiple_of(col, mod_tn)
        return pltpu.make_async_copy(
            adaw_hbm.at[pl.ds(0, wstage.shape[1]), pl.ds(col, mod_tn)], wstage.at[sl], sem.at[sl])

    @pl.when(step == 0)
    def _():
        chunk_copy(0, 0).start()

    chunk_copy(step, slot).wait()

    @pl.when(step + 1 < n_steps)
    def _():
        chunk_copy(step + 1, 1 - slot).start()

    mod_ref[...] = jnp.dot(_silu(cond_ref[...]).astype(BF16), wstage[slot].astype(BF16),
                           preferred_element_type=F32) + adab_ref[...]

    nt = (((1,), (1,)), ((), ()))
    n_blocks = rows // ATT_QROWS
    n_heads = q_ref.shape[1] // HEAD_DIM

    def window(item):
        head, blk = divmod(item, n_blocks)
        r0 = blk * ATT_QROWS
        ws = min(max(r0 - NA_KH // 2, 0), rows - ATT_KROWS)
        return r0 * GRID_W, ws * GRID_W, slice(head * HEAD_DIM, (head + 1) * HEAD_DIM)

    def scores(item):
        head, blk = divmod(item, n_blocks)
        q0, k0, lanes = window(item)
        cls = 0 if blk == 0 else (2 if blk == n_blocks - 1 else 1)
        q = q_ref[q0:q0 + ATT_QB, lanes]
        s_lat = lax.dot_general(q, k_ref[k0:k0 + ATT_KB, lanes], nt, preferred_element_type=F32)
        s_ctx = lax.dot_general(q, kc_ref[:, lanes], nt, preferred_element_type=F32)
        return s_lat + tab_ref[head, cls], s_ctx

    def probs(s_lat, s_ctx):
        mx = jnp.maximum(jnp.max(s_lat, axis=-1, keepdims=True),
                         jnp.max(s_ctx, axis=-1, keepdims=True))
        e_lat = jnp.exp2(s_lat - mx)
        e_ctx = jnp.exp2(s_ctx - mx)
        denom = jnp.sum(e_lat, axis=-1, keepdims=True) + jnp.sum(e_ctx, axis=-1, keepdims=True)
        return e_lat.astype(BF16), e_ctx.astype(BF16), denom

    def finish(item, e_lat, e_ctx, denom):
        q0, k0, lanes = window(item)
        o = (jnp.dot(e_lat, v_ref[k0:k0 + ATT_KB, lanes], preferred_element_type=F32)
             + jnp.dot(e_ctx, vc_ref[:, lanes], preferred_element_type=F32))
        o_ref[q0:q0 + ATT_QB, lanes] = (o / denom).astype(o_ref.dtype)

    n_items = n_heads * n_blocks
    nxt = scores(0)
    pending = None
    for item in range(n_items):
        cur = nxt
        if item + 1 < n_items:
            nxt = scores(item + 1)
        p = probs(*cur)
        if pending is not None:
            finish(item - 1, *pending)
        pending = p
    finish(n_items - 1, *pending)


def _attention(qk, rest, kvc, tabs, cond, ada_w, ada_b, mod_col0, batch, seq, heads, ctx_len):
    rows = seq // GRID_W
    hp = ATT_HEADS_PER_STEP
    groups = heads // hp
    assert groups * hp == heads
    width = hp * HEAD_DIM
    tab_block = (hp,) + tabs.shape[1:]
    n_steps = groups * batch
    crows, d = cond.shape
    mod_cols = ada_w.shape[1] - mod_col0
    mod_tn = mod_cols // n_steps
    assert mod_tn * n_steps == mod_cols and mod_tn % V7X_LANES == 0 and mod_col0 % mod_tn == 0
    est = _vmem_estimate(
        [((seq, width), BF16)] * 4 + [((ctx_len, width), BF16)] * 2 + [(tab_block, F32)],
        scratch=[((2, d, mod_tn), F32)],
        temps=[((ATT_QB, ATT_KB + ctx_len), F32)] * 12 + [((d, mod_tn), F32)])
    step = lambda h, b: h * batch + b
    return pl.pallas_call(
        functools.partial(_attn_body, rows=rows, batch=batch, n_steps=n_steps,
                          mod_col0=mod_col0, mod_tn=mod_tn),
        grid=(groups, batch),
        in_specs=[
            pl.BlockSpec((seq, width), lambda h, b: (b, h)),
            pl.BlockSpec((seq, width), lambda h, b: (b, groups + h)),
            pl.BlockSpec((seq, width), lambda h, b: (b, h)),
            pl.BlockSpec((ctx_len, width), lambda h, b: (b, h)),
            pl.BlockSpec((ctx_len, width), lambda h, b: (b, groups + h)),
            pl.BlockSpec(tab_block, lambda h, b: (h, 0, 0, 0)),
            pl.BlockSpec((crows, d), lambda h, b: (0, 0)),
            pl.BlockSpec(memory_space=pl.ANY),
            pl.BlockSpec((1, mod_tn), lambda h, b: (0, mod_col0 // mod_tn + step(h, b))),
        ],
        out_specs=[
            pl.BlockSpec((seq, width), lambda h, b: (b, h)),
            pl.BlockSpec((crows, mod_tn), lambda h, b: (0, step(h, b))),
        ],
        out_shape=[
            jax.ShapeDtypeStruct((batch * seq, heads * HEAD_DIM), BF16),
            jax.ShapeDtypeStruct((crows, mod_cols), F32),
        ],
        scratch_shapes=[pltpu.VMEM((2, d, mod_tn), F32), pltpu.SemaphoreType.DMA((2,))],
        compiler_params=_params(("arbitrary", "arbitrary"), est),
        name="nbr_attention",
    )(qk, qk, rest, kvc, kvc, tabs, cond, ada_w, ada_b.reshape(1, -1))


HALO = V7X_BF16_SUBLANES


def _mix_body(at_ref, u_ref, b_ref, c_ref, up_ref, cp_ref, un_ref, cn_ref, cw_ref, gg_ref,
              o_ref, *, per_seq, width):
    i = pl.program_id(0)
    ts = u_ref.shape[0]
    has_prev = jnp.where(i % per_seq == 0, 0.0, 1.0).astype(F32)
    has_next = jnp.where(i % per_seq == per_seq - 1, 0.0, 1.0).astype(F32)
    g = c_ref[...].astype(F32) * u_ref[...].astype(F32)
    g_prev = (cp_ref[...].astype(F32) * up_ref[...].astype(F32))[HALO - 1:HALO, :] * has_prev
    g_next = (cn_ref[...].astype(F32) * un_ref[...].astype(F32))[0:1, :] * has_next
    row = lax.broadcasted_iota(jnp.int32, g.shape, 0)
    below = jnp.where(row == 0, g_prev, pltpu.roll(g, 1, 0))
    above = jnp.where(row == ts - 1, g_next, pltpu.roll(g, ts - 1, 0))
    conv = cw_ref[0:1, :] * below + cw_ref[1:2, :] * g + cw_ref[2:3, :] * above
    conv_o = b_ref[...].astype(F32) * conv
    o_ref[:, :width] = _rms(at_ref[...].astype(F32), gg_ref[0:1, :]).astype(o_ref.dtype)
    o_ref[:, width:] = _rms(conv_o, gg_ref[1:2, :]).astype(o_ref.dtype)


def _mix_prep(attn, p, conv_w, group_g, seq, width, col_u, ts=512):
    m = attn.shape[0]
    cu = col_u // width
    hb = ts // HALO
    n_halo = m // HALO
    est = _vmem_estimate(
        [((ts, width), BF16)] * 4 + [((HALO, width), BF16)] * 4 + [((ts, 2 * width), BF16)],
        temps=[((ts, width), F32)] * 8)
    prev_map = lambda c: (lambda i: (jnp.maximum(i * hb - 1, 0), c))
    next_map = lambda c: (lambda i: (jnp.minimum((i + 1) * hb, n_halo - 1), c))
    return pl.pallas_call(
        functools.partial(_mix_body, per_seq=seq // ts, width=width),
        grid=(m // ts,),
        in_specs=[
            pl.BlockSpec((ts, width), lambda i: (i, 0)),
            pl.BlockSpec((ts, width), lambda i: (i, cu)),
            pl.BlockSpec((ts, width), lambda i: (i, cu + 1)),
            pl.BlockSpec((ts, width), lambda i: (i, cu + 2)),
            pl.BlockSpec((HALO, width), prev_map(cu)),
            pl.BlockSpec((HALO, width), prev_map(cu + 2)),
            pl.BlockSpec((HALO, width), next_map(cu)),
            pl.BlockSpec((HALO, width), next_map(cu + 2)),
            pl.BlockSpec((3, width), lambda i: (0, 0)),
            pl.BlockSpec((2, width), lambda i: (0, 0)),
        ],
        out_specs=pl.BlockSpec((ts, 2 * width), lambda i: (i, 0)),
        out_shape=jax.ShapeDtypeStruct((m, 2 * width), BF16),
        compiler_params=_params(("arbitrary",), est),
        name="mix_prep",
    )(attn, p, p, p, p, p, p, p, conv_w, group_g.reshape(2, width))


def _rope_tables(seq):
    rows = seq // GRID_W
    quarter = HEAD_DIM // 4
    freqs = ROPE_THETA ** (-jnp.arange(quarter, dtype=F32) / quarter)
    ang_r = jnp.arange(rows, dtype=F32)[:, None] * freqs[None, :]
    ang_c = jnp.arange(GRID_W, dtype=F32)[:, None] * freqs[None, :]
    per_row = lambda a: jnp.repeat(a, GRID_W, axis=0)
    per_col = lambda a: jnp.tile(a, (rows, 1))
    cos_r, sin_r = per_row(jnp.cos(ang_r)), per_row(jnp.sin(ang_r))
    cos_c, sin_c = per_col(jnp.cos(ang_c)), per_col(jnp.sin(ang_c))
    cos = jnp.concatenate([cos_r, cos_r, cos_c, cos_c], axis=-1)
    sin = jnp.concatenate([-sin_r, sin_r, -sin_c, sin_c], axis=-1)
    return cos, sin


def _layer(x, ctx, cond, ada_w, ada_b, norm1_g, w_in, conv_w, rpb, group_g, w_out, norm2_g,
           w_gate_up, w_down, cos, sin):
    b, s, d = x.shape
    ctx_len = ctx.shape[1]
    heads = rpb.shape[0]
    na_width = heads * HEAD_DIM
    conv_width = conv_w.shape[1]
    hidden = w_down.shape[0]
    m = b * s
    n_early = 2
    early, tabs = _adaln_and_bias(cond, ada_w, ada_b, n_early * d, rpb, s // GRID_W)
    sh1, sc1 = [early[:b, None, i * d:(i + 1) * d] for i in range(n_early)]
    csh1, csc1 = [early[b:b + 1, None, i * d:(i + 1) * d] for i in range(n_early)]

    h = _norm_mod(x, norm1_g, sh1, sc1).reshape(m, d)
    hc = _norm_mod(ctx, norm1_g, csh1, csc1).reshape(b * ctx_len, d)
    qk = _inproj_qk(h, w_in, cos, sin, na_width)
    rest = _inproj_rest(h, w_in, 2 * na_width)
    kvc = _ctx_kv(hc, w_in, na_width, 2 * na_width)
    attn, late = _attention(qk, rest, kvc, tabs, cond, ada_w, ada_b, n_early * d, b, s, heads,
                            ctx_len)
    g1, sh2, sc2, g2 = [late[:b, None, i * d:(i + 1) * d] for i in range(N_MOD - n_early)]
    y = _mix_prep(attn, rest, conv_w, group_g, s, conv_width, na_width)
    x1 = _outproj(y, w_out, x.reshape(m, d), g1, s, tm=1024, tn=512, row_splits=2,
                  name="out_proj")
    h2 = _norm_mod(x1.reshape(b, s, d), norm2_g, sh2, sc2).reshape(m, d)
    hid = _ffn_up(h2, w_gate_up, hidden)
    return _outproj(hid, w_down, x1, g2, s, tm=512, tn=512, row_splits=1, name="ffn_down")


def kernel(x, c, ctx, c_ctx, ada_w, ada_b, norm1_g, w_in, conv_w, rpb, group_norm_g, w_out,
           norm2_g, w_gate_up, w_down, final_norm_g):
    b, s, d = x.shape
    depth = ada_w.shape[0]
    assert depth == 1, "context-stream update between layers is not implemented"
    assert conv_w.shape[2] == rpb.shape[1] * HEAD_DIM, "equal-width branches expected"
    cos, sin = _rope_tables(s)
    pad = (-(b + 1)) % 8
    cond = jnp.concatenate([c, c_ctx[None, :], jnp.zeros((pad, d), F32)], axis=0)
    x2 = _layer(x, ctx, cond, ada_w[0], ada_b[0], norm1_g[0], w_in[0], conv_w[0], rpb[0],
                group_norm_g[0], w_out[0], norm2_g[0], w_gate_up[0], w_down[0], cos, sin)
    return _final_norm(x2, final_norm_g).reshape(b, s, d)
```
